```python
import math
import jax, jax.numpy as jnp
from jax import lax
import numpy as np

D_MODEL = 2048
BATCH = 1
SEQ = 8192
DEPTH = 1

RW_HEADS = 16
RW_HEAD_DIM = 64
RW_WIDTH = RW_HEADS * RW_HEAD_DIM
RW_DECAY_LORA = 64
RW_ICLR_LORA = 64
RW_GATE_LORA = 160
FOX_HEADS = 16
FOX_HEAD_DIM = 64
FOX_WIDTH = FOX_HEADS * FOX_HEAD_DIM
Q_BLOCK = 128
N_MEM = 256
X_HEADS = 4
X_HEAD_DIM = 128
X_WIDTH = X_HEADS * X_HEAD_DIM
D_FF = 4 * D_MODEL
NORM_EPS = 1e-5
GN_EPS = 64e-5

RW_COLS = 3 * RW_WIDTH + RW_DECAY_LORA + RW_ICLR_LORA + RW_GATE_LORA
FOX_COLS = 3 * FOX_WIDTH + FOX_HEADS
GATE_COLS = 2 * D_MODEL
N_IN = RW_COLS + FOX_COLS + GATE_COLS

kernel_name = 'hybrid_rwkv7_fox_gated_block'


def rmsnorm(x, g, eps=NORM_EPS):
    xf = x.astype(jnp.float32)
    y = xf * lax.rsqrt(jnp.mean(xf * xf, axis=-1, keepdims=True) + eps)
    return (y * g.astype(jnp.float32)).astype(x.dtype)


def rwkv7_scan(r, w, k, v, a, b):
    Bn, Sn, H, N = r.shape
    xs = tuple(jnp.moveaxis(t.astype(jnp.float32), 1, 0) for t in (r, w, k, v, a, b))

    def step(state, inp):
        r_t, w_t, k_t, v_t, a_t, b_t = inp
        sa = jnp.einsum('bhij,bhj->bhi', state, a_t)
        state = (state * w_t[:, :, None, :] + sa[..., None] * b_t[:, :, None, :]
                 + v_t[..., None] * k_t[:, :, None, :])
        y = jnp.einsum('bhij,bhj->bhi', state, r_t)
        return state, y

    s0 = jnp.zeros((Bn, H, N, N), jnp.float32)
    _, ys = lax.scan(step, s0, xs)
    return jnp.moveaxis(ys, 0, 1)


def rwkv7_branch(p, mu, w0, w_up, a0, a_up, g_up, k_k, k_a, r_k, ln_w, ln_b):
    Bn, Sn, _ = p.shape
    H, N, W = RW_HEADS, RW_HEAD_DIM, RW_WIDTH
    p_prev = jnp.pad(p, ((0, 0), (1, 0), (0, 0)))[:, :-1]
    p = p + (p_prev - p) * mu
    r, k, v, wd, ad, gd = jnp.split(
        p, [W, 2 * W, 3 * W, 3 * W + RW_DECAY_LORA, 3 * W + RW_DECAY_LORA + RW_ICLR_LORA], axis=-1)
    w = -jax.nn.softplus(-(w0 + jnp.tanh(wd) @ w_up)) - 0.5
    decay = jnp.exp(-jnp.exp(w.astype(jnp.float32)))
    iclr = jax.nn.sigmoid(a0 + ad @ a_up)
    g = jax.nn.sigmoid(gd) @ g_up
    heads = lambda t: t.reshape(Bn, Sn, H, N)
    kk = heads(k * k_k).astype(jnp.float32)
    kk = kk * lax.rsqrt(jnp.maximum(jnp.sum(kk * kk, axis=-1, keepdims=True), 1e-24))
    k = k * (1.0 + (iclr - 1.0) * k_a)
    r_h, k_h, v_h, iclr_h = heads(r), heads(k), heads(v), heads(iclr)
    y = rwkv7_scan(r_h, heads(decay), k_h, v_h, -kk, kk * iclr_h)
    mean = jnp.mean(y, axis=-1, keepdims=True)
    var = jnp.mean(jnp.square(y - mean), axis=-1, keepdims=True)
    y = ((y - mean) * lax.rsqrt(var + GN_EPS)).reshape(Bn, Sn, W) * ln_w + ln_b
    bonus = jnp.sum(r_h * k_h * r_k, axis=-1, keepdims=True) * v_h
    y = y + bonus.reshape(Bn, Sn, W)
    return (y * g).astype(p.dtype)


def forgetting_attention(q, k, v, logf):
    Sn, dh = q.shape[1], q.shape[-1]
    c = jnp.transpose(jnp.cumsum(logf.astype(jnp.float32), axis=1), (0, 2, 1))
    scale = dh ** -0.5
    outs = []
    for i in range(Sn // Q_BLOCK):
        qs, qe = i * Q_BLOCK, (i + 1) * Q_BLOCK
        s = jnp.einsum('bqhd,bkhd->bhqk', q[:, qs:qe], k[:, :qe]).astype(jnp.float32) * scale
        s = s + (c[:, :, qs:qe, None] - c[:, :, None, :qe])
        causal = jnp.arange(qe)[None, :] <= jnp.arange(qs, qe)[:, None]
        s = jnp.where(causal, s, -jnp.inf)
        pr = jax.nn.softmax(s, axis=-1)
        outs.append(jnp.einsum('bhqk,bkhd->bqhd', pr.astype(v.dtype), v[:, :qe]))
    return jnp.concatenate(outs, axis=1)


def fox_branch(p, b_f, q_norm, k_norm):
    Bn, Sn, _ = p.shape
    H, N, W = FOX_HEADS, FOX_HEAD_DIM, FOX_WIDTH
    q, k, v, fz = jnp.split(p, [W, 2 * W, 3 * W], axis=-1)
    q = rmsnorm(q.reshape(Bn, Sn, H, N), q_norm)
    k = rmsnorm(k.reshape(Bn, Sn, H, N), k_norm)
    v = v.reshape(Bn, Sn, H, N)
    logf = jax.nn.log_sigmoid((fz + b_f).astype(jnp.float32))
    return forgetting_attention(q, k, v, logf).reshape(Bn, Sn, W)


def memory_cross_attention(hn, mn, w_cq, w_ckv, w_co):
    Bn, Sn, _ = hn.shape
    q = (hn @ w_cq).reshape(Bn, Sn, X_HEADS, X_HEAD_DIM)
    k, v = jnp.split(mn @ w_ckv, 2, axis=-1)
    k = k.reshape(Bn, N_MEM, X_HEADS, X_HEAD_DIM)
    v = v.reshape(Bn, N_MEM, X_HEADS, X_HEAD_DIM)
    s = jnp.einsum('bqhd,bmhd->bhqm', q, k).astype(jnp.float32) * (X_HEAD_DIM ** -0.5)
    pr = jax.nn.softmax(s, axis=-1)
    o = jnp.einsum('bhqm,bmhd->bqhd', pr.astype(v.dtype), v).reshape(Bn, Sn, X_WIDTH)
    return o @ w_co


def sqrelu_mlp(hn, w_up, w_down):
    return jnp.square(jax.nn.relu(hn @ w_up)) @ w_down


def setup_inputs(seed: int = 0) -> dict:
    key = jax.random.key(seed)
    ks = iter(jax.random.split(key, 48))
    L, D = DEPTH, D_MODEL

    def nrm(shape, scale):
        return jax.random.normal(next(ks), shape, jnp.float32) * scale

    def unif(shape, lo, hi):
        return jax.random.uniform(next(ks), shape, jnp.float32, lo, hi)

    return {
        'x': nrm((BATCH, SEQ, D), 1.0),
        'mem': nrm((BATCH, N_MEM, D), 1.0),
        'ln_mix': 1.0 + nrm((L, D), 0.02),
        'w_in': nrm((L, D, N_IN), D ** -0.5),
        'rw_mu': unif((L, RW_COLS), 0.0, 1.0),
        'rw_w0': unif((L, RW_WIDTH), -6.0, -1.0),
        'rw_w_up': nrm((L, RW_DECAY_LORA, RW_WIDTH), 0.1 * RW_DECAY_LORA ** -0.5),
        'rw_a0': nrm((L, RW_WIDTH), 0.1),
        'rw_a_up': nrm((L, RW_ICLR_LORA, RW_WIDTH), 0.1 * RW_ICLR_LORA ** -0.5),
        'rw_g_up': nrm((L, RW_GATE_LORA, RW_WIDTH), RW_GATE_LORA ** -0.5),
        'rw_k_k': 0.85 + nrm((L, RW_WIDTH), 0.02),
        'rw_k_a': 1.0 + nrm((L, RW_WIDTH), 0.02),
        'rw_r_k': nrm((L, RW_HEADS, RW_HEAD_DIM), 0.1),
        'rw_ln_w': 1.0 + nrm((L, RW_WIDTH), 0.02),
        'rw_ln_b': nrm((L, RW_WIDTH), 0.01),
        'fox_b_f': unif((L, FOX_HEADS), 1.0, 5.0),
        'fox_q_norm': 1.0 + nrm((L, FOX_HEAD_DIM), 0.02),
        'fox_k_norm': 1.0 + nrm((L, FOX_HEAD_DIM), 0.02),
        'w_proj_a': nrm((L, RW_WIDTH, D), RW_WIDTH ** -0.5),
        'w_proj_b': nrm((L, FOX_WIDTH, D), FOX_WIDTH ** -0.5),
        'w_out': nrm((L, D, D), D ** -0.5),
        'ln_cross': 1.0 + nrm((L, D), 0.02),
        'ln_mem': 1.0 + nrm((L, D), 0.02),
        'w_cq': nrm((L, D, X_WIDTH), D ** -0.5),
        'w_ckv': nrm((L, D, 2 * X_WIDTH), D ** -0.5),
        'w_co': nrm((L, X_WIDTH, D), X_WIDTH ** -0.5),
        'ln_mlp': 1.0 + nrm((L, D), 0.02),
        'w_up': nrm((L, D, D_FF), D ** -0.5),
        'w_down': nrm((L, D_FF, D), D_FF ** -0.5),
        'ln_final': 1.0 + nrm((D,), 0.02),
    }


def reference(x, mem, ln_mix, w_in, rw_mu, rw_w0, rw_w_up, rw_a0, rw_a_up, rw_g_up,
              rw_k_k, rw_k_a, rw_r_k, rw_ln_w, rw_ln_b, fox_b_f, fox_q_norm, fox_k_norm,
              w_proj_a, w_proj_b, w_out, ln_cross, ln_mem, w_cq, w_ckv, w_co,
              ln_mlp, w_up, w_down, ln_final):
    h = x
    for l in range(DEPTH):
        xn = rmsnorm(h, ln_mix[l])
        proj = xn @ w_in[l]
        p_rw = proj[..., :RW_COLS]
        p_fox = proj[..., RW_COLS:RW_COLS + FOX_COLS]
        p_gate = proj[..., RW_COLS + FOX_COLS:]
        y_a = rwkv7_branch(p_rw, rw_mu[l], rw_w0[l], rw_w_up[l], rw_a0[l], rw_a_up[l], rw_g_up[l],
                           rw_k_k[l], rw_k_a[l], rw_r_k[l], rw_ln_w[l], rw_ln_b[l])
        y_b = fox_branch(p_fox, fox_b_f[l], fox_q_norm[l], fox_k_norm[l])
        g_a, g_b = jnp.split(jax.nn.sigmoid(p_gate), 2, axis=-1)
        merged = g_a * (y_a @ w_proj_a[l]) + g_b * (y_b @ w_proj_b[l])
        h = h + merged @ w_out[l]
        h = h + memory_cross_attention(rmsnorm(h, ln_cross[l]), rmsnorm(mem, ln_mem[l]),
                                       w_cq[l], w_ckv[l], w_co[l])
        h = h + sqrelu_mlp(rmsnorm(h, ln_mlp[l]), w_up[l], w_down[l])
    return rmsnorm(h, ln_final)
```

```python
import functools

import jax
import jax.numpy as jnp
from jax import lax
from jax.experimental import pallas as pl
from jax.experimental.pallas import tpu as pltpu

F32 = jnp.float32
BF16 = jnp.bfloat16
HI = lax.Precision.HIGHEST

LANES = 128
HEAD_DIM = 64
N_HEADS = 16
WIDTH = N_HEADS * HEAD_DIM
N_PAIRS = WIDTH // LANES
DECAY_LORA, ICLR_LORA, GATE_LORA = 64, 64, 160
DECAY_PAD, ICLR_PAD, GATE_PAD = 128, 128, 256
RW_PAD = 3 * WIDTH + DECAY_PAD + ICLR_PAD + GATE_PAD
FZ_PAD = 256
FOX_PAD = 3 * WIDTH + FZ_PAD
X_HEADS, X_HEAD_DIM = 4, 128
NORM_EPS = 1e-5
GN_EPS = 64e-5
CHUNK = 64
SUB = 16
VMEM_LIMIT = 56 * 1024 * 1024


def _params(sem, vmem=VMEM_LIMIT):
    return pltpu.CompilerParams(dimension_semantics=sem, vmem_limit_bytes=vmem)


def _dot(a, b, precision=None):
    return jnp.dot(a, b, preferred_element_type=F32, precision=precision)


def _dot_t(a, b, precision=None):
    return lax.dot_general(a, b, (((1,), (1,)), ((), ())), preferred_element_type=F32, precision=precision)


def _dot_tt(a, b, precision=None):
    return lax.dot_general(a, b, (((0,), (0,)), ((), ())), preferred_element_type=F32, precision=precision)


def _softplus(x):
    return jnp.maximum(x, 0.0) + jnp.log(1.0 + jnp.exp(-jnp.abs(x)))


def _sigmoid(x):
    return 1.0 / (1.0 + jnp.exp(-x))


def _head_ones():
    r = lax.broadcasted_iota(jnp.int32, (LANES, LANES), 0) // HEAD_DIM
    c = lax.broadcasted_iota(jnp.int32, (LANES, LANES), 1) // HEAD_DIM
    return (r == c).astype(F32)


def _headsum(x, ones):
    parts = [_dot(x[:, c * LANES:(c + 1) * LANES], ones, HI) for c in range(x.shape[1] // LANES)]
    return parts[0] if len(parts) == 1 else jnp.concatenate(parts, axis=1)


def _rmsnorm_kernel(x_ref, g_ref, o_ref):
    x = x_ref[...]
    ms = jnp.mean(x * x, axis=-1, keepdims=True)
    o_ref[...] = (x * lax.rsqrt(ms + NORM_EPS) * g_ref[...]).astype(o_ref.dtype)


def _rmsnorm_bf16(x, g, tm):
    m, d = x.shape
    tm = min(tm, m)
    return pl.pallas_call(
        _rmsnorm_kernel,
        grid=(m // tm,),
        in_specs=[pl.BlockSpec((tm, d), lambda i: (i, 0)), pl.BlockSpec((1, d), lambda i: (0, 0))],
        out_specs=pl.BlockSpec((tm, d), lambda i: (i, 0)),
        out_shape=jax.ShapeDtypeStruct((m, d), BF16),
        compiler_params=_params(("parallel",)),
    )(x, g.reshape(1, d))


def _mm_kernel(a_ref, b_ref, o_ref):
    o_ref[...] = _dot(a_ref[...], b_ref[...]).astype(o_ref.dtype)


def _matmul(a, b, bm, bn, out_dtype=F32):
    m, k = a.shape
    n = b.shape[1]
    bm, bn = min(bm, m), min(bn, n)
    return pl.pallas_call(
        _mm_kernel,
        grid=(n // bn, m // bm),
        in_specs=[pl.BlockSpec((bm, k), lambda j, i: (i, 0)), pl.BlockSpec((k, bn), lambda j, i: (0, j))],
        out_specs=pl.BlockSpec((bm, bn), lambda j, i: (i, j)),
        out_shape=jax.ShapeDtypeStruct((m, n), out_dtype),
        compiler_params=_params(("parallel", "parallel")),
    )(a, b)


def _rw_prep_kernel(p_ref, mu_ref, w0_ref, wup_ref, a0_ref, aup_ref, gup_ref, kk_w_ref, ka_ref,
                    r_out, k_out, v_out, lw_out, kk_out, ic_out, g_out, carry_ref):
    tm = p_ref.shape[0]

    @pl.when(pl.program_id(0) == 0)
    def _():
        carry_ref[...] = jnp.zeros_like(carry_ref)

    p = p_ref[...]
    rows = lax.broadcasted_iota(jnp.int32, p.shape, 0)
    prev = jnp.where(rows == 0, carry_ref[...], pltpu.roll(p, 1, 0))
    carry_ref[...] = p[tm - 1:tm, :]
    ps = p + (prev - p) * mu_ref[...]

    w = WIDTH
    r, k, v = ps[:, 0:w], ps[:, w:2 * w], ps[:, 2 * w:3 * w]
    o = 3 * w
    wd = ps[:, o:o + DECAY_PAD]
    ad = ps[:, o + DECAY_PAD:o + DECAY_PAD + ICLR_PAD]
    gd = ps[:, o + DECAY_PAD + ICLR_PAD:o + DECAY_PAD + ICLR_PAD + GATE_PAD]

    z = w0_ref[...] + _dot(jnp.tanh(wd), wup_ref[...], HI)
    logw = -_softplus(-z) - 0.5
    lw_out[...] = -jnp.exp(logw)
    iclr = _sigmoid(a0_ref[...] + _dot(ad, aup_ref[...], HI))
    g_out[...] = _dot(_sigmoid(gd), gup_ref[...], HI)
    kk = k * kk_w_ref[...]
    ss = _headsum(kk * kk, _head_ones())
    kk_out[...] = kk * lax.rsqrt(jnp.maximum(ss, 1e-24))
    r_out[...] = r
    k_out[...] = k * (1.0 + (iclr - 1.0) * ka_ref[...])
    v_out[...] = v
    ic_out[...] = iclr


def _rw_prep(p_rw, mu_p, w0, wup_p, a0, aup_p, gup_p, k_k, k_a, tm):
    m = p_rw.shape[0]
    tm = min(tm, m)
    row = lambda a: a.reshape(1, -1)
    full = lambda a: pl.BlockSpec(a.shape, lambda i: (0, 0))
    ins = [p_rw, row(mu_p), row(w0), wup_p, row(a0), aup_p, gup_p, row(k_k), row(k_a)]
    out = jax.ShapeDtypeStruct((m, WIDTH), F32)
    return pl.pallas_call(
        _rw_prep_kernel,
        grid=(m // tm,),
        in_specs=[pl.BlockSpec((tm, RW_PAD), lambda i: (i, 0))] + [full(a) for a in ins[1:]],
        out_specs=[pl.BlockSpec((tm, WIDTH), lambda i: (i, 0))] * 7,
        out_shape=[out] * 7,
        scratch_shapes=[pltpu.VMEM((1, RW_PAD), F32)],
        compiler_params=_params(("arbitrary",)),
    )(*ins)


def _unit_lower_inverse(a, sub_mask, eye):
    d = jnp.where(sub_mask, a, 0.0)
    e = a - d
    p = eye + d
    dp = d
    for _ in range(SUB.bit_length() - 2):
        dp = _dot(dp, dp, HI)
        p = p + _dot(p, dp, HI)
    f = _dot(p, e, HI)
    g = eye + f
    fp = f
    for _ in range((CHUNK // SUB).bit_length() - 2):
        fp = _dot(fp, fp, HI)
        g = g + _dot(g, fp, HI)
    return _dot(g, p, HI)


def _rw_scan_kernel(r_ref, lw_ref, k_ref, v_ref, kk_ref, ic_ref, g_ref, rk_ref, lnw_ref, lnb_ref,
                    o_ref, h_ref):
    t_tile = r_ref.shape[0]
    c = CHUNK

    @pl.when(pl.program_id(1) == 0)
    def _():
        h_ref[...] = jnp.zeros_like(h_ref)

    lane = lax.broadcasted_iota(jnp.int32, (c, LANES), 1)
    head0 = lane < HEAD_DIM
    ri = lax.broadcasted_iota(jnp.int32, (c, c), 0)
    ci = lax.broadcasted_iota(jnp.int32, (c, c), 1)
    strict = ri > ci
    incl = ri >= ci
    tril = incl.astype(F32)
    eye = (ri == ci).astype(F32)
    sub_mask = (ri // SUB) == (ci // SUB)
    ones = _head_ones()
    pair_mask = ones > 0.5
    rk, lnw, lnb = rk_ref[...], lnw_ref[...], lnb_ref[...]

    def chunk_step(s, carry):
        sl = pl.ds(pl.multiple_of(s * c, c), c)
        r, lw, k, v = r_ref[sl, :], lw_ref[sl, :], k_ref[sl, :], v_ref[sl, :]
        kk, ic = kk_ref[sl, :], ic_ref[sl, :]
        cl = _dot(tril, lw, HI)
        cl_end = cl[c - 1:c, :]
        e_pos, e_neg = jnp.exp(cl), jnp.exp(-cl)
        b = kk * ic
        rt = r * e_pos
        at = -kk * jnp.exp(cl - lw)
        bt, kt = b * e_neg, k * e_neg
        e_end = jnp.exp(cl_end - cl)
        b_end, k_end = b * e_end, k * e_end

        ht = h_ref[...]
        r_s0 = _dot_t(rt, ht, HI)
        a_s0 = _dot_t(at, ht, HI)

        per_head = []
        for hmask in (head0, jnp.logical_not(head0)):
            lhs = jnp.concatenate([jnp.where(hmask, at, 0.0), jnp.where(hmask, rt, 0.0)], axis=0)
            sb = _dot_t(lhs, bt, HI)
            sk = _dot_t(lhs, kt, HI)
            a_ab = jnp.where(strict, sb[:c], 0.0)
            a_rb = jnp.where(incl, sb[c:], 0.0)
            a_ak = jnp.where(strict, sk[:c], 0.0)
            a_rk = jnp.where(incl, sk[c:], 0.0)
            per_head.append((_unit_lower_inverse(a_ab, sub_mask, eye), a_ak, a_rb, a_rk))

        (t0, ak0, rb0, rk0), (t1, ak1, rb1, rk1) = per_head
        rhs = a_s0 + jnp.where(head0, _dot(ak0, v, HI), _dot(ak1, v, HI))
        u = jnp.where(head0, _dot(t0, rhs, HI), _dot(t1, rhs, HI))
        y = r_s0 + jnp.where(head0, _dot(rb0, u, HI) + _dot(rk0, v, HI), _dot(rb1, u, HI) + _dot(rk1, v, HI))

        upd = _dot_tt(jnp.concatenate([u, v], axis=0), jnp.concatenate([b_end, k_end], axis=0), HI)
        h_ref[...] = ht * jnp.exp(cl_end) + jnp.where(pair_mask, upd, 0.0)

        mean = _dot(y, ones, HI) * (1.0 / HEAD_DIM)
        dlt = y - mean
        var = _dot(dlt * dlt, ones, HI) * (1.0 / HEAD_DIM)
        yn = dlt * lax.rsqrt(var + GN_EPS) * lnw + lnb
        bonus = _dot(r * k * rk, ones, HI) * v
        o_ref[sl, :] = ((yn + bonus) * g_ref[sl, :]).astype(o_ref.dtype)
        return carry

    lax.fori_loop(0, t_tile // c, chunk_step, 0)


def _rw_scan(r, lw, k, v, kk, ic, g, r_k, ln_w, ln_b, t_tile):
    m = r.shape[0]
    t_tile = min(t_tile, m)
    seq = pl.BlockSpec((t_tile, LANES), lambda p, t: (t, p))
    vec = pl.BlockSpec((1, LANES), lambda p, t: (0, p))
    row = lambda a: a.reshape(1, -1)
    return pl.pallas_call(
        _rw_scan_kernel,
        grid=(N_PAIRS, m // t_tile),
        in_specs=[seq] * 7 + [vec] * 3,
        out_specs=seq,
        out_shape=jax.ShapeDtypeStruct((m, WIDTH), BF16),
        scratch_shapes=[pltpu.VMEM((LANES, LANES), F32)],
        compiler_params=_params(("parallel", "arbitrary")),
    )(r, lw, k, v, kk, ic, g, row(r_k), row(ln_w), row(ln_b))


def _fox_prep_kernel(p_ref, qn_ref, kn_ref, bf_ref, q_out, k_out, v_out, c_out, carry_ref):
    tm = p_ref.shape[0]

    @pl.when(pl.program_id(0) == 0)
    def _():
        carry_ref[...] = jnp.zeros_like(carry_ref)

    w = WIDTH
    ones = _head_ones()
    q, k = p_ref[:, 0:w], p_ref[:, w:2 * w]
    q = q * lax.rsqrt(_headsum(q * q, ones) * (1.0 / HEAD_DIM) + NORM_EPS) * qn_ref[...]
    k = k * lax.rsqrt(_headsum(k * k, ones) * (1.0 / HEAD_DIM) + NORM_EPS) * kn_ref[...]
    q_out[...] = (q * (HEAD_DIM ** -0.5)).astype(q_out.dtype)
    k_out[...] = k.astype(k_out.dtype)
    v_out[...] = p_ref[:, 2 * w:3 * w].astype(v_out.dtype)

    logf = -_softplus(-(p_ref[:, 3 * w:3 * w + FZ_PAD] + bf_ref[...]))
    ri = lax.broadcasted_iota(jnp.int32, (tm, tm), 0)
    ci = lax.broadcasted_iota(jnp.int32, (tm, tm), 1)
    cum = _dot((ri >= ci).astype(F32), logf, HI) + carry_ref[...]
    c_out[...] = cum
    carry_ref[...] = cum[tm - 1:tm, :]


def _fox_prep(p_fox, q_norm, k_norm, b_f, tm):
    m = p_fox.shape[0]
    tm = min(tm, m)
    qn = jnp.tile(q_norm, N_HEADS).reshape(1, WIDTH)
    kn = jnp.tile(k_norm, N_HEADS).reshape(1, WIDTH)
    bf = jnp.pad(b_f, (0, FZ_PAD - N_HEADS)).reshape(1, FZ_PAD)
    full = lambda a: pl.BlockSpec(a.shape, lambda i: (0, 0))
    act = jax.ShapeDtypeStruct((m, WIDTH), BF16)
    return pl.pallas_call(
        _fox_prep_kernel,
        grid=(m // tm,),
        in_specs=[pl.BlockSpec((tm, FOX_PAD), lambda i: (i, 0)), full(qn), full(kn), full(bf)],
        out_specs=[pl.BlockSpec((tm, WIDTH), lambda i: (i, 0))] * 3 + [pl.BlockSpec((tm, FZ_PAD), lambda i: (i, 0))],
        out_shape=[act, act, act, jax.ShapeDtypeStruct((m, FZ_PAD), F32)],
        scratch_shapes=[pltpu.VMEM((1, FZ_PAD), F32)],
        compiler_params=_params(("arbitrary",)),
    )(p_fox, qn, kn, bf)


def _fox_attn_kernel(q_ref, k_ref, v_ref, c_ref, o_ref, m_ref, l_ref, acc_ref):
    tq, tk = q_ref.shape[0], k_ref.shape[0]
    qi, ki = pl.program_id(1), pl.program_id(2)

    @pl.when(ki == 0)
    def _():
        m_ref[...] = jnp.full_like(m_ref, -jnp.inf)
        l_ref[...] = jnp.zeros_like(l_ref)
        acc_ref[...] = jnp.zeros_like(acc_ref)

    @pl.when(ki <= qi)
    def _():
        q, k, v = q_ref[...], k_ref[...], v_ref[...]
        lane = lax.broadcasted_iota(jnp.int32, q.shape, 1)
        row = lax.broadcasted_iota(jnp.int32, (tq, tk), 0) + qi * tq
        col = lax.broadcasted_iota(jnp.int32, (tq, tk), 1) + ki * tk
        causal = col <= row
        for h in range(2):
            hmask = (lane < HEAD_DIM) if h == 0 else (lane >= HEAD_DIM)
            s = _dot_t(jnp.where(hmask, q, jnp.zeros_like(q)), k) - c_ref[h:h + 1, :]
            s = jnp.where(causal, s, -jnp.inf)
            m_old = m_ref[h]
            m_new = jnp.maximum(m_old, jnp.max(s, axis=1, keepdims=True))
            alpha = jnp.exp(m_old - m_new)
            p = jnp.exp(s - m_new)
            l_ref[h] = alpha * l_ref[h] + jnp.sum(p, axis=1, keepdims=True)
            acc_ref[h] = alpha * acc_ref[h] + _dot(p.astype(v.dtype), v)
            m_ref[h] = m_new

    @pl.when(ki == qi)
    def _():
        lane = lax.broadcasted_iota(jnp.int32, (tq, LANES), 1)
        o = jnp.where(lane < HEAD_DIM, acc_ref[0] / l_ref[0], acc_ref[1] / l_ref[1])
        o_ref[...] = o.astype(o_ref.dtype)


def _fox_attn(q, k, v, c_t, tq):
    m = q.shape[0]
    tq = min(tq, m)
    nq = m // tq
    qspec = pl.BlockSpec((tq, LANES), lambda p, i, j: (i, p))
    kspec = pl.BlockSpec((tq, LANES), lambda p, i, j: (jnp.minimum(i, j), p))
    cspec = pl.BlockSpec((None, 2, tq), lambda p, i, j: (p, 0, jnp.minimum(i, j)))
    return pl.pallas_call(
        _fox_attn_kernel,
        grid=(N_PAIRS, nq, nq),
        in_specs=[qspec, kspec, kspec, cspec],
        out_specs=qspec,
        out_shape=jax.ShapeDtypeStruct((m, WIDTH), BF16),
        scratch_shapes=[pltpu.VMEM((2, tq, 1), F32), pltpu.VMEM((2, tq, 1), F32), pltpu.VMEM((2, tq, LANES), F32)],
        compiler_params=_params(("parallel", "parallel", "arbitrary")),
    )(q, k, v, c_t)


def _mem_kv_kernel(mem_ref, g_ref, w_ref, o_ref):
    x = mem_ref[...]
    ms = jnp.mean(x * x, axis=-1, keepdims=True)
    xn = (x * lax.rsqrt(ms + NORM_EPS) * g_ref[...]).astype(BF16)
    o_ref[...] = _dot(xn, w_ref[...]).astype(o_ref.dtype)


def _mem_kv(mem, ln_mem, w_ckv):
    n, d = mem.shape
    ins = [mem, ln_mem.reshape(1, d), w_ckv]
    return pl.pallas_call(
        _mem_kv_kernel,
        grid=(1,),
        in_specs=[pl.BlockSpec(a.shape, lambda i: (0, 0)) for a in ins],
        out_specs=pl.BlockSpec((n, w_ckv.shape[1]), lambda i: (0, 0)),
        out_shape=jax.ShapeDtypeStruct((n, w_ckv.shape[1]), BF16),
        compiler_params=_params(("arbitrary",)),
    )(*ins)


def _post_kernel(ya_ref, yb_ref, pg_ref, x_ref, wa_ref, wb_ref, wo_ref, lnc_ref, wcq_ref, kv_ref, wco_ref, o_ref):
    d = x_ref.shape[1]
    xw = X_HEADS * X_HEAD_DIM
    ga = _sigmoid(pg_ref[:, 0:d])
    gb = _sigmoid(pg_ref[:, d:2 * d])
    merged = ga * _dot(ya_ref[...], wa_ref[...]) + gb * _dot(yb_ref[...], wb_ref[...])
    h = x_ref[...] + _dot(merged.astype(BF16), wo_ref[...])

    ms = jnp.mean(h * h, axis=-1, keepdims=True)
    hn = (h * lax.rsqrt(ms + NORM_EPS) * lnc_ref[...]).astype(BF16)
    q = _dot(hn, wcq_ref[...])
    outs = []
    for hd in range(X_HEADS):
        cols = slice(hd * X_HEAD_DIM, (hd + 1) * X_HEAD_DIM)
        vcols = slice(xw + hd * X_HEAD_DIM, xw + (hd + 1) * X_HEAD_DIM)
        s = _dot_t(q[:, cols].astype(BF16), kv_ref[:, cols]) * (X_HEAD_DIM ** -0.5)
        s = s - jnp.max(s, axis=1, keepdims=True)
        e = jnp.exp(s)
        pr = e / jnp.sum(e, axis=1, keepdims=True)
        outs.append(_dot(pr.astype(BF16), kv_ref[:, vcols]))
    o = jnp.concatenate(outs, axis=1).astype(BF16)
    o_ref[...] = h + _dot(o, wco_ref[...])


def _post(ya, yb, pg, x, wa, wb, wo, ln_cross, wcq, kv, wco, tm):
    m, d = x.shape
    tm = min(tm, m)
    lnc = ln_cross.reshape(1, d)
    rows = lambda a: pl.BlockSpec((tm, a.shape[1]), lambda i: (i, 0))
    const = lambda a: pl.BlockSpec(a.shape, lambda i: (0, 0), pipeline_mode=pl.Buffered(1))
    return pl.pallas_call(
        _post_kernel,
        grid=(m // tm,),
        in_specs=[rows(ya), rows(yb), rows(pg), rows(x)] + [const(a) for a in (wa, wb, wo, lnc, wcq, kv, wco)],
        out_specs=pl.BlockSpec((tm, d), lambda i: (i, 0)),
        out_shape=jax.ShapeDtypeStruct((m, d), F32),
        compiler_params=_params(("parallel",)),
    )(ya, yb, pg, x, wa, wb, wo, lnc, wcq, kv, wco)


def _mlp_kernel(h_ref, g_ref, wu_ref, wd_ref, gf_ref, o_ref, xn_ref, acc_ref):
    f = pl.program_id(1)

    @pl.when(f == 0)
    def _():
        h = h_ref[...]
        ms = jnp.mean(h * h, axis=-1, keepdims=True)
        xn_ref[...] = (h * lax.rsqrt(ms + NORM_EPS) * g_ref[...]).astype(xn_ref.dtype)
        acc_ref[...] = jnp.zeros_like(acc_ref)

    u = jnp.maximum(_dot(xn_ref[...], wu_ref[...]), 0.0)
    acc_ref[...] += _dot((u * u).astype(BF16), wd_ref[...])

    @pl.when(f == pl.num_programs(1) - 1)
    def _():
        h = h_ref[...] + acc_ref[...]
        ms = jnp.mean(h * h, axis=-1, keepdims=True)
        o_ref[...] = h * lax.rsqrt(ms + NORM_EPS) * gf_ref[...]


def _mlp(h, ln_mlp, w_up, w_down, ln_final, tm, tf):
    m, d = h.shape
    dff = w_up.shape[1]
    tm, tf = min(tm, m), min(tf, dff)
    return pl.pallas_call(
        _mlp_kernel,
        grid=(m // tm, dff // tf),
        in_specs=[pl.BlockSpec((tm, d), lambda i, f: (i, 0)),
                  pl.BlockSpec((1, d), lambda i, f: (0, 0)),
                  pl.BlockSpec((d, tf), lambda i, f: (0, f)),
                  pl.BlockSpec((tf, d), lambda i, f: (f, 0)),
                  pl.BlockSpec((1, d), lambda i, f: (0, 0))],
        out_specs=pl.BlockSpec((tm, d), lambda i, f: (i, 0)),
        out_shape=jax.ShapeDtypeStruct((m, d), F32),
        scratch_shapes=[pltpu.VMEM((tm, d), BF16), pltpu.VMEM((tm, d), F32)],
        compiler_params=_params(("parallel", "arbitrary")),
    )(h, ln_mlp.reshape(1, d), w_up, w_down, ln_final.reshape(1, d))


def _pad_cols(a, width):
    return jnp.pad(a, ((0, 0), (0, width - a.shape[1])))


def _pad_rows(a, height):
    return jnp.pad(a, ((0, height - a.shape[0]), (0, 0)))


def _layer(h, mem, ln_mix, w_in, rw_mu, rw_w0, rw_w_up, rw_a0, rw_a_up, rw_g_up, rw_k_k, rw_k_a, rw_r_k,
           rw_ln_w, rw_ln_b, fox_b_f, fox_q_norm, fox_k_norm, w_proj_a, w_proj_b, w_out, ln_cross, ln_mem,
           w_cq, w_ckv, w_co, ln_mlp, w_up, w_down, ln_final):
    w3 = 3 * WIDTH
    o_ad = w3 + DECAY_LORA
    o_gd = o_ad + ICLR_LORA
    rw_cols = o_gd + GATE_LORA
    fox_cols = w3 + N_HEADS

    def rw_layout(a):
        return jnp.concatenate([a[:, :w3], _pad_cols(a[:, w3:o_ad], DECAY_PAD), _pad_cols(a[:, o_ad:o_gd], ICLR_PAD),
                                _pad_cols(a[:, o_gd:rw_cols], GATE_PAD)], axis=1)

    w_rw = rw_layout(w_in[:, :rw_cols]).astype(BF16)
    w_fox = _pad_cols(w_in[:, rw_cols:rw_cols + fox_cols], FOX_PAD).astype(BF16)
    w_gate = w_in[:, rw_cols + fox_cols:].astype(BF16)
    mu_p = rw_layout(rw_mu.reshape(1, -1)).reshape(-1)

    xn = _rmsnorm_bf16(h, ln_mix, 512)
    p_rw = _matmul(xn, w_rw, 1024, 1792)
    p_fox = _matmul(xn, w_fox, 1024, 1664)
    p_gate = _matmul(xn, w_gate, 1024, 2048)

    r, k, v, lw, kk, ic, g = _rw_prep(p_rw, mu_p, rw_w0, _pad_rows(rw_w_up, DECAY_PAD), rw_a0,
                                      _pad_rows(rw_a_up, ICLR_PAD), _pad_rows(rw_g_up, GATE_PAD), rw_k_k, rw_k_a, 256)
    y_a = _rw_scan(r, lw, k, v, kk, ic, g, rw_r_k.reshape(-1), rw_ln_w, rw_ln_b, 512)

    fq, fk, fv, cum = _fox_prep(p_fox, fox_q_norm, fox_k_norm, fox_b_f, 512)
    c_t = cum[:, :N_HEADS].T.reshape(N_PAIRS, 2, -1)
    y_b = _fox_attn(fq, fk, fv, c_t, 512)

    kv = _mem_kv(mem, ln_mem, w_ckv.astype(BF16))
    h = _post(y_a, y_b, p_gate, h, w_proj_a.astype(BF16), w_proj_b.astype(BF16), w_out.astype(BF16), ln_cross,
              w_cq.astype(BF16), kv, w_co.astype(BF16), 256)
    return _mlp(h, ln_mlp, w_up.astype(BF16), w_down.astype(BF16), ln_final, 512, 1024)


def kernel(x, mem, ln_mix, w_in, rw_mu, rw_w0, rw_w_up, rw_a0, rw_a_up, rw_g_up, rw_k_k, rw_k_a, rw_r_k, rw_ln_w, rw_ln_b, fox_b_f, fox_q_norm, fox_k_norm, w_proj_a, w_proj_b, w_out, ln_cross, ln_mem, w_cq, w_ckv, w_co, ln_mlp, w_up, w_down, ln_final):
    b, s, d = x.shape
    assert b == 1 and ln_mix.shape[0] == 1, "single sequence, single layer"
    out = _layer(x[0], mem[0], ln_mix[0], w_in[0], rw_mu[0], rw_w0[0], rw_w_up[0], rw_a0[0], rw_a_up[0], rw_g_up[0],
                 rw_k_k[0], rw_k_a[0], rw_r_k[0], rw_ln_w[0], rw_ln_b[0], fox_b_f[0], fox_q_norm[0], fox_k_norm[0],
                 w_proj_a[0], w_proj_b[0], w_out[0], ln_cross[0], ln_mem[0], w_cq[0], w_ckv[0], w_co[0],
                 ln_mlp[0], w_up[0], w_down[0], ln_final)
    return out.reshape(b, s, d)
```

```python
import functools

import jax
import jax.numpy as jnp
from jax import lax
from jax.experimental import pallas as pl
from jax.experimental.pallas import tpu as pltpu

F32 = jnp.float32
BF16 = jnp.bfloat16
HI = lax.Precision.HIGHEST

LANES = 128
HEAD_DIM = 64
N_HEADS = 16
WIDTH = N_HEADS * HEAD_DIM
N_PAIRS = WIDTH // LANES
DECAY_LORA, ICLR_LORA, GATE_LORA = 64, 64, 160
DECAY_PAD, ICLR_PAD, GATE_PAD = 128, 128, 256
RW_PAD = 3 * WIDTH + DECAY_PAD + ICLR_PAD + GATE_PAD
FZ_PAD = 256
BIAS_PARTS = 3
LOG2E = 1.4426950408889634
FOX_PAD = 3 * WIDTH + FZ_PAD
X_HEADS, X_HEAD_DIM = 4, 128
NORM_EPS = 1e-5
GN_EPS = 64e-5
CHUNK = 64
SUB = 16
VMEM_LIMIT = 56 * 1024 * 1024


def _params(sem, vmem=VMEM_LIMIT):
    return pltpu.CompilerParams(dimension_semantics=sem, vmem_limit_bytes=vmem)


def _dot(a, b, precision=None):
    return jnp.dot(a, b, preferred_element_type=F32, precision=precision)


def _dot_t(a, b, precision=None):
    return lax.dot_general(a, b, (((1,), (1,)), ((), ())), preferred_element_type=F32, precision=precision)


def _dot_tt(a, b, precision=None):
    return lax.dot_general(a, b, (((0,), (0,)), ((), ())), preferred_element_type=F32, precision=precision)


def _split(x, terms):
    parts, rem = [], x
    for i in range(terms):
        part = rem.astype(BF16)
        parts.append(part)
        if i + 1 < terms:
            rem = rem - part.astype(F32)
    return parts


_DIMS = {"nn": (((1,), (0,)), ((), ())), "nt": (((1,), (1,)), ((), ())), "tn": (((0,), (0,)), ((), ()))}


def _mm(a_parts, b_parts, kind="nn"):
    order = max(len(a_parts), len(b_parts))
    acc = None
    for i, a in enumerate(a_parts):
        for j, b in enumerate(b_parts):
            if i + j < order:
                term = lax.dot_general(a, b, _DIMS[kind], preferred_element_type=F32)
                acc = term if acc is None else acc + term
    return acc


def _softplus(x):
    return jnp.maximum(x, 0.0) + jnp.log(1.0 + jnp.exp(-jnp.abs(x)))


def _sigmoid(x):
    return 1.0 / (1.0 + jnp.exp(-x))


def _head_ones():
    r = lax.broadcasted_iota(jnp.int32, (LANES, LANES), 0) // HEAD_DIM
    c = lax.broadcasted_iota(jnp.int32, (LANES, LANES), 1) // HEAD_DIM
    return (r == c).astype(F32)


def _headsum(x, ones):
    parts = [_dot(x[:, c * LANES:(c + 1) * LANES], ones, HI) for c in range(x.shape[1] // LANES)]
    return parts[0] if len(parts) == 1 else jnp.concatenate(parts, axis=1)


def _rmsnorm_kernel(x_ref, g_ref, o_ref):
    x = x_ref[...]
    ms = jnp.mean(x * x, axis=-1, keepdims=True)
    o_ref[...] = (x * lax.rsqrt(ms + NORM_EPS) * g_ref[...]).astype(o_ref.dtype)


def _rmsnorm_bf16(x, g, tm):
    m, d = x.shape
    tm = min(tm, m)
    return pl.pallas_call(
        _rmsnorm_kernel,
        grid=(m // tm,),
        in_specs=[pl.BlockSpec((tm, d), lambda i: (i, 0)), pl.BlockSpec((1, d), lambda i: (0, 0))],
        out_specs=pl.BlockSpec((tm, d), lambda i: (i, 0)),
        out_shape=jax.ShapeDtypeStruct((m, d), BF16),
        compiler_params=_params(("parallel",)),
    )(x, g.reshape(1, d))


def _mm_kernel(a_ref, b_ref, o_ref):
    o_ref[...] = _dot(a_ref[...], b_ref[...]).astype(o_ref.dtype)


def _matmul(a, b, bm, bn, out_dtype=F32):
    m, k = a.shape
    n = b.shape[1]
    bm, bn = min(bm, m), min(bn, n)
    return pl.pallas_call(
        _mm_kernel,
        grid=(n // bn, m // bm),
        in_specs=[pl.BlockSpec((bm, k), lambda j, i: (i, 0)), pl.BlockSpec((k, bn), lambda j, i: (0, j))],
        out_specs=pl.BlockSpec((bm, bn), lambda j, i: (i, j)),
        out_shape=jax.ShapeDtypeStruct((m, n), out_dtype),
        compiler_params=_params(("parallel", "parallel")),
    )(a, b)


def _rw_prep_kernel(p_ref, mu_ref, w0_ref, wup_ref, a0_ref, aup_ref, gup_ref, kk_w_ref, ka_ref,
                    r_out, k_out, v_out, lw_out, kk_out, ic_out, g_out, carry_ref):
    tm = p_ref.shape[0]

    @pl.when(pl.program_id(0) == 0)
    def _():
        carry_ref[...] = jnp.zeros_like(carry_ref)

    p = p_ref[...]
    rows = lax.broadcasted_iota(jnp.int32, p.shape, 0)
    prev = jnp.where(rows == 0, carry_ref[...], pltpu.roll(p, 1, 0))
    carry_ref[...] = p[tm - 1:tm, :]
    ps = p + (prev - p) * mu_ref[...]

    w = WIDTH
    r, k, v = ps[:, 0:w], ps[:, w:2 * w], ps[:, 2 * w:3 * w]
    o = 3 * w
    wd = ps[:, o:o + DECAY_PAD]
    ad = ps[:, o + DECAY_PAD:o + DECAY_PAD + ICLR_PAD]
    gd = ps[:, o + DECAY_PAD + ICLR_PAD:o + DECAY_PAD + ICLR_PAD + GATE_PAD]

    z = w0_ref[...] + _dot(jnp.tanh(wd), wup_ref[...], HI)
    logw = -_softplus(-z) - 0.5
    lw_out[...] = -jnp.exp(logw)
    iclr = _sigmoid(a0_ref[...] + _dot(ad, aup_ref[...], HI))
    g_out[...] = _dot(_sigmoid(gd), gup_ref[...], HI)
    kk = k * kk_w_ref[...]
    ss = _headsum(kk * kk, _head_ones())
    kk_out[...] = kk * lax.rsqrt(jnp.maximum(ss, 1e-24))
    r_out[...] = r
    k_out[...] = k * (1.0 + (iclr - 1.0) * ka_ref[...])
    v_out[...] = v
    ic_out[...] = iclr


def _rw_prep(p_rw, mu_p, w0, wup_p, a0, aup_p, gup_p, k_k, k_a, tm):
    m = p_rw.shape[0]
    tm = min(tm, m)
    row = lambda a: a.reshape(1, -1)
    full = lambda a: pl.BlockSpec(a.shape, lambda i: (0, 0))
    ins = [p_rw, row(mu_p), row(w0), wup_p, row(a0), aup_p, gup_p, row(k_k), row(k_a)]
    out = jax.ShapeDtypeStruct((m, WIDTH), F32)
    return pl.pallas_call(
        _rw_prep_kernel,
        grid=(m // tm,),
        in_specs=[pl.BlockSpec((tm, RW_PAD), lambda i: (i, 0))] + [full(a) for a in ins[1:]],
        out_specs=[pl.BlockSpec((tm, WIDTH), lambda i: (i, 0))] * 7,
        out_shape=[out] * 7,
        scratch_shapes=[pltpu.VMEM((1, RW_PAD), F32)],
        compiler_params=_params(("arbitrary",)),
    )(*ins)


def _bf(x):
    return [x.astype(BF16)]


def _unit_lower_inverse(a_list, sub_mask, eye):
    d = [jnp.where(sub_mask, a, 0.0) for a in a_list]
    e = [a - x for a, x in zip(a_list, d)]
    p = [eye + x for x in d]
    dp = [_bf(x) for x in d]
    for _ in range(SUB.bit_length() - 2):
        dp = [_bf(_mm(x, x)) for x in dp]
        p = [x + _mm(_bf(x), y) for x, y in zip(p, dp)]
    f = [_mm(_bf(x), _bf(y)) for x, y in zip(p, e)]
    g = [eye + x for x in f]
    fp = [_bf(x) for x in f]
    for _ in range((CHUNK // SUB).bit_length() - 2):
        fp = [_bf(_mm(x, x)) for x in fp]
        g = [x + _mm(_bf(x), y) for x, y in zip(g, fp)]
    return [_mm(_bf(x), _bf(y)) for x, y in zip(g, p)]


def _rw_scan_kernel(r_ref, lw_ref, k_ref, v_ref, kk_ref, ic_ref, g_ref, rk_ref, lnw_ref, lnb_ref,
                    o_ref, h_ref):
    t_tile = r_ref.shape[0]
    c = CHUNK

    @pl.when(pl.program_id(1) == 0)
    def _():
        h_ref[...] = jnp.zeros_like(h_ref)

    lane = lax.broadcasted_iota(jnp.int32, (c, LANES), 1)
    head_masks = (lane < HEAD_DIM, lane >= HEAD_DIM)
    ri = lax.broadcasted_iota(jnp.int32, (c, c), 0)
    ci = lax.broadcasted_iota(jnp.int32, (c, c), 1)
    strict = ri > ci
    incl = ri >= ci
    tril = _bf(incl)
    eye = (ri == ci).astype(F32)
    sub_mask = (ri // SUB) == (ci // SUB)
    ones_f = _head_ones()
    ones = _bf(ones_f)
    pair_mask = ones_f > 0.5
    rk, lnw, lnb = rk_ref[...], lnw_ref[...], lnb_ref[...]
    chunks = range(t_tile // c)
    units = [(s, h) for s in chunks for h in range(2)]
    sel = lambda pair: jnp.where(head_masks[0], pair[0], pair[1])

    sls = [slice(s * c, (s + 1) * c) for s in chunks]
    r = [r_ref[sl, :] for sl in sls]
    lw = [lw_ref[sl, :] for sl in sls]
    k = [k_ref[sl, :] for sl in sls]
    v = [v_ref[sl, :] for sl in sls]
    kk = [kk_ref[sl, :] for sl in sls]
    ic = [ic_ref[sl, :] for sl in sls]
    cl = [_mm(tril, _split(x, 3)) for x in lw]
    cl_end = [x[c - 1:c, :] for x in cl]
    b = [kk[s] * ic[s] for s in chunks]
    rt = [r[s] * jnp.exp(cl[s]) for s in chunks]
    at = [-kk[s] * jnp.exp(cl[s] - lw[s]) for s in chunks]
    e_neg = [jnp.exp(-x) for x in cl]
    bt = [_bf(b[s] * e_neg[s]) for s in chunks]
    kt = [_bf(k[s] * e_neg[s]) for s in chunks]
    e_end = [jnp.exp(cl_end[s] - cl[s]) for s in chunks]
    b_end = [_bf(b[s] * e_end[s]) for s in chunks]
    k_end = [_bf(k[s] * e_end[s]) for s in chunks]
    vs = [_bf(x) for x in v]

    at_h = [jnp.where(head_masks[h], at[s], 0.0) for s, h in units]
    lhs = [_bf(jnp.concatenate([at_h[i], jnp.where(head_masks[h], rt[s], 0.0)], axis=0))
           for i, (s, h) in enumerate(units)]
    sb = [_mm(lhs[i], bt[s], "nt") for i, (s, h) in enumerate(units)]
    sk = [_mm(lhs[i], kt[s], "nt") for i, (s, h) in enumerate(units)]
    akv = [_mm(_bf(jnp.where(strict, sk[i][:c], 0.0)), vs[s]) for i, (s, h) in enumerate(units)]
    rkv = [_mm(_bf(jnp.where(incl, sk[i][c:], 0.0)), vs[s]) for i, (s, h) in enumerate(units)]
    rb = [_bf(jnp.where(incl, x[c:], 0.0)) for x in sb]
    vk = [_mm(vs[s], k_end[s], "tn") for s in chunks]
    t_inv = [_bf(x) for x in _unit_lower_inverse([jnp.where(strict, x[:c], 0.0) for x in sb], sub_mask, eye)]
    w_h = [_mm(t_inv[i], _bf(at_h[i])) for i in range(len(units))]
    tk_h = [_mm(t_inv[i], _bf(akv[i])) for i in range(len(units))]
    s0_lhs = [_bf(jnp.concatenate([w_h[2 * s] + w_h[2 * s + 1], rt[s]], axis=0)) for s in chunks]
    tk = [sel(tk_h[2 * s:2 * s + 2]) for s in chunks]
    decay_end = [jnp.exp(x) for x in cl_end]

    ht = h_ref[...]
    us, rs0 = [], []
    for s in chunks:
        s0 = _mm(s0_lhs[s], _bf(ht), "nt")
        us.append(_bf(s0[:c] + tk[s]))
        rs0.append(s0[c:])
        ht = ht * decay_end[s] + jnp.where(pair_mask, _mm(us[s], b_end[s], "tn") + vk[s], 0.0)
    h_ref[...] = ht

    rbu = [_mm(rb[i], us[s]) for i, (s, h) in enumerate(units)]
    y = [rs0[s] + sel([rbu[2 * s] + rkv[2 * s], rbu[2 * s + 1] + rkv[2 * s + 1]]) for s in chunks]
    mean = [_mm(_split(x, 2), ones) * (1.0 / HEAD_DIM) for x in y]
    dlt = [y[s] - mean[s] for s in chunks]
    var = [_mm(_split(x * x, 2), ones) * (1.0 / HEAD_DIM) for x in dlt]
    bonus = [_mm(_split(r[s] * k[s] * rk, 2), ones) * v[s] for s in chunks]
    for s in chunks:
        yn = dlt[s] * lax.rsqrt(var[s] + GN_EPS) * lnw + lnb
        o_ref[sls[s], :] = ((yn + bonus[s]) * g_ref[sls[s], :]).astype(o_ref.dtype)


def _rw_scan(r, lw, k, v, kk, ic, g, r_k, ln_w, ln_b, t_tile):
    m = r.shape[0]
    t_tile = min(t_tile, m)
    seq = pl.BlockSpec((t_tile, LANES), lambda p, t: (t, p))
    vec = pl.BlockSpec((1, LANES), lambda p, t: (0, p))
    row = lambda a: a.reshape(1, -1)
    return pl.pallas_call(
        _rw_scan_kernel,
        grid=(N_PAIRS, m // t_tile),
        in_specs=[seq] * 7 + [vec] * 3,
        out_specs=seq,
        out_shape=jax.ShapeDtypeStruct((m, WIDTH), BF16),
        scratch_shapes=[pltpu.VMEM((LANES, LANES), F32)],
        compiler_params=_params(("parallel", "arbitrary")),
    )(r, lw, k, v, kk, ic, g, row(r_k), row(ln_w), row(ln_b))


def _fox_prep_kernel(p_ref, qn_ref, kn_ref, bf_ref, part_ref, q_out, k_out, v_out, ka_out, carry_ref):
    tm = p_ref.shape[0]

    @pl.when(pl.program_id(0) == 0)
    def _():
        carry_ref[...] = jnp.zeros_like(carry_ref)

    w = WIDTH
    ones = _head_ones()
    q, k = p_ref[:, 0:w], p_ref[:, w:2 * w]
    q = q * lax.rsqrt(_headsum(q * q, ones) * (1.0 / HEAD_DIM) + NORM_EPS) * qn_ref[...]
    k = k * lax.rsqrt(_headsum(k * k, ones) * (1.0 / HEAD_DIM) + NORM_EPS) * kn_ref[...]
    q_out[...] = (q * (HEAD_DIM ** -0.5 * LOG2E)).astype(q_out.dtype)
    k_out[...] = k.astype(k_out.dtype)
    v_out[...] = p_ref[:, 2 * w:3 * w].astype(v_out.dtype)

    logf = -_softplus(-(p_ref[:, 3 * w:3 * w + FZ_PAD] + bf_ref[...]))
    ri = lax.broadcasted_iota(jnp.int32, (tm, tm), 0)
    ci = lax.broadcasted_iota(jnp.int32, (tm, tm), 1)
    cum = _mm(_bf(ri >= ci), _split(logf, 3)) + carry_ref[...]
    carry_ref[...] = cum[tm - 1:tm, :]
    hi, mid, lo = _split(cum[:, 0:LANES] * (-LOG2E), 3)
    part = part_ref[...]
    ka_out[...] = jnp.where(part == 0, hi, jnp.where(part == 1, mid, lo))


def _fox_prep(p_fox, q_norm, k_norm, bf_rep, tm):
    m = p_fox.shape[0]
    tm = min(tm, m)
    qn = jnp.tile(q_norm, N_HEADS).reshape(1, WIDTH)
    kn = jnp.tile(k_norm, N_HEADS).reshape(1, WIDTH)
    part = (jnp.arange(LANES, dtype=jnp.int32) % BIAS_PARTS).reshape(1, LANES)
    full = lambda a: pl.BlockSpec(a.shape, lambda i: (0, 0))
    act = jax.ShapeDtypeStruct((m, WIDTH), BF16)
    return pl.pallas_call(
        _fox_prep_kernel,
        grid=(m // tm,),
        in_specs=[pl.BlockSpec((tm, FOX_PAD), lambda i: (i, 0)), full(qn), full(kn), full(bf_rep), full(part)],
        out_specs=[pl.BlockSpec((tm, WIDTH), lambda i: (i, 0))] * 3 + [pl.BlockSpec((tm, LANES), lambda i: (i, 0))],
        out_shape=[act, act, act, jax.ShapeDtypeStruct((m, LANES), BF16)],
        scratch_shapes=[pltpu.VMEM((1, FZ_PAD), F32)],
        compiler_params=_params(("arbitrary",)),
    )(p_fox, qn, kn, bf_rep, part)


def _fox_attn_kernel(qi_ref, ki_ref, q_ref, k_ref, ka_ref, v_ref, o_ref, m_ref, acc_ref):
    tq, tk = q_ref.shape[0], k_ref.shape[0]
    pair, step_id = pl.program_id(0), pl.program_id(1)
    qi, ki = qi_ref[step_id], ki_ref[step_id]

    @pl.when(ki == 0)
    def _():
        m_ref[...] = jnp.full_like(m_ref, -jnp.inf)
        acc_ref[...] = jnp.zeros_like(acc_ref)

    def step(diagonal):
        q = q_ref[...]
        lane_q = lax.broadcasted_iota(jnp.int32, (tq, LANES), 1)
        lane_k = lax.broadcasted_iota(jnp.int32, (tk, LANES), 1)
        k_full = jnp.concatenate([k_ref[...], ka_ref[...]], axis=1)
        v_full = jnp.concatenate([v_ref[...], (lane_k == 0).astype(BF16)], axis=1)
        q_full = []
        for h in range(2):
            hmask = (lane_q < HEAD_DIM) if h == 0 else (lane_q >= HEAD_DIM)
            first = (2 * pair + h) * BIAS_PARTS
            aug = jnp.logical_and(lane_q >= first, lane_q < first + BIAS_PARTS).astype(BF16)
            q_full.append(jnp.concatenate([jnp.where(hmask, q, jnp.zeros_like(q)), aug], axis=1))
        s = [_mm([k_full], [q_full[h]], "nt") for h in range(2)]
        if diagonal:
            causal = (lax.broadcasted_iota(jnp.int32, (tk, tq), 0) <= lax.broadcasted_iota(jnp.int32, (tk, tq), 1))
            s = [jnp.where(causal, x, -jnp.inf) for x in s]
        m_old = [m_ref[h] for h in range(2)]
        m_new = [jnp.maximum(m_old[h], jnp.max(s[h], axis=0, keepdims=True)) for h in range(2)]
        p = [jnp.exp2(s[h] - m_new[h]).astype(BF16) for h in range(2)]
        pv = [_mm([v_full], [p[h]], "tn") for h in range(2)]
        for h in range(2):
            acc_ref[h] = jnp.exp2(m_old[h] - m_new[h]) * acc_ref[h] + pv[h]
            m_ref[h] = m_new[h]

    @pl.when(ki < qi)
    def _():
        step(False)

    @pl.when(ki == qi)
    def _():
        step(True)
        row = lax.broadcasted_iota(jnp.int32, (LANES, tq), 0)
        o = [acc_ref[h, 0:LANES, :] / acc_ref[h, LANES:LANES + 1, :] for h in range(2)]
        o_ref[...] = jnp.where(row < HEAD_DIM, o[0], o[1]).T.astype(o_ref.dtype)


def _fox_attn(q, k, ka, v, tq):
    m = q.shape[0]
    tq = min(tq, m)
    nq = m // tq
    pairs = [(i, j) for i in range(nq) for j in range(i + 1)]
    qi_tab = jnp.asarray([i for i, _ in pairs], jnp.int32)
    ki_tab = jnp.asarray([j for _, j in pairs], jnp.int32)
    qspec = pl.BlockSpec((tq, LANES), lambda p, t, qi, ki: (qi[t], p))
    kspec = pl.BlockSpec((tq, LANES), lambda p, t, qi, ki: (ki[t], p))
    kaspec = pl.BlockSpec((tq, LANES), lambda p, t, qi, ki: (ki[t], 0))
    return pl.pallas_call(
        _fox_attn_kernel,
        grid_spec=pltpu.PrefetchScalarGridSpec(
            num_scalar_prefetch=2,
            grid=(N_PAIRS, len(pairs)),
            in_specs=[qspec, kspec, kaspec, kspec],
            out_specs=qspec,
            scratch_shapes=[pltpu.VMEM((2, 1, tq), F32), pltpu.VMEM((2, 2 * LANES, tq), F32)],
        ),
        out_shape=jax.ShapeDtypeStruct((m, WIDTH), BF16),
        compiler_params=_params(("parallel", "arbitrary")),
    )(qi_tab, ki_tab, q, k, ka, v)


def _mem_kv_kernel(mem_ref, g_ref, w_ref, o_ref):
    x = mem_ref[...]
    ms = jnp.mean(x * x, axis=-1, keepdims=True)
    xn = (x * lax.rsqrt(ms + NORM_EPS) * g_ref[...]).astype(BF16)
    o_ref[...] = _dot(xn, w_ref[...]).astype(o_ref.dtype)


def _mem_kv(mem, ln_mem, w_ckv):
    n, d = mem.shape
    ins = [mem, ln_mem.reshape(1, d), w_ckv]
    return pl.pallas_call(
        _mem_kv_kernel,
        grid=(1,),
        in_specs=[pl.BlockSpec(a.shape, lambda i: (0, 0)) for a in ins],
        out_specs=pl.BlockSpec((n, w_ckv.shape[1]), lambda i: (0, 0)),
        out_shape=jax.ShapeDtypeStruct((n, w_ckv.shape[1]), BF16),
        compiler_params=_params(("arbitrary",)),
    )(*ins)


def _post_kernel(ya_ref, yb_ref, pg_ref, x_ref, wa_ref, wb_ref, wo_ref, lnc_ref, wcq_ref, kv_ref, wco_ref, o_ref):
    d = x_ref.shape[1]
    xw = X_HEADS * X_HEAD_DIM
    ga = _sigmoid(pg_ref[:, 0:d])
    gb = _sigmoid(pg_ref[:, d:2 * d])
    merged = ga * _dot(ya_ref[...], wa_ref[...]) + gb * _dot(yb_ref[...], wb_ref[...])
    h = x_ref[...] + _dot(merged.astype(BF16), wo_ref[...])

    ms = jnp.mean(h * h, axis=-1, keepdims=True)
    hn = (h * lax.rsqrt(ms + NORM_EPS) * lnc_ref[...]).astype(BF16)
    q = _dot(hn, wcq_ref[...])
    outs = []
    for hd in range(X_HEADS):
        cols = slice(hd * X_HEAD_DIM, (hd + 1) * X_HEAD_DIM)
        vcols = slice(xw + hd * X_HEAD_DIM, xw + (hd + 1) * X_HEAD_DIM)
        s = _dot_t(q[:, cols].astype(BF16), kv_ref[:, cols]) * (X_HEAD_DIM ** -0.5)
        s = s - jnp.max(s, axis=1, keepdims=True)
        e = jnp.exp(s)
        pr = e / jnp.sum(e, axis=1, keepdims=True)
        outs.append(_dot(pr.astype(BF16), kv_ref[:, vcols]))
    o = jnp.concatenate(outs, axis=1).astype(BF16)
    o_ref[...] = h + _dot(o, wco_ref[...])


def _post(ya, yb, pg, x, wa, wb, wo, ln_cross, wcq, kv, wco, tm):
    m, d = x.shape
    tm = min(tm, m)
    lnc = ln_cross.reshape(1, d)
    rows = lambda a: pl.BlockSpec((tm, a.shape[1]), lambda i: (i, 0))
    const = lambda a: pl.BlockSpec(a.shape, lambda i: (0, 0), pipeline_mode=pl.Buffered(1))
    return pl.pallas_call(
        _post_kernel,
        grid=(m // tm,),
        in_specs=[rows(ya), rows(yb), rows(pg), rows(x)] + [const(a) for a in (wa, wb, wo, lnc, wcq, kv, wco)],
        out_specs=pl.BlockSpec((tm, d), lambda i: (i, 0)),
        out_shape=jax.ShapeDtypeStruct((m, d), F32),
        compiler_params=_params(("parallel",)),
    )(ya, yb, pg, x, wa, wb, wo, lnc, wcq, kv, wco)


def _mlp_kernel(h_ref, g_ref, wu_ref, wd_ref, gf_ref, o_ref, xn_ref, acc_ref):
    f = pl.program_id(1)

    @pl.when(f == 0)
    def _():
        h = h_ref[...]
        ms = jnp.mean(h * h, axis=-1, keepdims=True)
        xn_ref[...] = (h * lax.rsqrt(ms + NORM_EPS) * g_ref[...]).astype(xn_ref.dtype)
        acc_ref[...] = jnp.zeros_like(acc_ref)

    u = jnp.maximum(_dot(xn_ref[...], wu_ref[...]), 0.0)
    acc_ref[...] += _dot((u * u).astype(BF16), wd_ref[...])

    @pl.when(f == pl.num_programs(1) - 1)
    def _():
        h = h_ref[...] + acc_ref[...]
        ms = jnp.mean(h * h, axis=-1, keepdims=True)
        o_ref[...] = h * lax.rsqrt(ms + NORM_EPS) * gf_ref[...]


def _mlp(h, ln_mlp, w_up, w_down, ln_final, tm, tf):
    m, d = h.shape
    dff = w_up.shape[1]
    tm, tf = min(tm, m), min(tf, dff)
    return pl.pallas_call(
        _mlp_kernel,
        grid=(m // tm, dff // tf),
        in_specs=[pl.BlockSpec((tm, d), lambda i, f: (i, 0)),
                  pl.BlockSpec((1, d), lambda i, f: (0, 0)),
                  pl.BlockSpec((d, tf), lambda i, f: (0, f)),
                  pl.BlockSpec((tf, d), lambda i, f: (f, 0)),
                  pl.BlockSpec((1, d), lambda i, f: (0, 0))],
        out_specs=pl.BlockSpec((tm, d), lambda i, f: (i, 0)),
        out_shape=jax.ShapeDtypeStruct((m, d), F32),
        scratch_shapes=[pltpu.VMEM((tm, d), BF16), pltpu.VMEM((tm, d), F32)],
        compiler_params=_params(("parallel", "arbitrary")),
    )(h, ln_mlp.reshape(1, d), w_up, w_down, ln_final.reshape(1, d))


def _pad_cols(a, width):
    return jnp.pad(a, ((0, 0), (0, width - a.shape[1])))


def _pad_rows(a, height):
    return jnp.pad(a, ((0, height - a.shape[0]), (0, 0)))


def _layer(h, mem, ln_mix, w_in, rw_mu, rw_w0, rw_w_up, rw_a0, rw_a_up, rw_g_up, rw_k_k, rw_k_a, rw_r_k,
           rw_ln_w, rw_ln_b, fox_b_f, fox_q_norm, fox_k_norm, w_proj_a, w_proj_b, w_out, ln_cross, ln_mem,
           w_cq, w_ckv, w_co, ln_mlp, w_up, w_down, ln_final):
    w3 = 3 * WIDTH
    o_ad = w3 + DECAY_LORA
    o_gd = o_ad + ICLR_LORA
    rw_cols = o_gd + GATE_LORA
    fox_cols = w3 + N_HEADS

    def rw_layout(a):
        return jnp.concatenate([a[:, :w3], _pad_cols(a[:, w3:o_ad], DECAY_PAD), _pad_cols(a[:, o_ad:o_gd], ICLR_PAD),
                                _pad_cols(a[:, o_gd:rw_cols], GATE_PAD)], axis=1)

    w_rw = rw_layout(w_in[:, :rw_cols]).astype(BF16)
    rep = jnp.repeat(jnp.arange(N_HEADS), BIAS_PARTS)
    w_fz = w_in[:, rw_cols + w3:rw_cols + fox_cols][:, rep]
    w_fox = _pad_cols(jnp.concatenate([w_in[:, rw_cols:rw_cols + w3], w_fz], axis=1), FOX_PAD).astype(BF16)
    bf_rep = jnp.pad(fox_b_f[rep], (0, FZ_PAD - N_HEADS * BIAS_PARTS)).reshape(1, FZ_PAD)
    w_gate = w_in[:, rw_cols + fox_cols:].astype(BF16)
    mu_p = rw_layout(rw_mu.reshape(1, -1)).reshape(-1)

    xn = _rmsnorm_bf16(h, ln_mix, 512)
    p_rw = _matmul(xn, w_rw, 1024, 1792)
    p_fox = _matmul(xn, w_fox, 1024, 1664)
    p_gate = _matmul(xn, w_gate, 1024, 2048)

    r, k, v, lw, kk, ic, g = _rw_prep(p_rw, mu_p, rw_w0, _pad_rows(rw_w_up, DECAY_PAD), rw_a0,
                                      _pad_rows(rw_a_up, ICLR_PAD), _pad_rows(rw_g_up, GATE_PAD), rw_k_k, rw_k_a, 256)
    y_a = _rw_scan(r, lw, k, v, kk, ic, g, rw_r_k.reshape(-1), rw_ln_w, rw_ln_b, 512)

    fq, fk, fv, fka = _fox_prep(p_fox, fox_q_norm, fox_k_norm, bf_rep, 512)
    y_b = _fox_attn(fq, fk, fka, fv, 512)

    kv = _mem_kv(mem, ln_mem, w_ckv.astype(BF16))
    h = _post(y_a, y_b, p_gate, h, w_proj_a.astype(BF16), w_proj_b.astype(BF16), w_out.astype(BF16), ln_cross,
              w_cq.astype(BF16), kv, w_co.astype(BF16), 256)
    return _mlp(h, ln_mlp, w_up.astype(BF16), w_down.astype(BF16), ln_final, 512, 1024)


def kernel(x, mem, ln_mix, w_in, rw_mu, rw_w0, rw_w_up, rw_a0, rw_a_up, rw_g_up, rw_k_k, rw_k_a, rw_r_k, rw_ln_w, rw_ln_b, fox_b_f, fox_q_norm, fox_k_norm, w_proj_a, w_proj_b, w_out, ln_cross, ln_mem, w_cq, w_ckv, w_co, ln_mlp, w_up, w_down, ln_final):
    b, s, d = x.shape
    assert b == 1 and ln_mix.shape[0] == 1, "single sequence, single layer"
    out = _layer(x[0], mem[0], ln_mix[0], w_in[0], rw_mu[0], rw_w0[0], rw_w_up[0], rw_a0[0], rw_a_up[0], rw_g_up[0],
                 rw_k_k[0], rw_k_a[0], rw_r_k[0], rw_ln_w[0], rw_ln_b[0], fox_b_f[0], fox_q_norm[0], fox_k_norm[0],
                 w_proj_a[0], w_proj_b[0], w_out[0], ln_cross[0], ln_mem[0], w_cq[0], w_ckv[0], w_co[0],
                 ln_mlp[0], w_up[0], w_down[0], ln_final)
    return out.reshape(b, s, d)
```

```python
import functools

import jax
import jax.numpy as jnp
from jax import lax
from jax.experimental import pallas as pl
from jax.experimental.pallas import tpu as pltpu

F32 = jnp.float32
BF16 = jnp.bfloat16
HI = lax.Precision.HIGHEST

LANES = 128
HEAD_DIM = 64
N_HEADS = 16
WIDTH = N_HEADS * HEAD_DIM
N_PAIRS = WIDTH // LANES
DECAY_LORA, ICLR_LORA, GATE_LORA = 64, 64, 160
DECAY_PAD, ICLR_PAD, GATE_PAD = 128, 128, 256
RW_PAD = 3 * WIDTH + DECAY_PAD + ICLR_PAD + GATE_PAD
FZ_PAD = 256
BIAS_PARTS = 3
LOG2E = 1.4426950408889634
ONES_ROWS = 16
FOX_PAD = 3 * WIDTH + FZ_PAD
X_HEADS, X_HEAD_DIM = 4, 128
NORM_EPS = 1e-5
GN_EPS = 64e-5
CHUNK = 64
SUB = 16
VMEM_LIMIT = 56 * 1024 * 1024


def _params(sem, vmem=VMEM_LIMIT):
    return pltpu.CompilerParams(dimension_semantics=sem, vmem_limit_bytes=vmem)


def _dot(a, b, precision=None):
    return jnp.dot(a, b, preferred_element_type=F32, precision=precision)


def _dot_t(a, b, precision=None):
    return lax.dot_general(a, b, (((1,), (1,)), ((), ())), preferred_element_type=F32, precision=precision)


def _dot_tt(a, b, precision=None):
    return lax.dot_general(a, b, (((0,), (0,)), ((), ())), preferred_element_type=F32, precision=precision)


def _split(x, terms):
    parts, rem = [], x
    for i in range(terms):
        part = rem.astype(BF16)
        parts.append(part)
        if i + 1 < terms:
            rem = rem - part.astype(F32)
    return parts


_DIMS = {"nn": (((1,), (0,)), ((), ())), "nt": (((1,), (1,)), ((), ())), "tn": (((0,), (0,)), ((), ()))}


def _mm(a_parts, b_parts, kind="nn"):
    order = max(len(a_parts), len(b_parts))
    acc = None
    for i, a in enumerate(a_parts):
        for j, b in enumerate(b_parts):
            if i + j < order:
                term = lax.dot_general(a, b, _DIMS[kind], preferred_element_type=F32)
                acc = term if acc is None else acc + term
    return acc


def _softplus(x):
    return jnp.maximum(x, 0.0) + jnp.log(1.0 + jnp.exp(-jnp.abs(x)))


def _sigmoid(x):
    return 1.0 / (1.0 + jnp.exp(-x))


def _head_ones():
    r = lax.broadcasted_iota(jnp.int32, (LANES, LANES), 0) // HEAD_DIM
    c = lax.broadcasted_iota(jnp.int32, (LANES, LANES), 1) // HEAD_DIM
    return (r == c).astype(F32)


def _headsum(x, ones):
    parts = [_dot(x[:, c * LANES:(c + 1) * LANES], ones, HI) for c in range(x.shape[1] // LANES)]
    return parts[0] if len(parts) == 1 else jnp.concatenate(parts, axis=1)


def _rmsnorm_kernel(x_ref, g_ref, o_ref):
    x = x_ref[...]
    ms = jnp.mean(x * x, axis=-1, keepdims=True)
    o_ref[...] = (x * lax.rsqrt(ms + NORM_EPS) * g_ref[...]).astype(o_ref.dtype)


def _rmsnorm_bf16(x, g, tm):
    m, d = x.shape
    tm = min(tm, m)
    return pl.pallas_call(
        _rmsnorm_kernel,
        grid=(m // tm,),
        in_specs=[pl.BlockSpec((tm, d), lambda i: (i, 0)), pl.BlockSpec((1, d), lambda i: (0, 0))],
        out_specs=pl.BlockSpec((tm, d), lambda i: (i, 0)),
        out_shape=jax.ShapeDtypeStruct((m, d), BF16),
        compiler_params=_params(("parallel",)),
    )(x, g.reshape(1, d))


def _mm_kernel(a_ref, b_ref, o_ref):
    o_ref[...] = _dot(a_ref[...], b_ref[...]).astype(o_ref.dtype)


def _matmul(a, b, bm, bn, out_dtype=F32):
    m, k = a.shape
    n = b.shape[1]
    bm, bn = min(bm, m), min(bn, n)
    return pl.pallas_call(
        _mm_kernel,
        grid=(n // bn, m // bm),
        in_specs=[pl.BlockSpec((bm, k), lambda j, i: (i, 0)), pl.BlockSpec((k, bn), lambda j, i: (0, j))],
        out_specs=pl.BlockSpec((bm, bn), lambda j, i: (i, j)),
        out_shape=jax.ShapeDtypeStruct((m, n), out_dtype),
        compiler_params=_params(("parallel", "parallel")),
    )(a, b)


def _rw_prep_kernel(p_ref, mu_ref, w0_ref, wup_ref, a0_ref, aup_ref, gup_ref, kk_w_ref, ka_ref,
                    r_out, k_out, v_out, lw_out, kk_out, ic_out, g_out, carry_ref):
    tm = p_ref.shape[0]

    @pl.when(pl.program_id(0) == 0)
    def _():
        carry_ref[...] = jnp.zeros_like(carry_ref)

    p = p_ref[...]
    rows = lax.broadcasted_iota(jnp.int32, p.shape, 0)
    prev = jnp.where(rows == 0, carry_ref[...], pltpu.roll(p, 1, 0))
    carry_ref[...] = p[tm - 1:tm, :]
    ps = p + (prev - p) * mu_ref[...]

    w = WIDTH
    r, k, v = ps[:, 0:w], ps[:, w:2 * w], ps[:, 2 * w:3 * w]
    o = 3 * w
    wd = ps[:, o:o + DECAY_PAD]
    ad = ps[:, o + DECAY_PAD:o + DECAY_PAD + ICLR_PAD]
    gd = ps[:, o + DECAY_PAD + ICLR_PAD:o + DECAY_PAD + ICLR_PAD + GATE_PAD]

    z = w0_ref[...] + _dot(jnp.tanh(wd), wup_ref[...], HI)
    logw = -_softplus(-z) - 0.5
    lw_out[...] = -jnp.exp(logw)
    iclr = _sigmoid(a0_ref[...] + _dot(ad, aup_ref[...], HI))
    g_out[...] = _dot(_sigmoid(gd), gup_ref[...], HI)
    kk = k * kk_w_ref[...]
    ss = _headsum(kk * kk, _head_ones())
    kk_out[...] = kk * lax.rsqrt(jnp.maximum(ss, 1e-24))
    r_out[...] = r
    k_out[...] = k * (1.0 + (iclr - 1.0) * ka_ref[...])
    v_out[...] = v
    ic_out[...] = iclr


def _rw_prep(p_rw, mu_p, w0, wup_p, a0, aup_p, gup_p, k_k, k_a, tm):
    m = p_rw.shape[0]
    tm = min(tm, m)
    row = lambda a: a.reshape(1, -1)
    full = lambda a: pl.BlockSpec(a.shape, lambda i: (0, 0))
    ins = [p_rw, row(mu_p), row(w0), wup_p, row(a0), aup_p, gup_p, row(k_k), row(k_a)]
    out = jax.ShapeDtypeStruct((m, WIDTH), F32)
    return pl.pallas_call(
        _rw_prep_kernel,
        grid=(m // tm,),
        in_specs=[pl.BlockSpec((tm, RW_PAD), lambda i: (i, 0))] + [full(a) for a in ins[1:]],
        out_specs=[pl.BlockSpec((tm, WIDTH), lambda i: (i, 0))] * 7,
        out_shape=[out] * 7,
        scratch_shapes=[pltpu.VMEM((1, RW_PAD), F32)],
        compiler_params=_params(("arbitrary",)),
    )(*ins)


def _bf(x):
    return [x.astype(BF16)]


def _unit_lower_inverse(a_list, sub_mask, eye):
    d = [jnp.where(sub_mask, a, 0.0) for a in a_list]
    e = [a - x for a, x in zip(a_list, d)]
    p = [eye + x for x in d]
    dp = [_bf(x) for x in d]
    for _ in range(SUB.bit_length() - 2):
        dp = [_bf(_mm(x, x)) for x in dp]
        p = [x + _mm(_bf(x), y) for x, y in zip(p, dp)]
    f = [_mm(_bf(x), _bf(y)) for x, y in zip(p, e)]
    g = [eye + x for x in f]
    fp = [_bf(x) for x in f]
    for _ in range((CHUNK // SUB).bit_length() - 2):
        fp = [_bf(_mm(x, x)) for x in fp]
        g = [x + _mm(_bf(x), y) for x, y in zip(g, fp)]
    return [_mm(_bf(x), _bf(y)) for x, y in zip(g, p)]


def _rw_scan_kernel(r_ref, lw_ref, k_ref, v_ref, kk_ref, ic_ref, g_ref, rk_ref, lnw_ref, lnb_ref,
                    o_ref, h_ref):
    t_tile = r_ref.shape[0]
    c = CHUNK

    @pl.when(pl.program_id(1) == 0)
    def _():
        h_ref[...] = jnp.zeros_like(h_ref)

    lane = lax.broadcasted_iota(jnp.int32, (c, LANES), 1)
    head_masks = (lane < HEAD_DIM, lane >= HEAD_DIM)
    ri = lax.broadcasted_iota(jnp.int32, (c, c), 0)
    ci = lax.broadcasted_iota(jnp.int32, (c, c), 1)
    strict = ri > ci
    incl = ri >= ci
    tril = _bf(incl)
    eye = (ri == ci).astype(F32)
    sub_mask = (ri // SUB) == (ci // SUB)
    ones_f = _head_ones()
    ones = _bf(ones_f)
    pair_mask = ones_f > 0.5
    rk, lnw, lnb = rk_ref[...], lnw_ref[...], lnb_ref[...]
    chunks = range(t_tile // c)
    units = [(s, h) for s in chunks for h in range(2)]
    sel = lambda pair: jnp.where(head_masks[0], pair[0], pair[1])

    sls = [slice(s * c, (s + 1) * c) for s in chunks]
    r = [r_ref[sl, :] for sl in sls]
    lw = [lw_ref[sl, :] for sl in sls]
    k = [k_ref[sl, :] for sl in sls]
    v = [v_ref[sl, :] for sl in sls]
    kk = [kk_ref[sl, :] for sl in sls]
    ic = [ic_ref[sl, :] for sl in sls]
    cl = [_mm(tril, _split(x, 3)) for x in lw]
    cl_end = [x[c - 1:c, :] for x in cl]
    b = [kk[s] * ic[s] for s in chunks]
    rt = [r[s] * jnp.exp(cl[s]) for s in chunks]
    at = [-kk[s] * jnp.exp(cl[s] - lw[s]) for s in chunks]
    e_neg = [jnp.exp(-x) for x in cl]
    bt = [_bf(b[s] * e_neg[s]) for s in chunks]
    kt = [_bf(k[s] * e_neg[s]) for s in chunks]
    e_end = [jnp.exp(cl_end[s] - cl[s]) for s in chunks]
    b_end = [_bf(b[s] * e_end[s]) for s in chunks]
    k_end = [_bf(k[s] * e_end[s]) for s in chunks]
    vs = [_bf(x) for x in v]

    at_h = [jnp.where(head_masks[h], at[s], 0.0) for s, h in units]
    lhs = [_bf(jnp.concatenate([at_h[i], jnp.where(head_masks[h], rt[s], 0.0)], axis=0))
           for i, (s, h) in enumerate(units)]
    sb = [_mm(lhs[i], bt[s], "nt") for i, (s, h) in enumerate(units)]
    sk = [_mm(lhs[i], kt[s], "nt") for i, (s, h) in enumerate(units)]
    akv = [_mm(_bf(jnp.where(strict, sk[i][:c], 0.0)), vs[s]) for i, (s, h) in enumerate(units)]
    rkv = [_mm(_bf(jnp.where(incl, sk[i][c:], 0.0)), vs[s]) for i, (s, h) in enumerate(units)]
    rb = [_bf(jnp.where(incl, x[c:], 0.0)) for x in sb]
    vk = [_mm(vs[s], k_end[s], "tn") for s in chunks]
    t_inv = [_bf(x) for x in _unit_lower_inverse([jnp.where(strict, x[:c], 0.0) for x in sb], sub_mask, eye)]
    w_h = [_mm(t_inv[i], _bf(at_h[i])) for i in range(len(units))]
    tk_h = [_mm(t_inv[i], _bf(akv[i])) for i in range(len(units))]
    s0_lhs = [_bf(jnp.concatenate([w_h[2 * s] + w_h[2 * s + 1], rt[s]], axis=0)) for s in chunks]
    tk = [sel(tk_h[2 * s:2 * s + 2]) for s in chunks]
    decay_end = [jnp.exp(x) for x in cl_end]

    ht = h_ref[...]
    us, rs0 = [], []
    for s in chunks:
        s0 = _mm(s0_lhs[s], _bf(ht), "nt")
        us.append(_bf(s0[:c] + tk[s]))
        rs0.append(s0[c:])
        ht = ht * decay_end[s] + jnp.where(pair_mask, _mm(us[s], b_end[s], "tn") + vk[s], 0.0)
    h_ref[...] = ht

    rbu = [_mm(rb[i], us[s]) for i, (s, h) in enumerate(units)]
    y = [rs0[s] + sel([rbu[2 * s] + rkv[2 * s], rbu[2 * s + 1] + rkv[2 * s + 1]]) for s in chunks]
    mean = [_mm(_split(x, 2), ones) * (1.0 / HEAD_DIM) for x in y]
    dlt = [y[s] - mean[s] for s in chunks]
    var = [_mm(_split(x * x, 2), ones) * (1.0 / HEAD_DIM) for x in dlt]
    bonus = [_mm(_split(r[s] * k[s] * rk, 2), ones) * v[s] for s in chunks]
    for s in chunks:
        yn = dlt[s] * lax.rsqrt(var[s] + GN_EPS) * lnw + lnb
        o_ref[sls[s], :] = ((yn + bonus[s]) * g_ref[sls[s], :]).astype(o_ref.dtype)


def _rw_scan(r, lw, k, v, kk, ic, g, r_k, ln_w, ln_b, t_tile):
    m = r.shape[0]
    t_tile = min(t_tile, m)
    seq = pl.BlockSpec((t_tile, LANES), lambda p, t: (t, p))
    vec = pl.BlockSpec((1, LANES), lambda p, t: (0, p))
    row = lambda a: a.reshape(1, -1)
    return pl.pallas_call(
        _rw_scan_kernel,
        grid=(N_PAIRS, m // t_tile),
        in_specs=[seq] * 7 + [vec] * 3,
        out_specs=seq,
        out_shape=jax.ShapeDtypeStruct((m, WIDTH), BF16),
        scratch_shapes=[pltpu.VMEM((LANES, LANES), F32)],
        compiler_params=_params(("parallel", "arbitrary")),
    )(r, lw, k, v, kk, ic, g, row(r_k), row(ln_w), row(ln_b))


def _fox_prep_kernel(p_ref, qn_ref, kn_ref, bf_ref, part_ref, q_out, k_out, vt_out, ka_out, carry_ref):
    tm = p_ref.shape[0]

    @pl.when(pl.program_id(0) == 0)
    def _():
        carry_ref[...] = jnp.zeros_like(carry_ref)

    w = WIDTH
    ones = _head_ones()
    q, k = p_ref[:, 0:w], p_ref[:, w:2 * w]
    q = q * lax.rsqrt(_headsum(q * q, ones) * (1.0 / HEAD_DIM) + NORM_EPS) * qn_ref[...]
    k = k * lax.rsqrt(_headsum(k * k, ones) * (1.0 / HEAD_DIM) + NORM_EPS) * kn_ref[...]
    q_out[...] = (q * (HEAD_DIM ** -0.5 * LOG2E)).astype(q_out.dtype)
    k_out[...] = k.astype(k_out.dtype)
    vt_out[...] = p_ref[:, 2 * w:3 * w].T.astype(vt_out.dtype)

    logf =-_softplus(-(p_ref[:, 3 * w:3 * w + FZ_PAD] + bf_ref[...]))
    ri = lax.broadcasted_iota(jnp.int32, (tm, tm), 0)
    ci = lax.broadcasted_iota(jnp.int32, (tm, tm), 1)
    cum = _mm(_bf(ri >= ci), _split(logf, 3)) + carry_ref[...]
    carry_ref[...] = cum[tm - 1:tm, :]
    hi, mid, lo = _split(cum[:, 0:LANES] * (-LOG2E), 3)
    part = part_ref[...]
    ka_out[...] = jnp.where(part == 0, hi, jnp.where(part == 1, mid, lo))


def _fox_prep(p_fox, q_norm, k_norm, bf_rep, tm):
    m = p_fox.shape[0]
    tm = min(tm, m)
    qn = jnp.tile(q_norm, N_HEADS).reshape(1, WIDTH)
    kn = jnp.tile(k_norm, N_HEADS).reshape(1, WIDTH)
    part = (jnp.arange(LANES, dtype=jnp.int32) % BIAS_PARTS).reshape(1, LANES)
    full = lambda a: pl.BlockSpec(a.shape, lambda i: (0, 0))
    act = jax.ShapeDtypeStruct((m, WIDTH), BF16)
    return pl.pallas_call(
        _fox_prep_kernel,
        grid=(m // tm,),
        in_specs=[pl.BlockSpec((tm, FOX_PAD), lambda i: (i, 0)), full(qn), full(kn), full(bf_rep), full(part)],
        out_specs=[pl.BlockSpec((tm, WIDTH), lambda i: (i, 0))] * 2
        + [pl.BlockSpec((WIDTH, tm), lambda i: (0, i)), pl.BlockSpec((tm, LANES), lambda i: (i, 0))],
        out_shape=[act, act, jax.ShapeDtypeStruct((WIDTH, m), BF16), jax.ShapeDtypeStruct((m, LANES), BF16)],
        scratch_shapes=[pltpu.VMEM((1, FZ_PAD), F32)],
        compiler_params=_params(("arbitrary",)),
    )(p_fox, qn, kn, bf_rep, part)


def _fox_attn_kernel(qi_ref, kb_ref, kind_ref, q_ref, k_ref, ka_ref, vt_ref, o_ref, m_ref, acc_ref):
    tq = q_ref.shape[0]
    pair, step_id = pl.program_id(0), pl.program_id(1)
    kb, kind = kb_ref[step_id], kind_ref[step_id]

    @pl.when(kb == 0)
    def _():
        m_ref[...] = jnp.full_like(m_ref, -jnp.inf)
        acc_ref[...] = jnp.zeros_like(acc_ref)

    def step(kinds):
        q = q_ref[...]
        lane_q = lax.broadcasted_iota(jnp.int32, (tq, LANES), 1)
        q_full = []
        for h in range(2):
            hmask = (lane_q < HEAD_DIM) if h == 0 else (lane_q >= HEAD_DIM)
            first = (2 * pair + h) * BIAS_PARTS
            aug = jnp.logical_and(lane_q >= first, lane_q < first + BIAS_PARTS).astype(BF16)
            q_full.append(jnp.concatenate([jnp.where(hmask, q, jnp.zeros_like(q)), aug], axis=1))
        subs = [slice(b * tq, (b + 1) * tq) for b in range(len(kinds))]
        k_full = [jnp.concatenate([k_ref[sl, :], ka_ref[sl, :]], axis=1) for sl in subs]
        ones_rows = jnp.ones((ONES_ROWS, tq), BF16)
        v_aug = [jnp.concatenate([vt_ref[:, sl], ones_rows], axis=0) for sl in subs]
        s = [[_mm([k_full[b]], [q_full[h]], "nt") for h in range(2)] for b in range(len(kinds))]
        causal = lax.broadcasted_iota(jnp.int32, (tq, tq), 0) <= lax.broadcasted_iota(jnp.int32, (tq, tq), 1)
        m = [m_ref[h] for h in range(2)]
        acc = [acc_ref[h] for h in range(2)]
        for b, diagonal in enumerate(kinds):
            sb = [jnp.where(causal, x, -jnp.inf) for x in s[b]] if diagonal else s[b]
            m_new = [jnp.maximum(m[h], jnp.max(sb[h], axis=0, keepdims=True)) for h in range(2)]
            p = [jnp.exp2(sb[h] - m_new[h]).astype(BF16) for h in range(2)]
            pv = [_mm([v_aug[b]], [p[h]]) for h in range(2)]
            acc = [jnp.exp2(m[h] - m_new[h]) * acc[h] + pv[h] for h in range(2)]
            m = m_new
        for h in range(2):
            acc_ref[h] = acc[h]
            m_ref[h] = m[h]

    def finish():
        row = lax.broadcasted_iota(jnp.int32, (LANES, tq), 0)
        o = [acc_ref[h, 0:LANES, :] / acc_ref[h, LANES:LANES + 1, :] for h in range(2)]
        o_ref[...] = jnp.where(row < HEAD_DIM, o[0], o[1]).T.astype(o_ref.dtype)

    @pl.when(kind == 0)
    def _():
        step((False, False))

    @pl.when(kind == 1)
    def _():
        step((False, True))
        finish()

    @pl.when(kind == 2)
    def _():
        step((True,))
        finish()


def _fox_attn(q, k, ka, vt, tq):
    m = q.shape[0]
    tq = min(tq, m)
    nq = m // tq
    assert nq % 2 == 0, "key blocks are fetched in pairs"
    steps = []
    for i in range(nq):
        steps += [(i, j, 0) for j in range(i // 2)] + [(i, i // 2, 1 if i % 2 else 2)]
    qi_tab, kb_tab, kind_tab = (jnp.asarray([st[c] for st in steps], jnp.int32) for c in range(3))
    qspec = pl.BlockSpec((tq, LANES), lambda p, t, qi, kb, kind: (qi[t], p))
    kspec = pl.BlockSpec((2 * tq, LANES), lambda p, t, qi, kb, kind: (kb[t], p))
    kaspec = pl.BlockSpec((2 * tq, LANES), lambda p, t, qi, kb, kind: (kb[t], 0))
    vtspec = pl.BlockSpec((LANES, 2 * tq), lambda p, t, qi, kb, kind: (p, kb[t]))
    return pl.pallas_call(
        _fox_attn_kernel,
        grid_spec=pltpu.PrefetchScalarGridSpec(
            num_scalar_prefetch=3,
            grid=(N_PAIRS, len(steps)),
            in_specs=[qspec, kspec, kaspec, vtspec],
            out_specs=qspec,
            scratch_shapes=[pltpu.VMEM((2, 1, tq), F32), pltpu.VMEM((2, LANES + ONES_ROWS, tq), F32)],
        ),
        out_shape=jax.ShapeDtypeStruct((m, WIDTH), BF16),
        compiler_params=_params(("parallel", "arbitrary")),
    )(qi_tab, kb_tab, kind_tab, q, k, ka, vt)


def _mem_kv_kernel(mem_ref, g_ref, w_ref, o_ref):
    x = mem_ref[...]
    ms = jnp.mean(x * x, axis=-1, keepdims=True)
    xn = (x * lax.rsqrt(ms + NORM_EPS) * g_ref[...]).astype(BF16)
    o_ref[...] = _dot(xn, w_ref[...]).astype(o_ref.dtype)


def _mem_kv(mem, ln_mem, w_ckv):
    n, d = mem.shape
    ins = [mem, ln_mem.reshape(1, d), w_ckv]
    return pl.pallas_call(
        _mem_kv_kernel,
        grid=(1,),
        in_specs=[pl.BlockSpec(a.shape, lambda i: (0, 0)) for a in ins],
        out_specs=pl.BlockSpec((n, w_ckv.shape[1]), lambda i: (0, 0)),
        out_shape=jax.ShapeDtypeStruct((n, w_ckv.shape[1]), BF16),
        compiler_params=_params(("arbitrary",)),
    )(*ins)


def _post_kernel(ya_ref, yb_ref, pg_ref, x_ref, wa_ref, wb_ref, wo_ref, lnc_ref, wcq_ref, kv_ref, wco_ref, o_ref):
    d = x_ref.shape[1]
    xw = X_HEADS * X_HEAD_DIM
    ga = _sigmoid(pg_ref[:, 0:d])
    gb = _sigmoid(pg_ref[:, d:2 * d])
    merged = ga * _dot(ya_ref[...], wa_ref[...]) + gb * _dot(yb_ref[...], wb_ref[...])
    h = x_ref[...] + _dot(merged.astype(BF16), wo_ref[...])

    ms = jnp.mean(h * h, axis=-1, keepdims=True)
    hn = (h * lax.rsqrt(ms + NORM_EPS) * lnc_ref[...]).astype(BF16)
    q = _dot(hn, wcq_ref[...])
    outs = []
    for hd in range(X_HEADS):
        cols = slice(hd * X_HEAD_DIM, (hd + 1) * X_HEAD_DIM)
        vcols = slice(xw + hd * X_HEAD_DIM, xw + (hd + 1) * X_HEAD_DIM)
        s = _dot_t(q[:, cols].astype(BF16), kv_ref[:, cols]) * (X_HEAD_DIM ** -0.5)
        s = s - jnp.max(s, axis=1, keepdims=True)
        e = jnp.exp(s)
        pr = e / jnp.sum(e, axis=1, keepdims=True)
        outs.append(_dot(pr.astype(BF16), kv_ref[:, vcols]))
    o = jnp.concatenate(outs, axis=1).astype(BF16)
    o_ref[...] = h + _dot(o, wco_ref[...])


def _post(ya, yb, pg, x, wa, wb, wo, ln_cross, wcq, kv, wco, tm):
    m, d = x.shape
    tm = min(tm, m)
    lnc = ln_cross.reshape(1, d)
    rows = lambda a: pl.BlockSpec((tm, a.shape[1]), lambda i: (i, 0))
    const = lambda a: pl.BlockSpec(a.shape, lambda i: (0, 0), pipeline_mode=pl.Buffered(1))
    return pl.pallas_call(
        _post_kernel,
        grid=(m // tm,),
        in_specs=[rows(ya), rows(yb), rows(pg), rows(x)] + [const(a) for a in (wa, wb, wo, lnc, wcq, kv, wco)],
        out_specs=pl.BlockSpec((tm, d), lambda i: (i, 0)),
        out_shape=jax.ShapeDtypeStruct((m, d), F32),
        compiler_params=_params(("parallel",)),
    )(ya, yb, pg, x, wa, wb, wo, lnc, wcq, kv, wco)


def _mlp_kernel(h_ref, g_ref, wu_ref, wd_ref, gf_ref, o_ref, xn_ref, acc_ref):
    f = pl.program_id(1)

    @pl.when(f == 0)
    def _():
        h = h_ref[...]
        ms = jnp.mean(h * h, axis=-1, keepdims=True)
        xn_ref[...] = (h * lax.rsqrt(ms + NORM_EPS) * g_ref[...]).astype(xn_ref.dtype)
        acc_ref[...] = jnp.zeros_like(acc_ref)

    u = jnp.maximum(_dot(xn_ref[...], wu_ref[...]), 0.0)
    acc_ref[...] += _dot((u * u).astype(BF16), wd_ref[...])

    @pl.when(f == pl.num_programs(1) - 1)
    def _():
        h = h_ref[...] + acc_ref[...]
        ms = jnp.mean(h * h, axis=-1, keepdims=True)
        o_ref[...] = h * lax.rsqrt(ms + NORM_EPS) * gf_ref[...]


def _mlp(h, ln_mlp, w_up, w_down, ln_final, tm, tf):
    m, d = h.shape
    dff = w_up.shape[1]
    tm, tf = min(tm, m), min(tf, dff)
    return pl.pallas_call(
        _mlp_kernel,
        grid=(m // tm, dff // tf),
        in_specs=[pl.BlockSpec((tm, d), lambda i, f: (i, 0)),
                  pl.BlockSpec((1, d), lambda i, f: (0, 0)),
                  pl.BlockSpec((d, tf), lambda i, f: (0, f)),
                  pl.BlockSpec((tf, d), lambda i, f: (f, 0)),
                  pl.BlockSpec((1, d), lambda i, f: (0, 0))],
        out_specs=pl.BlockSpec((tm, d), lambda i, f: (i, 0)),
        out_shape=jax.ShapeDtypeStruct((m, d), F32),
        scratch_shapes=[pltpu.VMEM((tm, d), BF16), pltpu.VMEM((tm, d), F32)],
        compiler_params=_params(("parallel", "arbitrary")),
    )(h, ln_mlp.reshape(1, d), w_up, w_down, ln_final.reshape(1, d))


def _pad_cols(a, width):
    return jnp.pad(a, ((0, 0), (0, width - a.shape[1])))


def _pad_rows(a, height):
    return jnp.pad(a, ((0, height - a.shape[0]), (0, 0)))


def _layer(h, mem, ln_mix, w_in, rw_mu, rw_w0, rw_w_up, rw_a0, rw_a_up, rw_g_up, rw_k_k, rw_k_a, rw_r_k,
           rw_ln_w, rw_ln_b, fox_b_f, fox_q_norm, fox_k_norm, w_proj_a, w_proj_b, w_out, ln_cross, ln_mem,
           w_cq, w_ckv, w_co, ln_mlp, w_up, w_down, ln_final):
    w3 = 3 * WIDTH
    o_ad = w3 + DECAY_LORA
    o_gd = o_ad + ICLR_LORA
    rw_cols = o_gd + GATE_LORA
    fox_cols = w3 + N_HEADS

    def rw_layout(a):
        return jnp.concatenate([a[:, :w3], _pad_cols(a[:, w3:o_ad], DECAY_PAD), _pad_cols(a[:, o_ad:o_gd], ICLR_PAD),
                                _pad_cols(a[:, o_gd:rw_cols], GATE_PAD)], axis=1)

    w_rw = rw_layout(w_in[:, :rw_cols]).astype(BF16)
    rep = jnp.repeat(jnp.arange(N_HEADS), BIAS_PARTS)
    w_fz = w_in[:, rw_cols + w3:rw_cols + fox_cols][:, rep]
    w_fox = _pad_cols(jnp.concatenate([w_in[:, rw_cols:rw_cols + w3], w_fz], axis=1), FOX_PAD).astype(BF16)
    bf_rep = jnp.pad(fox_b_f[rep], (0, FZ_PAD - N_HEADS * BIAS_PARTS)).reshape(1, FZ_PAD)
    w_gate = w_in[:, rw_cols + fox_cols:].astype(BF16)
    mu_p = rw_layout(rw_mu.reshape(1, -1)).reshape(-1)

    xn = _rmsnorm_bf16(h, ln_mix, 512)
    p_rw = _matmul(xn, w_rw, 1024, 1792)
    p_fox = _matmul(xn, w_fox, 1024, 1664)
    p_gate = _matmul(xn, w_gate, 1024, 2048)

    r, k, v, lw, kk, ic, g = _rw_prep(p_rw, mu_p, rw_w0, _pad_rows(rw_w_up, DECAY_PAD), rw_a0,
                                      _pad_rows(rw_a_up, ICLR_PAD), _pad_rows(rw_g_up, GATE_PAD), rw_k_k, rw_k_a, 256)
    y_a = _rw_scan(r, lw, k, v, kk, ic, g, rw_r_k.reshape(-1), rw_ln_w, rw_ln_b, 512)

    fq, fk, fvt, fka = _fox_prep(p_fox, fox_q_norm, fox_k_norm, bf_rep, 512)
    y_b = _fox_attn(fq, fk, fka, fvt, 512)

    kv = _mem_kv(mem, ln_mem, w_ckv.astype(BF16))
    h = _post(y_a, y_b, p_gate, h, w_proj_a.astype(BF16), w_proj_b.astype(BF16), w_out.astype(BF16), ln_cross,
              w_cq.astype(BF16), kv, w_co.astype(BF16), 256)
    return _mlp(h, ln_mlp, w_up.astype(BF16), w_down.astype(BF16), ln_final, 512, 1024)


def kernel(x, mem, ln_mix, w_in, rw_mu, rw_w0, rw_w_up, rw_a0, rw_a_up, rw_g_up, rw_k_k, rw_k_a, rw_r_k, rw_ln_w, rw_ln_b, fox_b_f, fox_q_norm, fox_k_norm, w_proj_a, w_proj_b, w_out, ln_cross, ln_mem, w_cq, w_ckv, w_co, ln_mlp, w_up, w_down, ln_final):
    b, s, d = x.shape
    assert b == 1 and ln_mix.shape[0] == 1, "single sequence, single layer"
    out = _layer(x[0], mem[0], ln_mix[0], w_in[0], rw_mu[0], rw_w0[0], rw_w_up[0], rw_a0[0], rw_a_up[0], rw_g_up[0],
                 rw_k_k[0], rw_k_a[0], rw_r_k[0], rw_ln_w[0], rw_ln_b[0], fox_b_f[0], fox_q_norm[0], fox_k_norm[0],
                 w_proj_a[0], w_proj_b[0], w_out[0], ln_cross[0], ln_mem[0], w_cq[0], w_ckv[0], w_co[0],
                 ln_mlp[0], w_up[0], w_down[0], ln_final)
    return out.reshape(b, s, d)
```

```python
import functools

import jax
import jax.numpy as jnp
from jax import lax
from jax.experimental import pallas as pl
from jax.experimental.pallas import tpu as pltpu

F32 = jnp.float32
BF16 = jnp.bfloat16
HI = lax.Precision.HIGHEST

LANES = 128
HEAD_DIM = 64
N_HEADS = 16
WIDTH = N_HEADS * HEAD_DIM
N_PAIRS = WIDTH // LANES
DECAY_LORA, ICLR_LORA, GATE_LORA = 64, 64, 160
DECAY_PAD, ICLR_PAD, GATE_PAD = 128, 128, 256
RW_PAD = 3 * WIDTH + DECAY_PAD + ICLR_PAD + GATE_PAD
FZ_PAD = 256
BIAS_PARTS = 3
LOG2E = 1.4426950408889634
ONES_ROWS = 16
FOX_PAD = 3 * WIDTH + FZ_PAD
X_HEADS, X_HEAD_DIM = 4, 128
NORM_EPS = 1e-5
GN_EPS = 64e-5
CHUNK = 64
SUB = 16
VMEM_LIMIT = 56 * 1024 * 1024


def _params(sem, vmem=VMEM_LIMIT):
    return pltpu.CompilerParams(dimension_semantics=sem, vmem_limit_bytes=vmem)


def _dot(a, b, precision=None):
    return jnp.dot(a, b, preferred_element_type=F32, precision=precision)


def _dot_t(a, b, precision=None):
    return lax.dot_general(a, b, (((1,), (1,)), ((), ())), preferred_element_type=F32, precision=precision)


def _dot_tt(a, b, precision=None):
    return lax.dot_general(a, b, (((0,), (0,)), ((), ())), preferred_element_type=F32, precision=precision)


def _split(x, terms):
    parts, rem = [], x
    for i in range(terms):
        part = rem.astype(BF16)
        parts.append(part)
        if i + 1 < terms:
            rem = rem - part.astype(F32)
    return parts


_DIMS = {"nn": (((1,), (0,)), ((), ())), "nt": (((1,), (1,)), ((), ())), "tn": (((0,), (0,)), ((), ()))}


def _mm(a_parts, b_parts, kind="nn"):
    order = max(len(a_parts), len(b_parts))
    acc = None
    for i, a in enumerate(a_parts):
        for j, b in enumerate(b_parts):
            if i + j < order:
                term = lax.dot_general(a, b, _DIMS[kind], preferred_element_type=F32)
                acc = term if acc is None else acc + term
    return acc


def _softplus(x):
    return jnp.maximum(x, 0.0) + jnp.log(1.0 + jnp.exp(-jnp.abs(x)))


def _sigmoid(x):
    return 1.0 / (1.0 + jnp.exp(-x))


def _head_ones():
    r = lax.broadcasted_iota(jnp.int32, (LANES, LANES), 0) // HEAD_DIM
    c = lax.broadcasted_iota(jnp.int32, (LANES, LANES), 1) // HEAD_DIM
    return (r == c).astype(F32)


def _headsum(x, ones):
    parts = [_dot(x[:, c * LANES:(c + 1) * LANES], ones, HI) for c in range(x.shape[1] // LANES)]
    return parts[0] if len(parts) == 1 else jnp.concatenate(parts, axis=1)


def _rmsnorm_kernel(x_ref, g_ref, o_ref):
    x = x_ref[...]
    ms = jnp.mean(x * x, axis=-1, keepdims=True)
    o_ref[...] = (x * lax.rsqrt(ms + NORM_EPS) * g_ref[...]).astype(o_ref.dtype)


def _rmsnorm_bf16(x, g, tm):
    m, d = x.shape
    tm = min(tm, m)
    return pl.pallas_call(
        _rmsnorm_kernel,
        grid=(m // tm,),
        in_specs=[pl.BlockSpec((tm, d), lambda i: (i, 0)), pl.BlockSpec((1, d), lambda i: (0, 0))],
        out_specs=pl.BlockSpec((tm, d), lambda i: (i, 0)),
        out_shape=jax.ShapeDtypeStruct((m, d), BF16),
        compiler_params=_params(("parallel",)),
    )(x, g.reshape(1, d))


def _mm_kernel(a_ref, b_ref, o_ref):
    o_ref[...] = _dot(a_ref[...], b_ref[...]).astype(o_ref.dtype)


def _matmul(a, b, bm, bn, out_dtype=F32):
    m, k = a.shape
    n = b.shape[1]
    bm, bn = min(bm, m), min(bn, n)
    return pl.pallas_call(
        _mm_kernel,
        grid=(n // bn, m // bm),
        in_specs=[pl.BlockSpec((bm, k), lambda j, i: (i, 0)), pl.BlockSpec((k, bn), lambda j, i: (0, j))],
        out_specs=pl.BlockSpec((bm, bn), lambda j, i: (i, j)),
        out_shape=jax.ShapeDtypeStruct((m, n), out_dtype),
        compiler_params=_params(("parallel", "parallel")),
    )(a, b)


def _rw_prep_kernel(p_ref, mu_ref, w0_ref, wup_ref, a0_ref, aup_ref, gup_ref, kk_w_ref, ka_ref,
                    r_out, k_out, v_out, lw_out, kk_out, ic_out, g_out, carry_ref):
    tm = p_ref.shape[0]

    @pl.when(pl.program_id(0) == 0)
    def _():
        carry_ref[...] = jnp.zeros_like(carry_ref)

    p = p_ref[...]
    rows = lax.broadcasted_iota(jnp.int32, p.shape, 0)
    prev = jnp.where(rows == 0, carry_ref[...], pltpu.roll(p, 1, 0))
    carry_ref[...] = p[tm - 1:tm, :]
    ps = p + (prev - p) * mu_ref[...]

    w = WIDTH
    r, k, v = ps[:, 0:w], ps[:, w:2 * w], ps[:, 2 * w:3 * w]
    o = 3 * w
    wd = ps[:, o:o + DECAY_PAD]
    ad = ps[:, o + DECAY_PAD:o + DECAY_PAD + ICLR_PAD]
    gd = ps[:, o + DECAY_PAD + ICLR_PAD:o + DECAY_PAD + ICLR_PAD + GATE_PAD]

    z = w0_ref[...] + _dot(jnp.tanh(wd), wup_ref[...], HI)
    logw = -_softplus(-z) - 0.5
    lw_out[...] = -jnp.exp(logw)
    iclr = _sigmoid(a0_ref[...] + _dot(ad, aup_ref[...], HI))
    g_out[...] = _dot(_sigmoid(gd), gup_ref[...], HI)
    kk = k * kk_w_ref[...]
    ss = _headsum(kk * kk, _head_ones())
    kk_out[...] = kk * lax.rsqrt(jnp.maximum(ss, 1e-24))
    r_out[...] = r
    k_out[...] = k * (1.0 + (iclr - 1.0) * ka_ref[...])
    v_out[...] = v
    ic_out[...] = iclr


def _rw_prep(p_rw, mu_p, w0, wup_p, a0, aup_p, gup_p, k_k, k_a, tm):
    m = p_rw.shape[0]
    tm = min(tm, m)
    row = lambda a: a.reshape(1, -1)
    full = lambda a: pl.BlockSpec(a.shape, lambda i: (0, 0))
    ins = [p_rw, row(mu_p), row(w0), wup_p, row(a0), aup_p, gup_p, row(k_k), row(k_a)]
    out = jax.ShapeDtypeStruct((m, WIDTH), F32)
    return pl.pallas_call(
        _rw_prep_kernel,
        grid=(m // tm,),
        in_specs=[pl.BlockSpec((tm, RW_PAD), lambda i: (i, 0))] + [full(a) for a in ins[1:]],
        out_specs=[pl.BlockSpec((tm, WIDTH), lambda i: (i, 0))] * 7,
        out_shape=[out] * 7,
        scratch_shapes=[pltpu.VMEM((1, RW_PAD), F32)],
        compiler_params=_params(("arbitrary",)),
    )(*ins)


def _bf(x):
    return [x.astype(BF16)]


def _pair_diag(x, head0):
    xb = x.astype(BF16)
    zero = jnp.zeros_like(xb)
    return [jnp.concatenate([jnp.where(head0, xb, zero), jnp.where(head0, zero, xb)], axis=0)]


def _unit_lower_inverse(a_list, sub_mask, eye, head0):
    d = [jnp.where(sub_mask, a, 0.0) for a in a_list]
    e = [a - x for a, x in zip(a_list, d)]
    p = [eye + x for x in d]
    diag = lambda xs: [_pair_diag(x, head0) for x in xs]
    dp, dp_diag = d, diag(d)
    for _ in range(SUB.bit_length() - 2):
        dp = [_mm(_bf(x), y) for x, y in zip(dp, dp_diag)]
        dp_diag = diag(dp)
        p = [x + _mm(_bf(x), y) for x, y in zip(p, dp_diag)]
    f = [_mm(_bf(x), y) for x, y in zip(p, diag(e))]
    g = [eye + x for x in f]
    fp, fp_diag = f, diag(f)
    for _ in range((CHUNK // SUB).bit_length() - 2):
        fp = [_mm(_bf(x), y) for x, y in zip(fp, fp_diag)]
        fp_diag = diag(fp)
        g = [x + _mm(_bf(x), y) for x, y in zip(g, fp_diag)]
    return [_mm(_bf(x), y) for x, y in zip(g, diag(p))]


def _rw_scan_kernel(r_ref, lw_ref, k_ref, v_ref, kk_ref, ic_ref, g_ref, rk_ref, lnw_ref, lnb_ref,
                    o_ref, h_ref):
    c = CHUNK
    n_chunks = r_ref.shape[0] // c

    @pl.when(pl.program_id(0) == 0)
    def _():
        h_ref[...] = jnp.zeros_like(h_ref)

    lane = lax.broadcasted_iota(jnp.int32, (c, LANES), 1)
    head0 = lane < HEAD_DIM
    ri = lax.broadcasted_iota(jnp.int32, (c, LANES), 0)
    ci = jnp.bitwise_and(lane, HEAD_DIM - 1)
    strict = ri > ci
    incl = ri >= ci
    eye = (ri == ci).astype(F32)
    sub_mask = (ri // SUB) == (ci // SUB)
    tril = _bf(lax.broadcasted_iota(jnp.int32, (c, c), 0) >= lax.broadcasted_iota(jnp.int32, (c, c), 1))
    ones_f = _head_ones()
    ones = _bf(ones_f)
    pair_mask = ones_f > 0.5
    diag = lambda x: _pair_diag(x, head0)
    units = [(s, p) for s in range(n_chunks) for p in range(N_PAIRS)]
    rows = lambda s: slice(s * c, (s + 1) * c)
    cols = lambda p: slice(p * LANES, (p + 1) * LANES)

    pre = []
    for s in range(n_chunks):
        r, lw, k, v = r_ref[rows(s), :], lw_ref[rows(s), :], k_ref[rows(s), :], v_ref[rows(s), :]
        b = kk_ref[rows(s), :] * ic_ref[rows(s), :]
        cl = _mm(tril, _split(lw, 3))
        cl_end = cl[c - 1:c, :]
        e_neg = jnp.exp(-cl)
        e_end = jnp.exp(cl_end - cl)
        pre.append(dict(r=r, k=k, v=v, rt=r * jnp.exp(cl), at=-kk_ref[rows(s), :] * jnp.exp(cl - lw),
                        bt=b * e_neg, kt=k * e_neg, b_end=b * e_end, k_end=k * e_end, decay_end=jnp.exp(cl_end)))
    get = lambda name: [pre[s][name][:, cols(p)] for s, p in units]
    r, k, v, rt, at = get("r"), get("k"), get("v"), get("rt"), get("at")
    b_end, k_end = [_bf(x) for x in get("b_end")], [_bf(x) for x in get("k_end")]
    v_diag = [diag(x) for x in v]

    lhs = [_bf(jnp.concatenate([a, b], axis=0)) for a, b in zip(at, rt)]
    sb = [_mm(x, diag(y), "nt") for x, y in zip(lhs, get("bt"))]
    sk = [_mm(x, diag(y), "nt") for x, y in zip(lhs, get("kt"))]
    akv = [_mm(_bf(jnp.where(strict, x[:c], 0.0)), y) for x, y in zip(sk, v_diag)]
    rkv = [_mm(_bf(jnp.where(incl, x[c:], 0.0)), y) for x, y in zip(sk, v_diag)]
    rb = [_bf(jnp.where(incl, x[c:], 0.0)) for x in sb]
    vk = [_mm(_bf(x), y, "tn") for x, y in zip(v, k_end)]
    t_inv = [_bf(x) for x in _unit_lower_inverse([jnp.where(strict, x[:c], 0.0) for x in sb], sub_mask, eye, head0)]
    w = [_mm(x, diag(y)) for x, y in zip(t_inv, at)]
    tk = [_mm(x, diag(y)) for x, y in zip(t_inv, akv)]
    s0_lhs = [_bf(jnp.concatenate([a, b], axis=0)) for a, b in zip(w, rt)]

    ht = [h_ref[p] for p in range(N_PAIRS)]
    u, rs0 = [], []
    for s in range(n_chunks):
        i0 = s * N_PAIRS
        s0 = [_mm(s0_lhs[i0 + p], _bf(ht[p]), "nt") for p in range(N_PAIRS)]
        u += [s0[p][:c] + tk[i0 + p] for p in range(N_PAIRS)]
        rs0 += [s0[p][c:] for p in range(N_PAIRS)]
        ub = [_mm(_bf(u[i0 + p]), b_end[i0 + p], "tn") for p in range(N_PAIRS)]
        ht = [ht[p] * pre[s]["decay_end"][:, cols(p)] + jnp.where(pair_mask, ub[p] + vk[i0 + p], 0.0)
              for p in range(N_PAIRS)]
    for p in range(N_PAIRS):
        h_ref[p] = ht[p]

    y = [a + _mm(x, diag(z)) + b for a, x, z, b in zip(rs0, rb, u, rkv)]
    mean = [_mm(_split(x, 2), ones) * (1.0 / HEAD_DIM) for x in y]
    dlt = [a - b for a, b in zip(y, mean)]
    var = [_mm(_split(x * x, 2), ones) * (1.0 / HEAD_DIM) for x in dlt]
    bonus = [_mm(_split(r[i] * k[i] * rk_ref[:, cols(p)], 2), ones) * v[i] for i, (s, p) in enumerate(units)]
    for i, (s, p) in enumerate(units):
        yn = dlt[i] * lax.rsqrt(var[i] + GN_EPS) * lnw_ref[:, cols(p)] + lnb_ref[:, cols(p)]
        o_ref[rows(s), cols(p)] = ((yn + bonus[i]) * g_ref[rows(s), cols(p)]).astype(o_ref.dtype)


def _rw_scan(r, lw, k, v, kk, ic, g, r_k, ln_w, ln_b, t_tile):
    m = r.shape[0]
    t_tile = min(t_tile, m)
    seq = pl.BlockSpec((t_tile, WIDTH), lambda t: (t, 0))
    vec = pl.BlockSpec((1, WIDTH), lambda t: (0, 0))
    row = lambda a: a.reshape(1, -1)
    return pl.pallas_call(
        _rw_scan_kernel,
        grid=(m // t_tile,),
        in_specs=[seq] * 7 + [vec] * 3,
        out_specs=seq,
        out_shape=jax.ShapeDtypeStruct((m, WIDTH), BF16),
        scratch_shapes=[pltpu.VMEM((N_PAIRS, LANES, LANES), F32)],
        compiler_params=_params(("arbitrary",)),
    )(r, lw, k, v, kk, ic, g, row(r_k), row(ln_w), row(ln_b))


def _fox_prep_kernel(p_ref, qn_ref, kn_ref, bf_ref, part_ref, q_out, k_out, vt_out, ka_out, carry_ref):
    tm = p_ref.shape[0]

    @pl.when(pl.program_id(0) == 0)
    def _():
        carry_ref[...] = jnp.zeros_like(carry_ref)

    w = WIDTH
    ones = _head_ones()
    q, k = p_ref[:, 0:w], p_ref[:, w:2 * w]
    q = q * lax.rsqrt(_headsum(q * q, ones) * (1.0 / HEAD_DIM) + NORM_EPS) * qn_ref[...]
    k = k * lax.rsqrt(_headsum(k * k, ones) * (1.0 / HEAD_DIM) + NORM_EPS) * kn_ref[...]
    q_out[...] = (q * (HEAD_DIM ** -0.5 * LOG2E)).astype(q_out.dtype)
    k_out[...] = k.astype(k_out.dtype)
    vt_out[...] = p_ref[:, 2 * w:3 * w].T.astype(vt_out.dtype)

    logf =-_softplus(-(p_ref[:, 3 * w:3 * w + FZ_PAD] + bf_ref[...]))
    ri = lax.broadcasted_iota(jnp.int32, (tm, tm), 0)
    ci = lax.broadcasted_iota(jnp.int32, (tm, tm), 1)
    cum = _mm(_bf(ri >= ci), _split(logf, 3)) + carry_ref[...]
    carry_ref[...] = cum[tm - 1:tm, :]
    hi, mid, lo = _split(cum[:, 0:LANES] * (-LOG2E), 3)
    part = part_ref[...]
    ka_out[...] = jnp.where(part == 0, hi, jnp.where(part == 1, mid, lo))


def _fox_prep(p_fox, q_norm, k_norm, bf_rep, tm):
    m = p_fox.shape[0]
    tm = min(tm, m)
    qn = jnp.tile(q_norm, N_HEADS).reshape(1, WIDTH)
    kn = jnp.tile(k_norm, N_HEADS).reshape(1, WIDTH)
    part = (jnp.arange(LANES, dtype=jnp.int32) % BIAS_PARTS).reshape(1, LANES)
    full = lambda a: pl.BlockSpec(a.shape, lambda i: (0, 0))
    act = jax.ShapeDtypeStruct((m, WIDTH), BF16)
    return pl.pallas_call(
        _fox_prep_kernel,
        grid=(m // tm,),
        in_specs=[pl.BlockSpec((tm, FOX_PAD), lambda i: (i, 0)), full(qn), full(kn), full(bf_rep), full(part)],
        out_specs=[pl.BlockSpec((tm, WIDTH), lambda i: (i, 0))] * 2
        + [pl.BlockSpec((WIDTH, tm), lambda i: (0, i)), pl.BlockSpec((tm, LANES), lambda i: (i, 0))],
        out_shape=[act, act, jax.ShapeDtypeStruct((WIDTH, m), BF16), jax.ShapeDtypeStruct((m, LANES), BF16)],
        scratch_shapes=[pltpu.VMEM((1, FZ_PAD), F32)],
        compiler_params=_params(("arbitrary",)),
    )(p_fox, qn, kn, bf_rep, part)


def _fox_attn_kernel(qi_ref, kb_ref, kind_ref, q_ref, k_ref, ka_ref, vt_ref, o_ref, m_ref, acc_ref):
    tq = q_ref.shape[0]
    pair, step_id = pl.program_id(0), pl.program_id(1)
    kb, kind = kb_ref[step_id], kind_ref[step_id]

    @pl.when(kb == 0)
    def _():
        m_ref[...] = jnp.full_like(m_ref, -jnp.inf)
        acc_ref[...] = jnp.zeros_like(acc_ref)

    def step(kinds):
        q = q_ref[...]
        lane_q = lax.broadcasted_iota(jnp.int32, (tq, LANES), 1)
        q_full = []
        for h in range(2):
            hmask = (lane_q < HEAD_DIM) if h == 0 else (lane_q >= HEAD_DIM)
            first = (2 * pair + h) * BIAS_PARTS
            aug = jnp.logical_and(lane_q >= first, lane_q < first + BIAS_PARTS).astype(BF16)
            q_full.append(jnp.concatenate([jnp.where(hmask, q, jnp.zeros_like(q)), aug], axis=1))
        subs = [slice(b * tq, (b + 1) * tq) for b in range(len(kinds))]
        k_full = [jnp.concatenate([k_ref[sl, :], ka_ref[sl, :]], axis=1) for sl in subs]
        ones_rows = jnp.ones((ONES_ROWS, tq), BF16)
        v_aug = [jnp.concatenate([vt_ref[:, sl], ones_rows], axis=0) for sl in subs]
        s = [[_mm([k_full[b]], [q_full[h]], "nt") for h in range(2)] for b in range(len(kinds))]
        causal = lax.broadcasted_iota(jnp.int32, (tq, tq), 0) <= lax.broadcasted_iota(jnp.int32, (tq, tq), 1)
        m = [m_ref[h] for h in range(2)]
        acc = [acc_ref[h] for h in range(2)]
        for b, diagonal in enumerate(kinds):
            sb = [jnp.where(causal, x, -jnp.inf) for x in s[b]] if diagonal else s[b]
            m_new = [jnp.maximum(m[h], jnp.max(sb[h], axis=0, keepdims=True)) for h in range(2)]
            p = [jnp.exp2(sb[h] - m_new[h]).astype(BF16) for h in range(2)]
            pv = [_mm([v_aug[b]], [p[h]]) for h in range(2)]
            acc = [jnp.exp2(m[h] - m_new[h]) * acc[h] + pv[h] for h in range(2)]
            m = m_new
        for h in range(2):
            acc_ref[h] = acc[h]
            m_ref[h] = m[h]

    def finish():
        row = lax.broadcasted_iota(jnp.int32, (LANES, tq), 0)
        o = [acc_ref[h, 0:LANES, :] / acc_ref[h, LANES:LANES + 1, :] for h in range(2)]
        o_ref[...] = jnp.where(row < HEAD_DIM, o[0], o[1]).T.astype(o_ref.dtype)

    @pl.when(kind == 0)
    def _():
        step((False, False))

    @pl.when(kind == 1)
    def _():
        step((False, True))
        finish()

    @pl.when(kind == 2)
    def _():
        step((True,))
        finish()


def _fox_attn(q, k, ka, vt, tq):
    m = q.shape[0]
    tq = min(tq, m)
    nq = m // tq
    assert nq % 2 == 0, "key blocks are fetched in pairs"
    steps = []
    for i in range(nq):
        steps += [(i, j, 0) for j in range(i // 2)] + [(i, i // 2, 1 if i % 2 else 2)]
    qi_tab, kb_tab, kind_tab = (jnp.asarray([st[c] for st in steps], jnp.int32) for c in range(3))
    qspec = pl.BlockSpec((tq, LANES), lambda p, t, qi, kb, kind: (qi[t], p))
    kspec = pl.BlockSpec((2 * tq, LANES), lambda p, t, qi, kb, kind: (kb[t], p))
    kaspec = pl.BlockSpec((2 * tq, LANES), lambda p, t, qi, kb, kind: (kb[t], 0))
    vtspec = pl.BlockSpec((LANES, 2 * tq), lambda p, t, qi, kb, kind: (p, kb[t]))
    return pl.pallas_call(
        _fox_attn_kernel,
        grid_spec=pltpu.PrefetchScalarGridSpec(
            num_scalar_prefetch=3,
            grid=(N_PAIRS, len(steps)),
            in_specs=[qspec, kspec, kaspec, vtspec],
            out_specs=qspec,
            scratch_shapes=[pltpu.VMEM((2, 1, tq), F32), pltpu.VMEM((2, LANES + ONES_ROWS, tq), F32)],
        ),
        out_shape=jax.ShapeDtypeStruct((m, WIDTH), BF16),
        compiler_params=_params(("parallel", "arbitrary")),
    )(qi_tab, kb_tab, kind_tab, q, k, ka, vt)


def _mem_kv_kernel(mem_ref, g_ref, w_ref, o_ref):
    x = mem_ref[...]
    ms = jnp.mean(x * x, axis=-1, keepdims=True)
    xn = (x * lax.rsqrt(ms + NORM_EPS) * g_ref[...]).astype(BF16)
    o_ref[...] = _dot(xn, w_ref[...]).astype(o_ref.dtype)


def _mem_kv(mem, ln_mem, w_ckv):
    n, d = mem.shape
    ins = [mem, ln_mem.reshape(1, d), w_ckv]
    return pl.pallas_call(
        _mem_kv_kernel,
        grid=(1,),
        in_specs=[pl.BlockSpec(a.shape, lambda i: (0, 0)) for a in ins],
        out_specs=pl.BlockSpec((n, w_ckv.shape[1]), lambda i: (0, 0)),
        out_shape=jax.ShapeDtypeStruct((n, w_ckv.shape[1]), BF16),
        compiler_params=_params(("arbitrary",)),
    )(*ins)


def _post_kernel(ya_ref, yb_ref, pg_ref, x_ref, wa_ref, wb_ref, wo_ref, lnc_ref, wcq_ref, kv_ref, wco_ref, o_ref):
    d = x_ref.shape[1]
    xw = X_HEADS * X_HEAD_DIM
    ga = _sigmoid(pg_ref[:, 0:d])
    gb = _sigmoid(pg_ref[:, d:2 * d])
    merged = ga * _dot(ya_ref[...], wa_ref[...]) + gb * _dot(yb_ref[...], wb_ref[...])
    h = x_ref[...] + _dot(merged.astype(BF16), wo_ref[...])

    ms = jnp.mean(h * h, axis=-1, keepdims=True)
    hn = (h * lax.rsqrt(ms + NORM_EPS) * lnc_ref[...]).astype(BF16)
    q = _dot(hn, wcq_ref[...])
    outs = []
    for hd in range(X_HEADS):
        cols = slice(hd * X_HEAD_DIM, (hd + 1) * X_HEAD_DIM)
        vcols = slice(xw + hd * X_HEAD_DIM, xw + (hd + 1) * X_HEAD_DIM)
        s = _dot_t(q[:, cols].astype(BF16), kv_ref[:, cols]) * (X_HEAD_DIM ** -0.5)
        s = s - jnp.max(s, axis=1, keepdims=True)
        e = jnp.exp(s)
        pr = e / jnp.sum(e, axis=1, keepdims=True)
        outs.append(_dot(pr.astype(BF16), kv_ref[:, vcols]))
    o = jnp.concatenate(outs, axis=1).astype(BF16)
    o_ref[...] = h + _dot(o, wco_ref[...])


def _post(ya, yb, pg, x, wa, wb, wo, ln_cross, wcq, kv, wco, tm):
    m, d = x.shape
    tm = min(tm, m)
    lnc = ln_cross.reshape(1, d)
    rows = lambda a: pl.BlockSpec((tm, a.shape[1]), lambda i: (i, 0))
    const = lambda a: pl.BlockSpec(a.shape, lambda i: (0, 0), pipeline_mode=pl.Buffered(1))
    return pl.pallas_call(
        _post_kernel,
        grid=(m // tm,),
        in_specs=[rows(ya), rows(yb), rows(pg), rows(x)] + [const(a) for a in (wa, wb, wo, lnc, wcq, kv, wco)],
        out_specs=pl.BlockSpec((tm, d), lambda i: (i, 0)),
        out_shape=jax.ShapeDtypeStruct((m, d), F32),
        compiler_params=_params(("parallel",)),
    )(ya, yb, pg, x, wa, wb, wo, lnc, wcq, kv, wco)


def _mlp_kernel(h_ref, g_ref, wu_ref, wd_ref, gf_ref, o_ref, xn_ref, acc_ref):
    f = pl.program_id(1)

    @pl.when(f == 0)
    def _():
        h = h_ref[...]
        ms = jnp.mean(h * h, axis=-1, keepdims=True)
        xn_ref[...] = (h * lax.rsqrt(ms + NORM_EPS) * g_ref[...]).astype(xn_ref.dtype)
        acc_ref[...] = jnp.zeros_like(acc_ref)

    u = jnp.maximum(_dot(xn_ref[...], wu_ref[...]), 0.0)
    acc_ref[...] += _dot((u * u).astype(BF16), wd_ref[...])

    @pl.when(f == pl.num_programs(1) - 1)
    def _():
        h = h_ref[...] + acc_ref[...]
        ms = jnp.mean(h * h, axis=-1, keepdims=True)
        o_ref[...] = h * lax.rsqrt(ms + NORM_EPS) * gf_ref[...]


def _mlp(h, ln_mlp, w_up, w_down, ln_final, tm, tf):
    m, d = h.shape
    dff = w_up.shape[1]
    tm, tf = min(tm, m), min(tf, dff)
    return pl.pallas_call(
        _mlp_kernel,
        grid=(m // tm, dff // tf),
        in_specs=[pl.BlockSpec((tm, d), lambda i, f: (i, 0)),
                  pl.BlockSpec((1, d), lambda i, f: (0, 0)),
                  pl.BlockSpec((d, tf), lambda i, f: (0, f)),
                  pl.BlockSpec((tf, d), lambda i, f: (f, 0)),
                  pl.BlockSpec((1, d), lambda i, f: (0, 0))],
        out_specs=pl.BlockSpec((tm, d), lambda i, f: (i, 0)),
        out_shape=jax.ShapeDtypeStruct((m, d), F32),
        scratch_shapes=[pltpu.VMEM((tm, d), BF16), pltpu.VMEM((tm, d), F32)],
        compiler_params=_params(("parallel", "arbitrary")),
    )(h, ln_mlp.reshape(1, d), w_up, w_down, ln_final.reshape(1, d))


def _pad_cols(a, width):
    return jnp.pad(a, ((0, 0), (0, width - a.shape[1])))


def _pad_rows(a, height):
    return jnp.pad(a, ((0, height - a.shape[0]), (0, 0)))


def _layer(h, mem, ln_mix, w_in, rw_mu, rw_w0, rw_w_up, rw_a0, rw_a_up, rw_g_up, rw_k_k, rw_k_a, rw_r_k,
           rw_ln_w, rw_ln_b, fox_b_f, fox_q_norm, fox_k_norm, w_proj_a, w_proj_b, w_out, ln_cross, ln_mem,
           w_cq, w_ckv, w_co, ln_mlp, w_up, w_down, ln_final):
    w3 = 3 * WIDTH
    o_ad = w3 + DECAY_LORA
    o_gd = o_ad + ICLR_LORA
    rw_cols = o_gd + GATE_LORA
    fox_cols = w3 + N_HEADS

    def rw_layout(a):
        return jnp.concatenate([a[:, :w3], _pad_cols(a[:, w3:o_ad], DECAY_PAD), _pad_cols(a[:, o_ad:o_gd], ICLR_PAD),
                                _pad_cols(a[:, o_gd:rw_cols], GATE_PAD)], axis=1)

    w_rw = rw_layout(w_in[:, :rw_cols]).astype(BF16)
    rep = jnp.repeat(jnp.arange(N_HEADS), BIAS_PARTS)
    w_fz = w_in[:, rw_cols + w3:rw_cols + fox_cols][:, rep]
    w_fox = _pad_cols(jnp.concatenate([w_in[:, rw_cols:rw_cols + w3], w_fz], axis=1), FOX_PAD).astype(BF16)
    bf_rep = jnp.pad(fox_b_f[rep], (0, FZ_PAD - N_HEADS * BIAS_PARTS)).reshape(1, FZ_PAD)
    w_gate = w_in[:, rw_cols + fox_cols:].astype(BF16)
    mu_p = rw_layout(rw_mu.reshape(1, -1)).reshape(-1)

    xn = _rmsnorm_bf16(h, ln_mix, 512)
    p_rw = _matmul(xn, w_rw, 1024, 1792)
    p_fox = _matmul(xn, w_fox, 1024, 1664)
    p_gate = _matmul(xn, w_gate, 1024, 2048)

    r, k, v, lw, kk, ic, g = _rw_prep(p_rw, mu_p, rw_w0, _pad_rows(rw_w_up, DECAY_PAD), rw_a0,
                                      _pad_rows(rw_a_up, ICLR_PAD), _pad_rows(rw_g_up, GATE_PAD), rw_k_k, rw_k_a, 256)
    y_a = _rw_scan(r, lw, k, v, kk, ic, g, rw_r_k.reshape(-1), rw_ln_w, rw_ln_b, 2 * CHUNK)

    fq, fk, fvt, fka = _fox_prep(p_fox, fox_q_norm, fox_k_norm, bf_rep, 512)
    y_b = _fox_attn(fq, fk, fka, fvt, 512)

    kv = _mem_kv(mem, ln_mem, w_ckv.astype(BF16))
    h = _post(y_a, y_b, p_gate, h, w_proj_a.astype(BF16), w_proj_b.astype(BF16), w_out.astype(BF16), ln_cross,
              w_cq.astype(BF16), kv, w_co.astype(BF16), 256)
    return _mlp(h, ln_mlp, w_up.astype(BF16), w_down.astype(BF16), ln_final, 512, 1024)


def kernel(x, mem, ln_mix, w_in, rw_mu, rw_w0, rw_w_up, rw_a0, rw_a_up, rw_g_up, rw_k_k, rw_k_a, rw_r_k, rw_ln_w, rw_ln_b, fox_b_f, fox_q_norm, fox_k_norm, w_proj_a, w_proj_b, w_out, ln_cross, ln_mem, w_cq, w_ckv, w_co, ln_mlp, w_up, w_down, ln_final):
    b, s, d = x.shape
    assert b == 1 and ln_mix.shape[0] == 1, "single sequence, single layer"
    out = _layer(x[0], mem[0], ln_mix[0], w_in[0], rw_mu[0], rw_w0[0], rw_w_up[0], rw_a0[0], rw_a_up[0], rw_g_up[0],
                 rw_k_k[0], rw_k_a[0], rw_r_k[0], rw_ln_w[0], rw_ln_b[0], fox_b_f[0], fox_q_norm[0], fox_k_norm[0],
                 w_proj_a[0], w_proj_b[0], w_out[0], ln_cross[0], ln_mem[0], w_cq[0], w_ckv[0], w_co[0],
                 ln_mlp[0], w_up[0], w_down[0], ln_final)
    return out.reshape(b, s, d)
```

```python
import functools

import jax
import jax.numpy as jnp
from jax import lax
from jax.experimental import pallas as pl
from jax.experimental.pallas import tpu as pltpu

F32 = jnp.float32
BF16 = jnp.bfloat16

LANES = 128
HEAD_DIM = 64
N_HEADS = 16
WIDTH = N_HEADS * HEAD_DIM
N_PAIRS = WIDTH // LANES
DECAY_LORA, ICLR_LORA, GATE_LORA = 64, 64, 160
DECAY_PAD, ICLR_PAD, GATE_PAD = 128, 128, 256
RW_PAD = 3 * WIDTH + DECAY_PAD + ICLR_PAD + GATE_PAD
FZ_PAD = 512
BIAS_PARTS = 3
LOG2E = 1.4426950408889634
ONES_ROWS = 16
FOX_PAD = 3 * WIDTH + FZ_PAD
X_HEADS, X_HEAD_DIM = 4, 128
NORM_EPS = 1e-5
GN_EPS = 64e-5
CHUNK = 64
SUB = 16
VMEM_LIMIT = 56 * 1024 * 1024


def _params(sem, vmem=VMEM_LIMIT):
    return pltpu.CompilerParams(dimension_semantics=sem, vmem_limit_bytes=vmem)


def _dot(a, b, precision=None):
    return jnp.dot(a, b, preferred_element_type=F32, precision=precision)


def _dot_t(a, b, precision=None):
    return lax.dot_general(a, b, (((1,), (1,)), ((), ())), preferred_element_type=F32, precision=precision)


def _split(x, terms):
    parts, rem = [], x
    for i in range(terms):
        part = rem.astype(BF16)
        parts.append(part)
        if i + 1 < terms:
            rem = rem - part.astype(F32)
    return parts


_DIMS = {"nn": (((1,), (0,)), ((), ())), "nt": (((1,), (1,)), ((), ())), "tn": (((0,), (0,)), ((), ()))}


def _mm(a_parts, b_parts, kind="nn"):
    order = max(len(a_parts), len(b_parts))
    acc = None
    for i, a in enumerate(a_parts):
        for j, b in enumerate(b_parts):
            if i + j < order:
                term = lax.dot_general(a, b, _DIMS[kind], preferred_element_type=F32)
                acc = term if acc is None else acc + term
    return acc


def _softplus(x):
    return jnp.maximum(x, 0.0) + jnp.log(1.0 + jnp.exp(-jnp.abs(x)))


def _sigmoid(x):
    return 1.0 / (1.0 + jnp.exp(-x))


def _head_ones():
    r = lax.broadcasted_iota(jnp.int32, (LANES, LANES), 0) // HEAD_DIM
    c = lax.broadcasted_iota(jnp.int32, (LANES, LANES), 1) // HEAD_DIM
    return (r == c).astype(F32)


def _headsum(x, ones_f32):
    chunks = [x[:, c * LANES:(c + 1) * LANES] for c in range(x.shape[1] // LANES)]
    return jnp.concatenate(_stacked_headsum(chunks, [ones_f32.astype(BF16)], 2), axis=1)


def _rmsnorm_kernel(x_ref, g_ref, o_ref):
    x = x_ref[...]
    ms = jnp.mean(x * x, axis=-1, keepdims=True)
    o_ref[...] = (x * lax.rsqrt(ms + NORM_EPS) * g_ref[...]).astype(o_ref.dtype)


def _rmsnorm_bf16(x, g, tm):
    m, d = x.shape
    tm = min(tm, m)
    return pl.pallas_call(
        _rmsnorm_kernel,
        grid=(m // tm,),
        in_specs=[pl.BlockSpec((tm, d), lambda i: (i, 0)), pl.BlockSpec((1, d), lambda i: (0, 0))],
        out_specs=pl.BlockSpec((tm, d), lambda i: (i, 0)),
        out_shape=jax.ShapeDtypeStruct((m, d), BF16),
        compiler_params=_params(("parallel",)),
    )(x, g.reshape(1, d))


def _mm_kernel(a_ref, b_ref, o_ref, *, gate):
    acc = _dot(a_ref[...], b_ref[...])
    o_ref[...] = (_sigmoid(acc) if gate else acc).astype(o_ref.dtype)


def _matmul(a, b, col0, n, bm, bn, out_dtype=F32, gate=False):
    m, k = a.shape
    bm = min(bm, m)
    assert col0 % bn == 0 and n % bn == 0 and m % bm == 0
    off = col0 // bn
    return pl.pallas_call(
        functools.partial(_mm_kernel, gate=gate),
        grid=(n // bn, m // bm),
        in_specs=[pl.BlockSpec((bm, k), lambda j, i: (i, 0)), pl.BlockSpec((k, bn), lambda j, i: (0, j + off))],
        out_specs=pl.BlockSpec((bm, bn), lambda j, i: (i, j)),
        out_shape=jax.ShapeDtypeStruct((m, n), out_dtype),
        compiler_params=_params(("parallel", "parallel")),
    )(a, b)


def _bf(x):
    return [x.astype(BF16)]


def _stacked_headsum(xs, ones, terms):
    n = xs[0].shape[0]
    out = _mm(_split(jnp.concatenate(xs, axis=0), terms), ones)
    return [out[i * n:(i + 1) * n] for i in range(len(xs))]


def _rw_inputs(p, prev_row, mu, w0, wup, a0, aup, gup, k_k, k_a, ones):
    rows = lax.broadcasted_iota(jnp.int32, p.shape, 0)
    prev = jnp.where(rows == 0, prev_row, pltpu.roll(p, 1, 0))
    ps = p + (prev - p) * mu
    w = WIDTH
    r, k, v = ps[:, 0:w], ps[:, w:2 * w], ps[:, 2 * w:3 * w]
    o = 3 * w
    wd = ps[:, o:o + DECAY_PAD]
    ad = ps[:, o + DECAY_PAD:o + DECAY_PAD + ICLR_PAD]
    gd = ps[:, o + DECAY_PAD + ICLR_PAD:o + DECAY_PAD + ICLR_PAD + GATE_PAD]
    z = w0 + _mm(_bf(jnp.tanh(wd)), [wup])
    lw = -jnp.exp(-_softplus(-z) - 0.5)
    iclr = _sigmoid(a0 + _mm(_bf(ad), [aup]))
    gate = _mm(_bf(_sigmoid(gd)), [gup])
    kk = k * k_k
    kk2 = kk * kk
    ss = _stacked_headsum([kk2[:, c * LANES:(c + 1) * LANES] for c in range(N_PAIRS)], ones, 2)
    kk = kk * lax.rsqrt(jnp.maximum(jnp.concatenate(ss, axis=1), 1e-24))
    return r, k * (1.0 + (iclr - 1.0) * k_a), v, lw, -kk, kk * iclr, gate


def _pair_diag(x, head0):
    xb = x.astype(BF16)
    zero = jnp.zeros_like(xb)
    return [jnp.concatenate([jnp.where(head0, xb, zero), jnp.where(head0, zero, xb)], axis=0)]


def _unit_lower_inverse(a_list, sub_mask, eye, head0):
    c = CHUNK
    d = [jnp.where(sub_mask, a, 0.0) for a in a_list]
    e = [a - x for a, x in zip(a_list, d)]
    p = [eye + x for x in d]
    diag = lambda xs: [_pair_diag(x, head0) for x in xs]
    stack = lambda xs, ys: [_bf(jnp.concatenate([x, y], axis=0)) for x, y in zip(xs, ys)]
    dp = [_mm(_bf(x), y) for x, y in zip(d, diag(d))]
    for _ in range(SUB.bit_length() - 3):
        both = [_mm(x, y) for x, y in zip(stack(dp, p), diag(dp))]
        dp = [x[:c] for x in both]
        p = [x + y[c:] for x, y in zip(p, both)]
    p = [x + _mm(_bf(x), y) for x, y in zip(p, diag(dp))]
    f = [_mm(_bf(x), y) for x, y in zip(p, diag(e))]
    g = [eye + x for x in f]
    fp = [_mm(_bf(x), y) for x, y in zip(f, diag(f))]
    for _ in range((CHUNK // SUB).bit_length() - 3):
        both = [_mm(x, y) for x, y in zip(stack(fp, g), diag(fp))]
        fp = [x[:c] for x in both]
        g = [x + y[c:] for x, y in zip(g, both)]
    g = [x + _mm(_bf(x), y) for x, y in zip(g, diag(fp))]
    return [_mm(_bf(x), y) for x, y in zip(g, diag(p))]


def _rw_kernel(p_ref, mu_ref, w0_ref, wup_ref, a0_ref, aup_ref, gup_ref, kkw_ref, ka_ref, rk_ref, lnw_ref, lnb_ref,
               o_ref, h_ref, carry_ref):
    c = CHUNK
    t_tile = p_ref.shape[0]
    n_chunks = t_tile // c

    @pl.when(pl.program_id(0) == 0)
    def _():
        h_ref[...] = jnp.zeros_like(h_ref)
        carry_ref[...] = jnp.zeros_like(carry_ref)

    lane = lax.broadcasted_iota(jnp.int32, (c, LANES), 1)
    head0 = lane < HEAD_DIM
    ri = lax.broadcasted_iota(jnp.int32, (c, LANES), 0)
    ci = jnp.bitwise_and(lane, HEAD_DIM - 1)
    strict = ri > ci
    incl = ri >= ci
    eye = (ri == ci).astype(F32)
    sub_mask = (ri // SUB) == (ci // SUB)
    tril = _bf(lax.broadcasted_iota(jnp.int32, (c, c), 0) >= lax.broadcasted_iota(jnp.int32, (c, c), 1))
    ones_f = _head_ones()
    ones = _bf(ones_f)
    pair_mask = ones_f > 0.5
    diag = lambda x: _pair_diag(x, head0)
    units = [(s, p) for s in range(n_chunks) for p in range(N_PAIRS)]
    rows = lambda s: slice(s * c, (s + 1) * c)
    cols = lambda p: slice(p * LANES, (p + 1) * LANES)

    p_tile = p_ref[...]
    r_all, k_all, v_all, lw_all, a_all, b_all, gate_all = _rw_inputs(
        p_tile, carry_ref[...], mu_ref[...], w0_ref[...], wup_ref[...], a0_ref[...], aup_ref[...], gup_ref[...],
        kkw_ref[...], ka_ref[...], ones)
    carry_ref[...] = p_tile[t_tile - 1:t_tile, :]

    pre = []
    for s in range(n_chunks):
        lw, k, b = lw_all[rows(s), :], k_all[rows(s), :], b_all[rows(s), :]
        cl = _mm(tril, _split(lw, 3))
        cl_end = cl[c - 1:c, :]
        e_neg = jnp.exp(-cl)
        e_end = jnp.exp(cl_end - cl)
        pre.append(dict(rt=r_all[rows(s), :] * jnp.exp(cl), at=a_all[rows(s), :] * jnp.exp(cl - lw),
                        bt=b * e_neg, kt=k * e_neg, b_end=b * e_end, k_end=k * e_end, decay_end=jnp.exp(cl_end)))
    get = lambda name: [pre[s][name][:, cols(p)] for s, p in units]
    tile = lambda x: [x[rows(s), cols(p)] for s, p in units]
    r, k, v, gate = tile(r_all), tile(k_all), tile(v_all), tile(gate_all)
    rt, at = get("rt"), get("at")
    stack = lambda xs, ys: [jnp.concatenate([x, y], axis=0) for x, y in zip(xs, ys)]
    bk_end = [_bf(x) for x in stack(get("b_end"), get("k_end"))]
    v_diag = [diag(x) for x in v]
    mask2 = jnp.concatenate([strict, incl], axis=0)

    lhs = [_bf(x) for x in stack(at, rt)]
    sb = [_mm(x, diag(y), "nt") for x, y in zip(lhs, get("bt"))]
    sk = [_mm(x, diag(y), "nt") for x, y in zip(lhs, get("kt"))]
    skv = [_mm(_bf(jnp.where(mask2, x, 0.0)), y) for x, y in zip(sk, v_diag)]
    rb = [_bf(jnp.where(incl, x[c:], 0.0)) for x in sb]
    t_inv = [_bf(x) for x in _unit_lower_inverse([jnp.where(strict, x[:c], 0.0) for x in sb], sub_mask, eye, head0)]
    w = [_mm(x, diag(y)) for x, y in zip(t_inv, at)]
    tk = [_mm(x, diag(y[:c])) for x, y in zip(t_inv, skv)]
    s0_lhs = [_bf(x) for x in stack(w, rt)]

    ht = [h_ref[p] for p in range(N_PAIRS)]
    u, rs0 = [], []
    for s in range(n_chunks):
        i0 = s * N_PAIRS
        s0 = [_mm(s0_lhs[i0 + p], _bf(ht[p]), "nt") for p in range(N_PAIRS)]
        u += [s0[p][:c] + tk[i0 + p] for p in range(N_PAIRS)]
        rs0 += [s0[p][c:] for p in range(N_PAIRS)]
        upd = [_mm(_bf(jnp.concatenate([u[i0 + p], v[i0 + p]], axis=0)), bk_end[i0 + p], "tn") for p in range(N_PAIRS)]
        ht = [ht[p] * pre[s]["decay_end"][:, cols(p)] + jnp.where(pair_mask, upd[p], 0.0) for p in range(N_PAIRS)]
    for p in range(N_PAIRS):
        h_ref[p] = ht[p]

    y = [a + _mm(x, diag(z)) + b[c:] for a, x, z, b in zip(rs0, rb, u, skv)]
    rkr = [r[i] * k[i] * rk_ref[:, cols(p)] for i, (s, p) in enumerate(units)]
    sums = _stacked_headsum(y + rkr, ones, 2)
    dlt = [a - b * (1.0 / HEAD_DIM) for a, b in zip(y, sums[:len(units)])]
    var = _stacked_headsum([x * x for x in dlt], ones, 2)
    for i, (s, p) in enumerate(units):
        yn = dlt[i] * lax.rsqrt(var[i] * (1.0 / HEAD_DIM) + GN_EPS) * lnw_ref[:, cols(p)] + lnb_ref[:, cols(p)]
        o_ref[rows(s), cols(p)] = ((yn + sums[len(units) + i] * v[i]) * gate[i]).astype(o_ref.dtype)


def _rw_branch(p_rw, mu_p, w0, wup_p, a0, aup_p, gup_p, k_k, k_a, r_k, ln_w, ln_b, t_tile):
    m = p_rw.shape[0]
    t_tile = min(t_tile, m)
    row = lambda a: a.reshape(1, -1)
    full = lambda a: pl.BlockSpec(a.shape, lambda t: (0, 0))
    params = [row(mu_p), row(w0), wup_p, row(a0), aup_p, gup_p, row(k_k), row(k_a), row(r_k), row(ln_w), row(ln_b)]
    return pl.pallas_call(
        _rw_kernel,
        grid=(m // t_tile,),
        in_specs=[pl.BlockSpec((t_tile, RW_PAD), lambda t: (t, 0))] + [full(a) for a in params],
        out_specs=pl.BlockSpec((t_tile, WIDTH), lambda t: (t, 0)),
        out_shape=jax.ShapeDtypeStruct((m, WIDTH), BF16),
        scratch_shapes=[pltpu.VMEM((N_PAIRS, LANES, LANES), F32), pltpu.VMEM((1, RW_PAD), F32)],
        compiler_params=_params(("arbitrary",)),
    )(p_rw, *params)


def _fox_prep_kernel(p_ref, qn_ref, kn_ref, bf_ref, part_ref, q_out, k_out, vt_out, ka_out, carry_ref):
    tm = p_ref.shape[0]

    @pl.when(pl.program_id(0) == 0)
    def _():
        carry_ref[...] = jnp.zeros_like(carry_ref)

    w = WIDTH
    ones = _head_ones()
    q, k = p_ref[:, 0:w], p_ref[:, w:2 * w]
    q = q * lax.rsqrt(_headsum(q * q, ones) * (1.0 / HEAD_DIM) + NORM_EPS) * qn_ref[...]
    k = k * lax.rsqrt(_headsum(k * k, ones) * (1.0 / HEAD_DIM) + NORM_EPS) * kn_ref[...]
    q_out[...] = (q * (HEAD_DIM ** -0.5 * LOG2E)).astype(q_out.dtype)
    k_out[...] = k.astype(k_out.dtype)
    vt_out[...] = p_ref[:, 2 * w:3 * w].T.astype(vt_out.dtype)

    logf = -_softplus(-(p_ref[:, 3 * w:3 * w + LANES] + bf_ref[...]))
    ri = lax.broadcasted_iota(jnp.int32, (tm, tm), 0)
    ci = lax.broadcasted_iota(jnp.int32, (tm, tm), 1)
    cum = _mm(_bf(ri >= ci), _split(logf, 3)) + carry_ref[...]
    carry_ref[...] = cum[tm - 1:tm, :]
    hi, mid, lo = _split(cum * (-LOG2E), 3)
    part = part_ref[...]
    ka_out[...] = jnp.where(part == 0, hi, jnp.where(part == 1, mid, lo))


def _fox_prep(p_fox, q_norm, k_norm, bf_rep, tm):
    m = p_fox.shape[0]
    tm = min(tm, m)
    qn = jnp.tile(q_norm, N_HEADS).reshape(1, WIDTH)
    kn = jnp.tile(k_norm, N_HEADS).reshape(1, WIDTH)
    part = (jnp.arange(LANES, dtype=jnp.int32) % BIAS_PARTS).reshape(1, LANES)
    full = lambda a: pl.BlockSpec(a.shape, lambda i: (0, 0))
    act = jax.ShapeDtypeStruct((m, WIDTH), BF16)
    return pl.pallas_call(
        _fox_prep_kernel,
        grid=(m // tm,),
        in_specs=[pl.BlockSpec((tm, FOX_PAD), lambda i: (i, 0)), full(qn), full(kn), full(bf_rep), full(part)],
        out_specs=[pl.BlockSpec((tm, WIDTH), lambda i: (i, 0))] * 2
        + [pl.BlockSpec((WIDTH, tm), lambda i: (0, i)), pl.BlockSpec((tm, LANES), lambda i: (i, 0))],
        out_shape=[act, act, jax.ShapeDtypeStruct((WIDTH, m), BF16), jax.ShapeDtypeStruct((m, LANES), BF16)],
        scratch_shapes=[pltpu.VMEM((1, LANES), F32)],
        compiler_params=_params(("arbitrary",)),
    )(p_fox, qn, kn, bf_rep, part)


def _fox_attn_kernel(qi_ref, kb_ref, kind_ref, q_ref, k_ref, ka_ref, vt_ref, o_ref, m_ref, acc_ref):
    tq = q_ref.shape[0]
    pair, step_id = pl.program_id(0), pl.program_id(1)
    kb, kind = kb_ref[step_id], kind_ref[step_id]

    @pl.when(kb == 0)
    def _():
        m_ref[...] = jnp.full_like(m_ref, -jnp.inf)
        acc_ref[...] = jnp.zeros_like(acc_ref)

    def step(kinds):
        q = q_ref[...]
        lane_q = lax.broadcasted_iota(jnp.int32, (tq, LANES), 1)
        q_full = []
        for h in range(2):
            hmask = (lane_q < HEAD_DIM) if h == 0 else (lane_q >= HEAD_DIM)
            first = (2 * pair + h) * BIAS_PARTS
            aug = jnp.logical_and(lane_q >= first, lane_q < first + BIAS_PARTS).astype(BF16)
            q_full.append(jnp.concatenate([jnp.where(hmask, q, jnp.zeros_like(q)), aug], axis=1))
        subs = [slice(b * tq, (b + 1) * tq) for b in range(len(kinds))]
        k_full = [jnp.concatenate([k_ref[sl, :], ka_ref[sl, :]], axis=1) for sl in subs]
        ones_rows = jnp.ones((ONES_ROWS, tq), BF16)
        v_aug = [jnp.concatenate([vt_ref[:, sl], ones_rows], axis=0) for sl in subs]
        s = [[_mm([k_full[b]], [q_full[h]], "nt") for h in range(2)] for b in range(len(kinds))]
        causal = lax.broadcasted_iota(jnp.int32, (tq, tq), 0) <= lax.broadcasted_iota(jnp.int32, (tq, tq), 1)
        m = [m_ref[h] for h in range(2)]
        acc = [acc_ref[h] for h in range(2)]
        for b, diagonal in enumerate(kinds):
            sb = [jnp.where(causal, x, -jnp.inf) for x in s[b]] if diagonal else s[b]
            m_new = [jnp.maximum(m[h], jnp.max(sb[h], axis=0, keepdims=True)) for h in range(2)]
            p = [jnp.exp2(sb[h] - m_new[h]).astype(BF16) for h in range(2)]
            pv = [_mm([v_aug[b]], [p[h]]) for h in range(2)]
            acc = [jnp.exp2(m[h] - m_new[h]) * acc[h] + pv[h] for h in range(2)]
            m = m_new
        for h in range(2):
            acc_ref[h] = acc[h]
            m_ref[h] = m[h]

    def finish():
        row = lax.broadcasted_iota(jnp.int32, (LANES, tq), 0)
        o = [acc_ref[h, 0:LANES, :] / acc_ref[h, LANES:LANES + 1, :] for h in range(2)]
        o_ref[...] = jnp.where(row < HEAD_DIM, o[0], o[1]).T.astype(o_ref.dtype)

    @pl.when(kind == 0)
    def _():
        step((False, False))

    @pl.when(kind == 1)
    def _():
        step((False, True))
        finish()

    @pl.when(kind == 2)
    def _():
        step((True,))
        finish()


def _fox_attn(q, k, ka, vt, tq):
    m = q.shape[0]
    tq = min(tq, m)
    nq = m // tq
    assert nq % 2 == 0, "key blocks are fetched in pairs"
    steps = []
    for i in range(nq):
        steps += [(i, j, 0) for j in range(i // 2)] + [(i, i // 2, 1 if i % 2 else 2)]
    qi_tab, kb_tab, kind_tab = (jnp.asarray([st[c] for st in steps], jnp.int32) for c in range(3))
    qspec = pl.BlockSpec((tq, LANES), lambda p, t, qi, kb, kind: (qi[t], p))
    kspec = pl.BlockSpec((2 * tq, LANES), lambda p, t, qi, kb, kind: (kb[t], p))
    kaspec = pl.BlockSpec((2 * tq, LANES), lambda p, t, qi, kb, kind: (kb[t], 0))
    vtspec = pl.BlockSpec((LANES, 2 * tq), lambda p, t, qi, kb, kind: (p, kb[t]))
    return pl.pallas_call(
        _fox_attn_kernel,
        grid_spec=pltpu.PrefetchScalarGridSpec(
            num_scalar_prefetch=3,
            grid=(N_PAIRS, len(steps)),
            in_specs=[qspec, kspec, kaspec, vtspec],
            out_specs=qspec,
            scratch_shapes=[pltpu.VMEM((2, 1, tq), F32), pltpu.VMEM((2, LANES + ONES_ROWS, tq), F32)],
        ),
        out_shape=jax.ShapeDtypeStruct((m, WIDTH), BF16),
        compiler_params=_params(("parallel", "arbitrary")),
    )(qi_tab, kb_tab, kind_tab, q, k, ka, vt)


def _mem_kv_kernel(mem_ref, g_ref, w_ref, o_ref):
    x = mem_ref[...]
    ms = jnp.mean(x * x, axis=-1, keepdims=True)
    xn = (x * lax.rsqrt(ms + NORM_EPS) * g_ref[...]).astype(BF16)
    o_ref[...] = _dot(xn, w_ref[...]).astype(o_ref.dtype)


def _mem_kv(mem, ln_mem, w_ckv):
    n, d = mem.shape
    ins = [mem, ln_mem.reshape(1, d), w_ckv]
    return pl.pallas_call(
        _mem_kv_kernel,
        grid=(1,),
        in_specs=[pl.BlockSpec(a.shape, lambda i: (0, 0)) for a in ins],
        out_specs=pl.BlockSpec((n, w_ckv.shape[1]), lambda i: (0, 0)),
        out_shape=jax.ShapeDtypeStruct((n, w_ckv.shape[1]), BF16),
        compiler_params=_params(("arbitrary",)),
    )(*ins)


def _post_kernel(ya_ref, yb_ref, pg_ref, x_ref, wa_ref, wb_ref, wo_ref, lnc_ref, wcq_ref, kv_ref, wco_ref, o_ref):
    d = x_ref.shape[1]
    xw = X_HEADS * X_HEAD_DIM
    ga = pg_ref[:, 0:d].astype(F32)
    gb = pg_ref[:, d:2 * d].astype(F32)
    merged = ga * _dot(ya_ref[...], wa_ref[...]) + gb * _dot(yb_ref[...], wb_ref[...])
    h = x_ref[...] + _dot(merged.astype(BF16), wo_ref[...])

    ms = jnp.mean(h * h, axis=-1, keepdims=True)
    hn = (h * lax.rsqrt(ms + NORM_EPS) * lnc_ref[...]).astype(BF16)
    q = _dot(hn, wcq_ref[...])
    outs = []
    for hd in range(X_HEADS):
        cols = slice(hd * X_HEAD_DIM, (hd + 1) * X_HEAD_DIM)
        vcols = slice(xw + hd * X_HEAD_DIM, xw + (hd + 1) * X_HEAD_DIM)
        s = _dot_t(q[:, cols].astype(BF16), kv_ref[:, cols]) * (X_HEAD_DIM ** -0.5)
        s = s - jnp.max(s, axis=1, keepdims=True)
        e = jnp.exp(s)
        pr = e / jnp.sum(e, axis=1, keepdims=True)
        outs.append(_dot(pr.astype(BF16), kv_ref[:, vcols]))
    o = jnp.concatenate(outs, axis=1).astype(BF16)
    o_ref[...] = h + _dot(o, wco_ref[...])


def _post(ya, yb, pg, x, wa, wb, wo, ln_cross, wcq, kv, wco, tm):
    m, d = x.shape
    tm = min(tm, m)
    lnc = ln_cross.reshape(1, d)
    rows = lambda a: pl.BlockSpec((tm, a.shape[1]), lambda i: (i, 0))
    const = lambda a: pl.BlockSpec(a.shape, lambda i: (0, 0), pipeline_mode=pl.Buffered(1))
    return pl.pallas_call(
        _post_kernel,
        grid=(m // tm,),
        in_specs=[rows(ya), rows(yb), rows(pg), rows(x)] + [const(a) for a in (wa, wb, wo, lnc, wcq, kv, wco)],
        out_specs=pl.BlockSpec((tm, d), lambda i: (i, 0)),
        out_shape=jax.ShapeDtypeStruct((m, d), F32),
        compiler_params=_params(("parallel",)),
    )(ya, yb, pg, x, wa, wb, wo, lnc, wcq, kv, wco)


def _mlp_kernel(h_ref, g_ref, wu_ref, wd_ref, gf_ref, o_ref, xn_ref):
    f = pl.program_id(1)

    @pl.when(f == 0)
    def _():
        h = h_ref[...]
        ms = jnp.mean(h * h, axis=-1, keepdims=True)
        xn_ref[...] = (h * lax.rsqrt(ms + NORM_EPS) * g_ref[...]).astype(xn_ref.dtype)
        o_ref[...] = jnp.zeros_like(o_ref)

    u = jnp.maximum(_dot(xn_ref[...], wu_ref[...]), 0.0)
    o_ref[...] += _dot((u * u).astype(BF16), wd_ref[...])

    @pl.when(f == pl.num_programs(1) - 1)
    def _():
        h = h_ref[...] + o_ref[...]
        ms = jnp.mean(h * h, axis=-1, keepdims=True)
        o_ref[...] = h * lax.rsqrt(ms + NORM_EPS) * gf_ref[...]


def _mlp(h, ln_mlp, w_up, w_down, ln_final, tm, tf):
    m, d = h.shape
    dff = w_up.shape[1]
    tm, tf = min(tm, m), min(tf, dff)
    return pl.pallas_call(
        _mlp_kernel,
        grid=(m // tm, dff // tf),
        in_specs=[pl.BlockSpec((tm, d), lambda i, f: (i, 0)),
                  pl.BlockSpec((1, d), lambda i, f: (0, 0)),
                  pl.BlockSpec((d, tf), lambda i, f: (0, f)),
                  pl.BlockSpec((tf, d), lambda i, f: (f, 0)),
                  pl.BlockSpec((1, d), lambda i, f: (0, 0))],
        out_specs=pl.BlockSpec((tm, d), lambda i, f: (i, 0)),
        out_shape=jax.ShapeDtypeStruct((m, d), F32),
        scratch_shapes=[pltpu.VMEM((tm, d), BF16)],
        compiler_params=_params(("parallel", "arbitrary")),
    )(h, ln_mlp.reshape(1, d), w_up, w_down, ln_final.reshape(1, d))


def _pad_cols(a, width):
    return jnp.pad(a, ((0, 0), (0, width - a.shape[1])))


def _pad_rows(a, height):
    return jnp.pad(a, ((0, height - a.shape[0]), (0, 0)))


def _layer(h, mem, ln_mix, w_in, rw_mu, rw_w0, rw_w_up, rw_a0, rw_a_up, rw_g_up, rw_k_k, rw_k_a, rw_r_k,
           rw_ln_w, rw_ln_b, fox_b_f, fox_q_norm, fox_k_norm, w_proj_a, w_proj_b, w_out, ln_cross, ln_mem,
           w_cq, w_ckv, w_co, ln_mlp, w_up, w_down, ln_final):
    w3 = 3 * WIDTH
    o_ad = w3 + DECAY_LORA
    o_gd = o_ad + ICLR_LORA
    rw_cols = o_gd + GATE_LORA
    fox_cols = w3 + N_HEADS

    def rw_layout(a):
        return [a[:, :w3], _pad_cols(a[:, w3:o_ad], DECAY_PAD), _pad_cols(a[:, o_ad:o_gd], ICLR_PAD),
                _pad_cols(a[:, o_gd:rw_cols], GATE_PAD)]

    w_fz = jnp.repeat(w_in[:, rw_cols + w3:rw_cols + fox_cols], BIAS_PARTS, axis=1)
    w_all = jnp.concatenate(rw_layout(w_in) + [w_in[:, rw_cols:rw_cols + w3], _pad_cols(w_fz, FZ_PAD),
                                               w_in[:, rw_cols + fox_cols:]], axis=1).astype(BF16)
    bf_rep = jnp.pad(jnp.repeat(fox_b_f, BIAS_PARTS), (0, LANES - N_HEADS * BIAS_PARTS)).reshape(1, LANES)
    mu_p = jnp.concatenate(rw_layout(rw_mu.reshape(1, -1)), axis=1).reshape(-1)

    xn = _rmsnorm_bf16(h, ln_mix, 512)
    p_rw = _matmul(xn, w_all, 0, RW_PAD, 1024, 1792)
    p_fox = _matmul(xn, w_all, RW_PAD, FOX_PAD, 1024, 1792)
    gates = _matmul(xn, w_all, RW_PAD + FOX_PAD, 2 * h.shape[1], 1024, 1024, BF16, gate=True)

    lora = lambda a, rows: _pad_rows(a, rows).astype(BF16)
    y_a = _rw_branch(p_rw, mu_p, rw_w0, lora(rw_w_up, DECAY_PAD), rw_a0, lora(rw_a_up, ICLR_PAD),
                     lora(rw_g_up, GATE_PAD), rw_k_k, rw_k_a, rw_r_k.reshape(-1), rw_ln_w, rw_ln_b, 2 * CHUNK)

    fq, fk, fvt, fka = _fox_prep(p_fox, fox_q_norm, fox_k_norm, bf_rep, 512)
    y_b = _fox_attn(fq, fk, fka, fvt, 512)

    kv = _mem_kv(mem, ln_mem, w_ckv.astype(BF16))
    h = _post(y_a, y_b, gates, h, w_proj_a.astype(BF16), w_proj_b.astype(BF16), w_out.astype(BF16), ln_cross,
              w_cq.astype(BF16), kv, w_co.astype(BF16), 256)
    return _mlp(h, ln_mlp, w_up.astype(BF16), w_down.astype(BF16), ln_final, 1024, 512)


def kernel(x, mem, ln_mix, w_in, rw_mu, rw_w0, rw_w_up, rw_a0, rw_a_up, rw_g_up, rw_k_k, rw_k_a, rw_r_k, rw_ln_w, rw_ln_b, fox_b_f, fox_q_norm, fox_k_norm, w_proj_a, w_proj_b, w_out, ln_cross, ln_mem, w_cq, w_ckv, w_co, ln_mlp, w_up, w_down, ln_final):
    b, s, d = x.shape
    assert b == 1 and ln_mix.shape[0] == 1, "single sequence, single layer"
    out = _layer(x[0], mem[0], ln_mix[0], w_in[0], rw_mu[0], rw_w0[0], rw_w_up[0], rw_a0[0], rw_a_up[0], rw_g_up[0],
                 rw_k_k[0], rw_k_a[0], rw_r_k[0], rw_ln_w[0], rw_ln_b[0], fox_b_f[0], fox_q_norm[0], fox_k_norm[0],
                 w_proj_a[0], w_proj_b[0], w_out[0], ln_cross[0], ln_mem[0], w_cq[0], w_ckv[0], w_co[0],
                 ln_mlp[0], w_up[0], w_down[0], ln_final)
    return out.reshape(b, s, d)
```

```python
import functools

import numpy as np

import jax
import jax.numpy as jnp
from jax import lax
from jax.experimental import pallas as pl
from jax.experimental.pallas import tpu as pltpu

F32 = jnp.float32
BF16 = jnp.bfloat16

LANES = 128
HEAD_DIM = 64
N_HEADS = 16
WIDTH = N_HEADS * HEAD_DIM
N_PAIRS = WIDTH // LANES
DECAY_LORA, ICLR_LORA, GATE_LORA = 64, 64, 160
DECAY_PAD, ICLR_PAD, GATE_PAD = 128, 128, 256
RW_PAD = 3 * WIDTH + DECAY_PAD + ICLR_PAD + GATE_PAD
FZ_PAD = 512
BOUND_SLACK = 1.02
SAFE_LOGIT_BOUND = 50.0
BIAS_PARTS = 3
BIAS_LANES = BIAS_PARTS * N_HEADS
LOG2E = 1.4426950408889634
KEY_BLOCKS = 4
AHEAD = 2
ONES_ROWS = 16
FOX_PAD = 3 * WIDTH + FZ_PAD
X_HEADS, X_HEAD_DIM = 4, 128
NORM_EPS = 1e-5
GN_EPS = 64e-5
CHUNK = 64
SUB = 16
VMEM_LIMIT = 56 * 1024 * 1024


def _params(sem, vmem=VMEM_LIMIT):
    return pltpu.CompilerParams(dimension_semantics=sem, vmem_limit_bytes=vmem)


def _dot(a, b, precision=None):
    return jnp.dot(a, b, preferred_element_type=F32, precision=precision)


def _dot_t(a, b, precision=None):
    return lax.dot_general(a, b, (((1,), (1,)), ((), ())), preferred_element_type=F32, precision=precision)


def _split(x, terms):
    parts, rem = [], x
    for i in range(terms):
        part = rem.astype(BF16)
        parts.append(part)
        if i + 1 < terms:
            rem = rem - part.astype(F32)
    return parts


_DIMS = {"nn": (((1,), (0,)), ((), ())), "nt": (((1,), (1,)), ((), ())), "tn": (((0,), (0,)), ((), ()))}


def _mm(a_parts, b_parts, kind="nn"):
    order = max(len(a_parts), len(b_parts))
    acc = None
    for i, a in enumerate(a_parts):
        for j, b in enumerate(b_parts):
            if i + j < order:
                term = lax.dot_general(a, b, _DIMS[kind], preferred_element_type=F32)
                acc = term if acc is None else acc + term
    return acc


def _softplus(x):
    return jnp.maximum(x, 0.0) + jnp.log(1.0 + jnp.exp(-jnp.abs(x)))


def _sigmoid(x):
    return 1.0 / (1.0 + jnp.exp(-x))


def _head_ones():
    r = lax.broadcasted_iota(jnp.int32, (LANES, LANES), 0) // HEAD_DIM
    c = lax.broadcasted_iota(jnp.int32, (LANES, LANES), 1) // HEAD_DIM
    return (r == c).astype(F32)


def _headsum(x, ones_f32, terms=2):
    chunks = [x[:, c * LANES:(c + 1) * LANES] for c in range(x.shape[1] // LANES)]
    return jnp.concatenate(_stacked_headsum(chunks, [ones_f32.astype(BF16)], terms), axis=1)


def _rmsnorm_kernel(x_ref, g_ref, o_ref):
    x = x_ref[...]
    ms = jnp.mean(x * x, axis=-1, keepdims=True)
    o_ref[...] = (x * lax.rsqrt(ms + NORM_EPS) * g_ref[...]).astype(o_ref.dtype)


def _rmsnorm_bf16(x, g, tm):
    m, d = x.shape
    tm = min(tm, m)
    return pl.pallas_call(
        _rmsnorm_kernel,
        grid=(m // tm,),
        in_specs=[pl.BlockSpec((tm, d), lambda i: (i, 0)), pl.BlockSpec((1, d), lambda i: (0, 0))],
        out_specs=pl.BlockSpec((tm, d), lambda i: (i, 0)),
        out_shape=jax.ShapeDtypeStruct((m, d), BF16),
        compiler_params=_params(("parallel",)),
    )(x, g.reshape(1, d))


def _mm_kernel(a_ref, b_ref, o_ref, *, gate):
    acc = _dot(a_ref[...], b_ref[...])
    o_ref[...] = (_sigmoid(acc) if gate else acc).astype(o_ref.dtype)


def _matmul(a, b, col0, n, bm, bn, out_dtype=F32, gate=False):
    m, k = a.shape
    bm = min(bm, m)
    assert col0 % bn == 0 and n % bn == 0 and m % bm == 0
    off = col0 // bn
    return pl.pallas_call(
        functools.partial(_mm_kernel, gate=gate),
        grid=(n // bn, m // bm),
        in_specs=[pl.BlockSpec((bm, k), lambda j, i: (i, 0)), pl.BlockSpec((k, bn), lambda j, i: (0, j + off))],
        out_specs=pl.BlockSpec((bm, bn), lambda j, i: (i, j)),
        out_shape=jax.ShapeDtypeStruct((m, n), out_dtype),
        compiler_params=_params(("parallel", "parallel")),
    )(a, b)


def _bf(x):
    return [x.astype(BF16)]


def _stacked_headsum(xs, ones, terms):
    n = xs[0].shape[0]
    out = _mm(_split(jnp.concatenate(xs, axis=0), terms), ones)
    return [out[i * n:(i + 1) * n] for i in range(len(xs))]


def _rw_inputs(p, prev_row, mu, w0, wup, a0, aup, gup, k_k, k_a, ones):
    rows = lax.broadcasted_iota(jnp.int32, p.shape, 0)
    prev = jnp.where(rows == 0, prev_row, pltpu.roll(p, 1, 0))
    ps = p + (prev - p) * mu
    w = WIDTH
    r, k, v = ps[:, 0:w], ps[:, w:2 * w], ps[:, 2 * w:3 * w]
    o = 3 * w
    wd = ps[:, o:o + DECAY_PAD]
    ad = ps[:, o + DECAY_PAD:o + DECAY_PAD + ICLR_PAD]
    gd = ps[:, o + DECAY_PAD + ICLR_PAD:o + DECAY_PAD + ICLR_PAD + GATE_PAD]
    z = w0 + _mm(_bf(jnp.tanh(wd)), [wup])
    lw = -jnp.exp(-_softplus(-z) - 0.5)
    iclr = _sigmoid(a0 + _mm(_bf(ad), [aup]))
    gate = _mm(_bf(_sigmoid(gd)), [gup])
    kk = k * k_k
    kk2 = kk * kk
    ss = _stacked_headsum([kk2[:, c * LANES:(c + 1) * LANES] for c in range(N_PAIRS)], ones, 2)
    kk = kk * lax.rsqrt(jnp.maximum(jnp.concatenate(ss, axis=1), 1e-24))
    return r, k * (1.0 + (iclr - 1.0) * k_a), v, lw, -kk, kk * iclr, gate


def _pair_diag(x, head0):
    xb = x.astype(BF16)
    zero = jnp.zeros_like(xb)
    return [jnp.concatenate([jnp.where(head0, xb, zero), jnp.where(head0, zero, xb)], axis=0)]


def _unit_lower_inverse(a_list, sub_mask, eye, head0):
    c = CHUNK
    d = [jnp.where(sub_mask, a, 0.0) for a in a_list]
    e = [a - x for a, x in zip(a_list, d)]
    p = [eye + x for x in d]
    diag = lambda xs: [_pair_diag(x, head0) for x in xs]
    stack = lambda xs, ys: [_bf(jnp.concatenate([x, y], axis=0)) for x, y in zip(xs, ys)]
    dp = [_mm(_bf(x), y) for x, y in zip(d, diag(d))]
    for _ in range(SUB.bit_length() - 3):
        both = [_mm(x, y) for x, y in zip(stack(dp, p), diag(dp))]
        dp = [x[:c] for x in both]
        p = [x + y[c:] for x, y in zip(p, both)]
    p = [x + _mm(_bf(x), y) for x, y in zip(p, diag(dp))]
    f = [_mm(_bf(x), y) for x, y in zip(p, diag(e))]
    g = [eye + x for x in f]
    fp = [_mm(_bf(x), y) for x, y in zip(f, diag(f))]
    for _ in range((CHUNK // SUB).bit_length() - 3):
        both = [_mm(x, y) for x, y in zip(stack(fp, g), diag(fp))]
        fp = [x[:c] for x in both]
        g = [x + y[c:] for x, y in zip(g, both)]
    g = [x + _mm(_bf(x), y) for x, y in zip(g, diag(fp))]
    return [_mm(_bf(x), y) for x, y in zip(g, diag(p))]


def _rw_kernel(p_ref, mu_ref, w0_ref, wup_ref, a0_ref, aup_ref, gup_ref, kkw_ref, ka_ref, rk_ref, lnw_ref, lnb_ref,
               o_ref, h_ref, carry_ref):
    c = CHUNK
    t_tile = p_ref.shape[0]
    n_chunks = t_tile // c

    @pl.when(pl.program_id(0) == 0)
    def _():
        h_ref[...] = jnp.zeros_like(h_ref)
        carry_ref[...] = jnp.zeros_like(carry_ref)

    lane = lax.broadcasted_iota(jnp.int32, (c, LANES), 1)
    head0 = lane < HEAD_DIM
    ri = lax.broadcasted_iota(jnp.int32, (c, LANES), 0)
    ci = jnp.bitwise_and(lane, HEAD_DIM - 1)
    strict = ri > ci
    incl = ri >= ci
    eye = (ri == ci).astype(F32)
    sub_mask = (ri // SUB) == (ci // SUB)
    tril = _bf(lax.broadcasted_iota(jnp.int32, (c, c), 0) >= lax.broadcasted_iota(jnp.int32, (c, c), 1))
    ones_f = _head_ones()
    ones = _bf(ones_f)
    pair_mask = ones_f > 0.5
    diag = lambda x: _pair_diag(x, head0)
    units = [(s, p) for s in range(n_chunks) for p in range(N_PAIRS)]
    rows = lambda s: slice(s * c, (s + 1) * c)
    cols = lambda p: slice(p * LANES, (p + 1) * LANES)

    p_tile = p_ref[...]
    r_all, k_all, v_all, lw_all, a_all, b_all, gate_all = _rw_inputs(
        p_tile, carry_ref[...], mu_ref[...], w0_ref[...], wup_ref[...], a0_ref[...], aup_ref[...], gup_ref[...],
        kkw_ref[...], ka_ref[...], ones)
    carry_ref[...] = p_tile[t_tile - 1:t_tile, :]

    pre = []
    for s in range(n_chunks):
        lw, k, b = lw_all[rows(s), :], k_all[rows(s), :], b_all[rows(s), :]
        cl = _mm(tril, _split(lw, 3))
        cl_end = cl[c - 1:c, :]
        e_neg = jnp.exp(-cl)
        e_end = jnp.exp(cl_end - cl)
        pre.append(dict(rt=r_all[rows(s), :] * jnp.exp(cl), at=a_all[rows(s), :] * jnp.exp(cl - lw),
                        bt=b * e_neg, kt=k * e_neg, b_end=b * e_end, k_end=k * e_end, decay_end=jnp.exp(cl_end)))
    get = lambda name: [pre[s][name][:, cols(p)] for s, p in units]
    tile = lambda x: [x[rows(s), cols(p)] for s, p in units]
    r, k, v, gate = tile(r_all), tile(k_all), tile(v_all), tile(gate_all)
    rt, at = get("rt"), get("at")
    stack = lambda xs, ys: [jnp.concatenate([x, y], axis=0) for x, y in zip(xs, ys)]
    bk_end = [_bf(x) for x in stack(get("b_end"), get("k_end"))]
    v_diag = [diag(x) for x in v]
    mask2 = jnp.concatenate([strict, incl], axis=0)

    lhs = [_bf(x) for x in stack(at, rt)]
    sb = [_mm(x, diag(y), "nt") for x, y in zip(lhs, get("bt"))]
    sk = [_mm(x, diag(y), "nt") for x, y in zip(lhs, get("kt"))]
    skv = [_mm(_bf(jnp.where(mask2, x, 0.0)), y) for x, y in zip(sk, v_diag)]
    rb = [_bf(jnp.where(incl, x[c:], 0.0)) for x in sb]
    t_inv = [_bf(x) for x in _unit_lower_inverse([jnp.where(strict, x[:c], 0.0) for x in sb], sub_mask, eye, head0)]
    w = [_mm(x, diag(y)) for x, y in zip(t_inv, at)]
    tk = [_mm(x, diag(y[:c])) for x, y in zip(t_inv, skv)]
    s0_lhs = [_bf(x) for x in stack(w, rt)]

    ht = [h_ref[p] for p in range(N_PAIRS)]
    u, rs0 = [], []
    for s in range(n_chunks):
        i0 = s * N_PAIRS
        s0 = [_mm(s0_lhs[i0 + p], _bf(ht[p]), "nt") for p in range(N_PAIRS)]
        u += [s0[p][:c] + tk[i0 + p] for p in range(N_PAIRS)]
        rs0 += [s0[p][c:] for p in range(N_PAIRS)]
        upd = [_mm(_bf(jnp.concatenate([u[i0 + p], v[i0 + p]], axis=0)), bk_end[i0 + p], "tn") for p in range(N_PAIRS)]
        ht = [ht[p] * pre[s]["decay_end"][:, cols(p)] + jnp.where(pair_mask, upd[p], 0.0) for p in range(N_PAIRS)]
    for p in range(N_PAIRS):
        h_ref[p] = ht[p]

    y = [a + _mm(x, diag(z)) + b[c:] for a, x, z, b in zip(rs0, rb, u, skv)]
    rkr = [r[i] * k[i] * rk_ref[:, cols(p)] for i, (s, p) in enumerate(units)]
    sums = _stacked_headsum(y + rkr, ones, 2)
    dlt = [a - b * (1.0 / HEAD_DIM) for a, b in zip(y, sums[:len(units)])]
    var = _stacked_headsum([x * x for x in dlt], ones, 2)
    for i, (s, p) in enumerate(units):
        yn = dlt[i] * lax.rsqrt(var[i] * (1.0 / HEAD_DIM) + GN_EPS) * lnw_ref[:, cols(p)] + lnb_ref[:, cols(p)]
        o_ref[rows(s), cols(p)] = ((yn + sums[len(units) + i] * v[i]) * gate[i]).astype(o_ref.dtype)


def _rw_branch(p_rw, mu_p, w0, wup_p, a0, aup_p, gup_p, k_k, k_a, r_k, ln_w, ln_b, t_tile):
    m = p_rw.shape[0]
    t_tile = min(t_tile, m)
    row = lambda a: a.reshape(1, -1)
    full = lambda a: pl.BlockSpec(a.shape, lambda t: (0, 0))
    params = [row(mu_p), row(w0), wup_p, row(a0), aup_p, gup_p, row(k_k), row(k_a), row(r_k), row(ln_w), row(ln_b)]
    return pl.pallas_call(
        _rw_kernel,
        grid=(m // t_tile,),
        in_specs=[pl.BlockSpec((t_tile, RW_PAD), lambda t: (t, 0))] + [full(a) for a in params],
        out_specs=pl.BlockSpec((t_tile, WIDTH), lambda t: (t, 0)),
        out_shape=jax.ShapeDtypeStruct((m, WIDTH), BF16),
        scratch_shapes=[pltpu.VMEM((N_PAIRS, LANES, LANES), F32), pltpu.VMEM((1, RW_PAD), F32)],
        compiler_params=_params(("arbitrary",)),
    )(p_rw, *params)


def _fox_prep_kernel(p_ref, qn_ref, kn_ref, bf_ref, part_ref, sel_ref,
                     q_out, k_out, vt_out, qa_out, ka_out, bound_out, carry_ref, kmax_ref, bmax_ref):
    tm = p_ref.shape[0]

    @pl.when(pl.program_id(0) == 0)
    def _():
        carry_ref[...] = jnp.zeros_like(carry_ref)
        kmax_ref[...] = jnp.zeros_like(kmax_ref)
        bmax_ref[...] = jnp.zeros_like(bmax_ref)

    w = WIDTH
    ones = _head_ones()
    q, k = p_ref[:, 0:w], p_ref[:, w:2 * w]
    q = q * lax.rsqrt(_headsum(q * q, ones) * (1.0 / HEAD_DIM) + NORM_EPS) * qn_ref[...]
    k = k * lax.rsqrt(_headsum(k * k, ones) * (1.0 / HEAD_DIM) + NORM_EPS) * kn_ref[...]
    qb = (q * (HEAD_DIM ** -0.5 * LOG2E)).astype(BF16)
    kb = k.astype(BF16)
    q_out[...] = qb
    k_out[...] = kb
    vt_out[...] = p_ref[:, 2 * w:3 * w].T.astype(vt_out.dtype)

    qf, kf = qb.astype(F32), kb.astype(F32)
    k_norm = jnp.sqrt(_headsum(kf * kf, ones, 1))
    kmax = jnp.maximum(kmax_ref[...], jnp.max(k_norm, axis=0, keepdims=True))
    kmax_ref[...] = kmax
    bound = jnp.sqrt(_headsum(qf * qf, ones, 1)) * kmax * BOUND_SLACK
    bsel = _mm(_bf(bound), [sel_ref[...]])

    logf = -_softplus(-(p_ref[:, 3 * w:3 * w + LANES] + bf_ref[...]))
    ri = lax.broadcasted_iota(jnp.int32, (tm, tm), 0)
    ci = lax.broadcasted_iota(jnp.int32, (tm, tm), 1)
    cum = _mm(_bf(ri >= ci), _split(logf, 3)) + carry_ref[...]
    carry_ref[...] = cum[tm - 1:tm, :]

    part = jnp.broadcast_to(part_ref[...], (tm, LANES))

    def pick(x):
        parts = [t.astype(F32) for t in _split(x, BIAS_PARTS)]
        return jnp.where(part == 0, parts[0], jnp.where(part == 1, parts[1], parts[2]))

    lane = lax.broadcasted_iota(jnp.int32, (tm, LANES), 1)
    low = lane < BIAS_LANES
    high = jnp.logical_and(lane >= BIAS_LANES, lane < 2 * BIAS_LANES)
    ka_out[...] = jnp.where(low, pick(cum * (-LOG2E)), jnp.where(high, 1.0, 0.0)).astype(ka_out.dtype)
    qa_out[...] = jnp.where(low, 1.0, jnp.where(high, pick(cum * LOG2E - bsel), 0.0)).astype(qa_out.dtype)
    bmax = jnp.maximum(bmax_ref[...], jnp.max(bsel, axis=0, keepdims=True))
    bmax_ref[...] = bmax
    bound_out[...] = jnp.broadcast_to(bmax, bound_out.shape)


def _fox_prep(p_fox, q_norm, k_norm, bf_rep, tm):
    m = p_fox.shape[0]
    tm = min(tm, m)
    qn = jnp.tile(q_norm, N_HEADS).reshape(1, WIDTH)
    kn = jnp.tile(k_norm, N_HEADS).reshape(1, WIDTH)
    part = (jnp.arange(LANES, dtype=jnp.int32) % BIAS_PARTS).reshape(1, LANES)
    heads = np.arange(N_HEADS)
    sel = np.zeros((WIDTH, LANES), np.float32)
    for j in range(BIAS_PARTS):
        sel[heads * HEAD_DIM, BIAS_LANES + BIAS_PARTS * heads + j] = 1.0
    sel = jnp.asarray(sel, BF16)
    full = lambda a: pl.BlockSpec(a.shape, lambda i: (0, 0))
    rows = lambda width: pl.BlockSpec((tm, width), lambda i: (i, 0))
    act = jax.ShapeDtypeStruct((m, WIDTH), BF16)
    aug = jax.ShapeDtypeStruct((m, LANES), BF16)
    return pl.pallas_call(
        _fox_prep_kernel,
        grid=(m // tm,),
        in_specs=[rows(FOX_PAD), full(qn), full(kn), full(bf_rep), full(part), full(sel)],
        out_specs=[rows(WIDTH), rows(WIDTH), pl.BlockSpec((WIDTH, tm), lambda i: (0, i)), rows(LANES), rows(LANES),
                   pl.BlockSpec((8, LANES), lambda i: (0, 0))],
        out_shape=[act, act, jax.ShapeDtypeStruct((WIDTH, m), BF16), aug, aug, jax.ShapeDtypeStruct((8, LANES), F32)],
        scratch_shapes=[pltpu.VMEM((1, LANES), F32), pltpu.VMEM((1, WIDTH), F32), pltpu.VMEM((1, LANES), F32)],
        compiler_params=_params(("arbitrary",)),
    )(p_fox, qn, kn, bf_rep, part, sel)


def _fox_attn_kernel(qi_ref, kb_ref, kind_ref, q_ref, qa_ref, k_ref, ka_ref, vt_ref, o_ref, m_ref, acc_ref, *,
                     running_max):
    tq = q_ref.shape[0]
    pair, step_id = pl.program_id(0), pl.program_id(1)
    kb, kind = kb_ref[step_id], kind_ref[step_id]

    @pl.when(kb == 0)
    def _():
        m_ref[...] = jnp.full_like(m_ref, -jnp.inf)
        acc_ref[...] = jnp.zeros_like(acc_ref)

    def step(kinds):
        q, qa = q_ref[...], qa_ref[...]
        lane_q = lax.broadcasted_iota(jnp.int32, (tq, LANES), 1)
        q_full = []
        for h in range(2):
            hmask = (lane_q < HEAD_DIM) if h == 0 else (lane_q >= HEAD_DIM)
            first = (2 * pair + h) * BIAS_PARTS
            in_group = lambda lo: jnp.logical_and(lane_q >= lo, lane_q < lo + BIAS_PARTS)
            aug = jnp.logical_or(in_group(first), in_group(first + BIAS_LANES))
            q_full.append(jnp.concatenate([jnp.where(hmask, q, jnp.zeros_like(q)),
                                           jnp.where(aug, qa, jnp.zeros_like(qa))], axis=1))
        subs = [slice(b * tq, (b + 1) * tq) for b in range(len(kinds))]
        k_full = [jnp.concatenate([k_ref[sl, :], ka_ref[sl, :]], axis=1) for sl in subs]
        ones_rows = jnp.ones((ONES_ROWS, tq), BF16)
        v_aug = [jnp.concatenate([vt_ref[:, sl], ones_rows], axis=0) for sl in subs]
        scores = lambda b: [_mm([k_full[b]], [q_full[h]], "nt") for h in range(2)]
        s = [scores(b) for b in range(min(AHEAD, len(kinds)))]
        causal = lax.broadcasted_iota(jnp.int32, (tq, tq), 0) <= lax.broadcasted_iota(jnp.int32, (tq, tq), 1)
        m = [m_ref[h] for h in range(2)]
        acc = [acc_ref[h] for h in range(2)]
        for b, diagonal in enumerate(kinds):
            if b + AHEAD < len(kinds):
                s.append(scores(b + AHEAD))
            sb = [jnp.where(causal, x, -jnp.inf) for x in s[b]] if diagonal else s[b]
            if running_max:
                m_new = [jnp.maximum(m[h], jnp.max(sb[h], axis=0, keepdims=True)) for h in range(2)]
                p = [jnp.exp2(sb[h] - m_new[h]).astype(BF16) for h in range(2)]
                acc = [jnp.exp2(m[h] - m_new[h]) * acc[h] for h in range(2)]
                m = m_new
            else:
                p = [jnp.exp2(sb[h]).astype(BF16) for h in range(2)]
            pv = [_mm([v_aug[b]], [p[h]]) for h in range(2)]
            acc = [acc[h] + pv[h] for h in range(2)]
        for h in range(2):
            acc_ref[h] = acc[h]
            if running_max:
                m_ref[h] = m[h]

    def finish():
        row = lax.broadcasted_iota(jnp.int32, (LANES, tq), 0)
        o = [acc_ref[h, 0:LANES, :] / acc_ref[h, LANES:LANES + 1, :] for h in range(2)]
        o_ref[...] = jnp.where(row < HEAD_DIM, o[0], o[1]).T.astype(o_ref.dtype)

    @pl.when(kind == 0)
    def _():
        step((False,) * KEY_BLOCKS)

    for n_full in range(KEY_BLOCKS):
        @pl.when(kind == 1 + n_full)
        def _(n_full=n_full):
            step((False,) * n_full + (True,))
            finish()


def _fox_attn(q, qa, k, ka, vt, tq, running_max):
    m = q.shape[0]
    tq = min(tq, m)
    nq = m // tq
    g = KEY_BLOCKS
    assert nq % g == 0, "key blocks are fetched in groups"
    steps = []
    for i in range(nq):
        steps += [(i, j, 0) for j in range(i // g)] + [(i, i // g, 1 + i % g)]
    qi_tab, kb_tab, kind_tab = (jnp.asarray([st[c] for st in steps], jnp.int32) for c in range(3))
    qspec = pl.BlockSpec((tq, LANES), lambda p, t, qi, kb, kind: (qi[t], p))
    qaspec = pl.BlockSpec((tq, LANES), lambda p, t, qi, kb, kind: (qi[t], 0))
    kspec = pl.BlockSpec((g * tq, LANES), lambda p, t, qi, kb, kind: (kb[t], p))
    kaspec = pl.BlockSpec((g * tq, LANES), lambda p, t, qi, kb, kind: (kb[t], 0))
    vtspec = pl.BlockSpec((LANES, g * tq), lambda p, t, qi, kb, kind: (p, kb[t]))
    return pl.pallas_call(
        functools.partial(_fox_attn_kernel, running_max=running_max),
        grid_spec=pltpu.PrefetchScalarGridSpec(
            num_scalar_prefetch=3,
            grid=(N_PAIRS, len(steps)),
            in_specs=[qspec, qaspec, kspec, kaspec, vtspec],
            out_specs=qspec,
            scratch_shapes=[pltpu.VMEM((2, 1, tq), F32), pltpu.VMEM((2, LANES + ONES_ROWS, tq), F32)],
        ),
        out_shape=jax.ShapeDtypeStruct((m, WIDTH), BF16),
        compiler_params=_params(("parallel", "arbitrary")),
    )(qi_tab, kb_tab, kind_tab, q, qa, k, ka, vt)


def _mem_kv_kernel(mem_ref, g_ref, w_ref, o_ref):
    x = mem_ref[...]
    ms = jnp.mean(x * x, axis=-1, keepdims=True)
    xn = (x * lax.rsqrt(ms + NORM_EPS) * g_ref[...]).astype(BF16)
    o_ref[...] = _dot(xn, w_ref[...]).astype(o_ref.dtype)


def _mem_kv(mem, ln_mem, w_ckv):
    n, d = mem.shape
    ins = [mem, ln_mem.reshape(1, d), w_ckv]
    return pl.pallas_call(
        _mem_kv_kernel,
        grid=(1,),
        in_specs=[pl.BlockSpec(a.shape, lambda i: (0, 0)) for a in ins],
        out_specs=pl.BlockSpec((n, w_ckv.shape[1]), lambda i: (0, 0)),
        out_shape=jax.ShapeDtypeStruct((n, w_ckv.shape[1]), BF16),
        compiler_params=_params(("arbitrary",)),
    )(*ins)


def _post_kernel(ya_ref, yb_ref, pg_ref, x_ref, wa_ref, wb_ref, wo_ref, lnc_ref, wcq_ref, kv_ref, wco_ref, o_ref):
    d = x_ref.shape[1]
    xw = X_HEADS * X_HEAD_DIM
    ga = pg_ref[:, 0:d].astype(F32)
    gb = pg_ref[:, d:2 * d].astype(F32)
    merged = ga * _dot(ya_ref[...], wa_ref[...]) + gb * _dot(yb_ref[...], wb_ref[...])
    h = x_ref[...] + _dot(merged.astype(BF16), wo_ref[...])

    ms = jnp.mean(h * h, axis=-1, keepdims=True)
    hn = (h * lax.rsqrt(ms + NORM_EPS) * lnc_ref[...]).astype(BF16)
    q = _dot(hn, wcq_ref[...])
    outs = []
    for hd in range(X_HEADS):
        cols = slice(hd * X_HEAD_DIM, (hd + 1) * X_HEAD_DIM)
        vcols = slice(xw + hd * X_HEAD_DIM, xw + (hd + 1) * X_HEAD_DIM)
        s = _dot_t(q[:, cols].astype(BF16), kv_ref[:, cols]) * (X_HEAD_DIM ** -0.5)
        s = s - jnp.max(s, axis=1, keepdims=True)
        e = jnp.exp(s)
        pr = e / jnp.sum(e, axis=1, keepdims=True)
        outs.append(_dot(pr.astype(BF16), kv_ref[:, vcols]))
    o = jnp.concatenate(outs, axis=1).astype(BF16)
    o_ref[...] = h + _dot(o, wco_ref[...])


def _post(ya, yb, pg, x, wa, wb, wo, ln_cross, wcq, kv, wco, tm):
    m, d = x.shape
    tm = min(tm, m)
    lnc = ln_cross.reshape(1, d)
    rows = lambda a: pl.BlockSpec((tm, a.shape[1]), lambda i: (i, 0))
    const = lambda a: pl.BlockSpec(a.shape, lambda i: (0, 0), pipeline_mode=pl.Buffered(1))
    return pl.pallas_call(
        _post_kernel,
        grid=(m // tm,),
        in_specs=[rows(ya), rows(yb), rows(pg), rows(x)] + [const(a) for a in (wa, wb, wo, lnc, wcq, kv, wco)],
        out_specs=pl.BlockSpec((tm, d), lambda i: (i, 0)),
        out_shape=jax.ShapeDtypeStruct((m, d), F32),
        compiler_params=_params(("parallel",)),
    )(ya, yb, pg, x, wa, wb, wo, lnc, wcq, kv, wco)


def _mlp_kernel(h_ref, g_ref, wu_ref, wd_ref, gf_ref, o_ref, xn_ref):
    f = pl.program_id(1)

    @pl.when(f == 0)
    def _():
        h = h_ref[...]
        ms = jnp.mean(h * h, axis=-1, keepdims=True)
        xn_ref[...] = (h * lax.rsqrt(ms + NORM_EPS) * g_ref[...]).astype(xn_ref.dtype)
        o_ref[...] = jnp.zeros_like(o_ref)

    u = jnp.maximum(_dot(xn_ref[...], wu_ref[...]), 0.0)
    o_ref[...] += _dot((u * u).astype(BF16), wd_ref[...])

    @pl.when(f == pl.num_programs(1) - 1)
    def _():
        h = h_ref[...] + o_ref[...]
        ms = jnp.mean(h * h, axis=-1, keepdims=True)
        o_ref[...] = h * lax.rsqrt(ms + NORM_EPS) * gf_ref[...]


def _mlp(h, ln_mlp, w_up, w_down, ln_final, tm, tf):
    m, d = h.shape
    dff = w_up.shape[1]
    tm, tf = min(tm, m), min(tf, dff)
    return pl.pallas_call(
        _mlp_kernel,
        grid=(m // tm, dff // tf),
        in_specs=[pl.BlockSpec((tm, d), lambda i, f: (i, 0)),
                  pl.BlockSpec((1, d), lambda i, f: (0, 0)),
                  pl.BlockSpec((d, tf), lambda i, f: (0, f)),
                  pl.BlockSpec((tf, d), lambda i, f: (f, 0)),
                  pl.BlockSpec((1, d), lambda i, f: (0, 0))],
        out_specs=pl.BlockSpec((tm, d), lambda i, f: (i, 0)),
        out_shape=jax.ShapeDtypeStruct((m, d), F32),
        scratch_shapes=[pltpu.VMEM((tm, d), BF16)],
        compiler_params=_params(("parallel", "arbitrary")),
    )(h, ln_mlp.reshape(1, d), w_up, w_down, ln_final.reshape(1, d))


def _pad_cols(a, width):
    return jnp.pad(a, ((0, 0), (0, width - a.shape[1])))


def _pad_rows(a, height):
    return jnp.pad(a, ((0, height - a.shape[0]), (0, 0)))


def _layer(h, mem, ln_mix, w_in, rw_mu, rw_w0, rw_w_up, rw_a0, rw_a_up, rw_g_up, rw_k_k, rw_k_a, rw_r_k,
           rw_ln_w, rw_ln_b, fox_b_f, fox_q_norm, fox_k_norm, w_proj_a, w_proj_b, w_out, ln_cross, ln_mem,
           w_cq, w_ckv, w_co, ln_mlp, w_up, w_down, ln_final):
    w3 = 3 * WIDTH
    o_ad = w3 + DECAY_LORA
    o_gd = o_ad + ICLR_LORA
    rw_cols = o_gd + GATE_LORA
    fox_cols = w3 + N_HEADS

    def rw_layout(a):
        return [a[:, :w3], _pad_cols(a[:, w3:o_ad], DECAY_PAD), _pad_cols(a[:, o_ad:o_gd], ICLR_PAD),
                _pad_cols(a[:, o_gd:rw_cols], GATE_PAD)]

    w_fz = jnp.tile(jnp.repeat(w_in[:, rw_cols + w3:rw_cols + fox_cols], BIAS_PARTS, axis=1), (1, 2))
    w_all = jnp.concatenate(rw_layout(w_in) + [w_in[:, rw_cols:rw_cols + w3], _pad_cols(w_fz, FZ_PAD),
                                               w_in[:, rw_cols + fox_cols:]], axis=1).astype(BF16)
    bf_rep = jnp.pad(jnp.tile(jnp.repeat(fox_b_f, BIAS_PARTS), 2), (0, LANES - 2 * BIAS_LANES)).reshape(1, LANES)
    mu_p = jnp.concatenate(rw_layout(rw_mu.reshape(1, -1)), axis=1).reshape(-1)

    xn = _rmsnorm_bf16(h, ln_mix, 512)
    p_rw = _matmul(xn, w_all, 0, RW_PAD, 1024, 1792)
    p_fox = _matmul(xn, w_all, RW_PAD, FOX_PAD, 1024, 1792)
    gates = _matmul(xn, w_all, RW_PAD + FOX_PAD, 2 * h.shape[1], 1024, 1024, BF16, gate=True)

    lora = lambda a, rows: _pad_rows(a, rows).astype(BF16)
    y_a = _rw_branch(p_rw, mu_p, rw_w0, lora(rw_w_up, DECAY_PAD), rw_a0, lora(rw_a_up, ICLR_PAD),
                     lora(rw_g_up, GATE_PAD), rw_k_k, rw_k_a, rw_r_k.reshape(-1), rw_ln_w, rw_ln_b, 2 * CHUNK)

    fq, fk, fvt, fqa, fka, fbound = _fox_prep(p_fox, fox_q_norm, fox_k_norm, bf_rep, 512)
    y_b = lax.cond(jnp.max(fbound) < SAFE_LOGIT_BOUND,
                   functools.partial(_fox_attn, tq=512, running_max=False),
                   functools.partial(_fox_attn, tq=512, running_max=True), fq, fqa, fk, fka, fvt)

    kv = _mem_kv(mem, ln_mem, w_ckv.astype(BF16))
    h = _post(y_a, y_b, gates, h, w_proj_a.astype(BF16), w_proj_b.astype(BF16), w_out.astype(BF16), ln_cross,
              w_cq.astype(BF16), kv, w_co.astype(BF16), 256)
    return _mlp(h, ln_mlp, w_up.astype(BF16), w_down.astype(BF16), ln_final, 1024, 512)


def kernel(x, mem, ln_mix, w_in, rw_mu, rw_w0, rw_w_up, rw_a0, rw_a_up, rw_g_up, rw_k_k, rw_k_a, rw_r_k, rw_ln_w, rw_ln_b, fox_b_f, fox_q_norm, fox_k_norm, w_proj_a, w_proj_b, w_out, ln_cross, ln_mem, w_cq, w_ckv, w_co, ln_mlp, w_up, w_down, ln_final):
    b, s, d = x.shape
    assert b == 1 and ln_mix.shape[0] == 1, "single sequence, single layer"
    out = _layer(x[0], mem[0], ln_mix[0], w_in[0], rw_mu[0], rw_w0[0], rw_w_up[0], rw_a0[0], rw_a_up[0], rw_g_up[0],
                 rw_k_k[0], rw_k_a[0], rw_r_k[0], rw_ln_w[0], rw_ln_b[0], fox_b_f[0], fox_q_norm[0], fox_k_norm[0],
                 w_proj_a[0], w_proj_b[0], w_out[0], ln_cross[0], ln_mem[0], w_cq[0], w_ckv[0], w_co[0],
                 ln_mlp[0], w_up[0], w_down[0], ln_final)
    return out.reshape(b, s, d)
```

```python
import functools

import numpy as np

import jax
import jax.numpy as jnp
from jax import lax
from jax.experimental import pallas as pl
from jax.experimental.pallas import tpu as pltpu

F32 = jnp.float32
BF16 = jnp.bfloat16

LANES = 128
HEAD_DIM = 64
N_HEADS = 16
WIDTH = N_HEADS * HEAD_DIM
N_PAIRS = WIDTH // LANES
DECAY_LORA, ICLR_LORA, GATE_LORA = 64, 64, 160
DECAY_PAD, ICLR_PAD, GATE_PAD = 128, 128, 256
RW_PAD = 3 * WIDTH + DECAY_PAD + ICLR_PAD + GATE_PAD
FZ_PAD = 512
BOUND_SLACK = 1.02
SAFE_LOGIT_BOUND = 50.0
BIAS_PARTS = 3
BIAS_LANES = BIAS_PARTS * N_HEADS
LOG2E = 1.4426950408889634
KEY_BLOCKS = 4
AHEAD = 2
ONES_ROWS = 16
FOX_PAD = 3 * WIDTH + FZ_PAD
X_HEADS, X_HEAD_DIM = 4, 128
NORM_EPS = 1e-5
GN_EPS = 64e-5
CHUNK = 64
SUB = 16
VMEM_LIMIT = 56 * 1024 * 1024


def _params(sem, vmem=VMEM_LIMIT):
    return pltpu.CompilerParams(dimension_semantics=sem, vmem_limit_bytes=vmem)


def _dot(a, b, precision=None):
    return jnp.dot(a, b, preferred_element_type=F32, precision=precision)


def _dot_t(a, b, precision=None):
    return lax.dot_general(a, b, (((1,), (1,)), ((), ())), preferred_element_type=F32, precision=precision)


def _split(x, terms):
    parts, rem = [], x
    for i in range(terms):
        part = rem.astype(BF16)
        parts.append(part)
        if i + 1 < terms:
            rem = rem - part.astype(F32)
    return parts


_DIMS = {"nn": (((1,), (0,)), ((), ())), "nt": (((1,), (1,)), ((), ())), "tn": (((0,), (0,)), ((), ()))}


def _mm(a_parts, b_parts, kind="nn"):
    order = max(len(a_parts), len(b_parts))
    acc = None
    for i, a in enumerate(a_parts):
        for j, b in enumerate(b_parts):
            if i + j < order:
                term = lax.dot_general(a, b, _DIMS[kind], preferred_element_type=F32)
                acc = term if acc is None else acc + term
    return acc


def _softplus(x):
    return jnp.maximum(x, 0.0) + jnp.log(1.0 + jnp.exp(-jnp.abs(x)))


def _sigmoid(x):
    return 1.0 / (1.0 + jnp.exp(-x))


def _head_ones():
    r = lax.broadcasted_iota(jnp.int32, (LANES, LANES), 0) // HEAD_DIM
    c = lax.broadcasted_iota(jnp.int32, (LANES, LANES), 1) // HEAD_DIM
    return (r == c).astype(F32)


def _headsum(x, ones_f32, terms=2):
    chunks = [x[:, c * LANES:(c + 1) * LANES] for c in range(x.shape[1] // LANES)]
    return jnp.concatenate(_stacked_headsum(chunks, [ones_f32.astype(BF16)], terms), axis=1)


def _rmsnorm_kernel(x_ref, g_ref, o_ref):
    x = x_ref[...]
    ms = jnp.mean(x * x, axis=-1, keepdims=True)
    o_ref[...] = (x * lax.rsqrt(ms + NORM_EPS) * g_ref[...]).astype(o_ref.dtype)


def _rmsnorm_bf16(x, g, tm):
    m, d = x.shape
    tm = min(tm, m)
    return pl.pallas_call(
        _rmsnorm_kernel,
        grid=(m // tm,),
        in_specs=[pl.BlockSpec((tm, d), lambda i: (i, 0)), pl.BlockSpec((1, d), lambda i: (0, 0))],
        out_specs=pl.BlockSpec((tm, d), lambda i: (i, 0)),
        out_shape=jax.ShapeDtypeStruct((m, d), BF16),
        compiler_params=_params(("parallel",)),
    )(x, g.reshape(1, d))


def _mm_kernel(a_ref, b_ref, o_ref, *, gate):
    acc = _dot(a_ref[...], b_ref[...])
    o_ref[...] = (_sigmoid(acc) if gate else acc).astype(o_ref.dtype)


def _matmul(a, b, col0, n, bm, bn, out_dtype=F32, gate=False):
    m, k = a.shape
    bm = min(bm, m)
    assert col0 % bn == 0 and n % bn == 0 and m % bm == 0
    off = col0 // bn
    return pl.pallas_call(
        functools.partial(_mm_kernel, gate=gate),
        grid=(n // bn, m // bm),
        in_specs=[pl.BlockSpec((bm, k), lambda j, i: (i, 0)), pl.BlockSpec((k, bn), lambda j, i: (0, j + off))],
        out_specs=pl.BlockSpec((bm, bn), lambda j, i: (i, j)),
        out_shape=jax.ShapeDtypeStruct((m, n), out_dtype),
        compiler_params=_params(("parallel", "parallel")),
    )(a, b)


def _bf(x):
    return [x.astype(BF16)]


def _stacked_headsum(xs, ones, terms):
    n = xs[0].shape[0]
    out = _mm(_split(jnp.concatenate(xs, axis=0), terms), ones)
    return [out[i * n:(i + 1) * n] for i in range(len(xs))]


def _rw_inputs(p, prev_row, mu, w0, wup, a0, aup, gup, k_k, k_a, ones):
    rows = lax.broadcasted_iota(jnp.int32, p.shape, 0)
    prev = jnp.where(rows == 0, prev_row, pltpu.roll(p, 1, 0))
    ps = p + (prev - p) * mu
    w = WIDTH
    r, k, v = ps[:, 0:w], ps[:, w:2 * w], ps[:, 2 * w:3 * w]
    o = 3 * w
    wd = ps[:, o:o + DECAY_PAD]
    ad = ps[:, o + DECAY_PAD:o + DECAY_PAD + ICLR_PAD]
    gd = ps[:, o + DECAY_PAD + ICLR_PAD:o + DECAY_PAD + ICLR_PAD + GATE_PAD]
    z = w0 + _mm(_bf(jnp.tanh(wd)), [wup])
    lw = -jnp.exp(-_softplus(-z) - 0.5)
    iclr = _sigmoid(a0 + _mm(_bf(ad), [aup]))
    gate = _mm(_bf(_sigmoid(gd)), [gup])
    kk = k * k_k
    kk2 = kk * kk
    ss = _stacked_headsum([kk2[:, c * LANES:(c + 1) * LANES] for c in range(N_PAIRS)], ones, 2)
    kk = kk * lax.rsqrt(jnp.maximum(jnp.concatenate(ss, axis=1), 1e-24))
    return r, k * (1.0 + (iclr - 1.0) * k_a), v, lw, -kk, kk * iclr, gate


def _pair_diag(x, head0):
    xb = x.astype(BF16)
    zero = jnp.zeros_like(xb)
    return [jnp.concatenate([jnp.where(head0, xb, zero), jnp.where(head0, zero, xb)], axis=0)]


def _unit_lower_inverse(a_list, sub_mask, eye, head0):
    c = CHUNK
    d = [jnp.where(sub_mask, a, 0.0) for a in a_list]
    e = [a - x for a, x in zip(a_list, d)]
    p = [eye + x for x in d]
    diag = lambda xs: [_pair_diag(x, head0) for x in xs]
    stack = lambda xs, ys: [_bf(jnp.concatenate([x, y], axis=0)) for x, y in zip(xs, ys)]
    dp = [_mm(_bf(x), y) for x, y in zip(d, diag(d))]
    for _ in range(SUB.bit_length() - 3):
        both = [_mm(x, y) for x, y in zip(stack(dp, p), diag(dp))]
        dp = [x[:c] for x in both]
        p = [x + y[c:] for x, y in zip(p, both)]
    p = [x + _mm(_bf(x), y) for x, y in zip(p, diag(dp))]
    f = [_mm(_bf(x), y) for x, y in zip(p, diag(e))]
    g = [eye + x for x in f]
    fp = [_mm(_bf(x), y) for x, y in zip(f, diag(f))]
    for _ in range((CHUNK // SUB).bit_length() - 3):
        both = [_mm(x, y) for x, y in zip(stack(fp, g), diag(fp))]
        fp = [x[:c] for x in both]
        g = [x + y[c:] for x, y in zip(g, both)]
    g = [x + _mm(_bf(x), y) for x, y in zip(g, diag(fp))]
    return [_mm(_bf(x), y) for x, y in zip(g, diag(p))]


def _rw_kernel(p_ref, mu_ref, w0_ref, wup_ref, a0_ref, aup_ref, gup_ref, kkw_ref, ka_ref, rk_ref, lnw_ref, lnb_ref,
               o_ref, h_ref, carry_ref):
    c = CHUNK
    t_tile = p_ref.shape[0]
    n_chunks = t_tile // c

    @pl.when(pl.program_id(0) == 0)
    def _():
        h_ref[...] = jnp.zeros_like(h_ref)
        carry_ref[...] = jnp.zeros_like(carry_ref)

    lane = lax.broadcasted_iota(jnp.int32, (c, LANES), 1)
    head0 = lane < HEAD_DIM
    ri = lax.broadcasted_iota(jnp.int32, (c, LANES), 0)
    ci = jnp.bitwise_and(lane, HEAD_DIM - 1)
    strict = ri > ci
    incl = ri >= ci
    eye = (ri == ci).astype(F32)
    sub_mask = (ri // SUB) == (ci // SUB)
    tril = _bf(lax.broadcasted_iota(jnp.int32, (c, c), 0) >= lax.broadcasted_iota(jnp.int32, (c, c), 1))
    ones_f = _head_ones()
    ones = _bf(ones_f)
    pair_mask = ones_f > 0.5
    diag = lambda x: _pair_diag(x, head0)
    units = [(s, p) for s in range(n_chunks) for p in range(N_PAIRS)]
    rows = lambda s: slice(s * c, (s + 1) * c)
    cols = lambda p: slice(p * LANES, (p + 1) * LANES)

    p_tile = p_ref[...]
    r_all, k_all, v_all, lw_all, a_all, b_all, gate_all = _rw_inputs(
        p_tile, carry_ref[...], mu_ref[...], w0_ref[...], wup_ref[...], a0_ref[...], aup_ref[...], gup_ref[...],
        kkw_ref[...], ka_ref[...], ones)
    carry_ref[...] = p_tile[t_tile - 1:t_tile, :]

    pre = []
    for s in range(n_chunks):
        lw, k, b = lw_all[rows(s), :], k_all[rows(s), :], b_all[rows(s), :]
        cl = _mm(tril, _split(lw, 3))
        cl_end = cl[c - 1:c, :]
        e_neg = jnp.exp(-cl)
        e_end = jnp.exp(cl_end - cl)
        pre.append(dict(rt=r_all[rows(s), :] * jnp.exp(cl), at=a_all[rows(s), :] * jnp.exp(cl - lw),
                        bt=b * e_neg, kt=k * e_neg, b_end=b * e_end, k_end=k * e_end, decay_end=jnp.exp(cl_end)))
    get = lambda name: [pre[s][name][:, cols(p)] for s, p in units]
    tile = lambda x: [x[rows(s), cols(p)] for s, p in units]
    r, k, v, gate = tile(r_all), tile(k_all), tile(v_all), tile(gate_all)
    rt, at = get("rt"), get("at")
    stack = lambda xs, ys: [jnp.concatenate([x, y], axis=0) for x, y in zip(xs, ys)]
    bk_end = [_bf(x) for x in stack(get("b_end"), get("k_end"))]
    v_diag = [diag(x) for x in v]
    mask2 = jnp.concatenate([strict, incl], axis=0)

    lhs = [_bf(x) for x in stack(at, rt)]
    sb = [_mm(x, diag(y), "nt") for x, y in zip(lhs, get("bt"))]
    sk = [_mm(x, diag(y), "nt") for x, y in zip(lhs, get("kt"))]
    skv = [_mm(_bf(jnp.where(mask2, x, 0.0)), y) for x, y in zip(sk, v_diag)]
    rb = [_bf(jnp.where(incl, x[c:], 0.0)) for x in sb]
    t_inv = [_bf(x) for x in _unit_lower_inverse([jnp.where(strict, x[:c], 0.0) for x in sb], sub_mask, eye, head0)]
    w = [_mm(x, diag(y)) for x, y in zip(t_inv, at)]
    tk = [_mm(x, diag(y[:c])) for x, y in zip(t_inv, skv)]
    s0_lhs = [_bf(x) for x in stack(w, rt)]

    ht = [h_ref[p] for p in range(N_PAIRS)]
    u, rs0 = [], []
    for s in range(n_chunks):
        i0 = s * N_PAIRS
        s0 = [_mm(s0_lhs[i0 + p], _bf(ht[p]), "nt") for p in range(N_PAIRS)]
        u += [s0[p][:c] + tk[i0 + p] for p in range(N_PAIRS)]
        rs0 += [s0[p][c:] for p in range(N_PAIRS)]
        upd = [_mm(_bf(jnp.concatenate([u[i0 + p], v[i0 + p]], axis=0)), bk_end[i0 + p], "tn") for p in range(N_PAIRS)]
        ht = [ht[p] * pre[s]["decay_end"][:, cols(p)] + jnp.where(pair_mask, upd[p], 0.0) for p in range(N_PAIRS)]
    for p in range(N_PAIRS):
        h_ref[p] = ht[p]

    y = [a + _mm(x, diag(z)) + b[c:] for a, x, z, b in zip(rs0, rb, u, skv)]
    rkr = [r[i] * k[i] * rk_ref[:, cols(p)] for i, (s, p) in enumerate(units)]
    sums = _stacked_headsum(y + rkr, ones, 2)
    dlt = [a - b * (1.0 / HEAD_DIM) for a, b in zip(y, sums[:len(units)])]
    var = _stacked_headsum([x * x for x in dlt], ones, 2)
    for i, (s, p) in enumerate(units):
        yn = dlt[i] * lax.rsqrt(var[i] * (1.0 / HEAD_DIM) + GN_EPS) * lnw_ref[:, cols(p)] + lnb_ref[:, cols(p)]
        o_ref[rows(s), cols(p)] = ((yn + sums[len(units) + i] * v[i]) * gate[i]).astype(o_ref.dtype)


def _rw_branch(p_rw, mu_p, w0, wup_p, a0, aup_p, gup_p, k_k, k_a, r_k, ln_w, ln_b, t_tile):
    m = p_rw.shape[0]
    t_tile = min(t_tile, m)
    row = lambda a: a.reshape(1, -1)
    full = lambda a: pl.BlockSpec(a.shape, lambda t: (0, 0))
    params = [row(mu_p), row(w0), wup_p, row(a0), aup_p, gup_p, row(k_k), row(k_a), row(r_k), row(ln_w), row(ln_b)]
    return pl.pallas_call(
        _rw_kernel,
        grid=(m // t_tile,),
        in_specs=[pl.BlockSpec((t_tile, RW_PAD), lambda t: (t, 0))] + [full(a) for a in params],
        out_specs=pl.BlockSpec((t_tile, WIDTH), lambda t: (t, 0)),
        out_shape=jax.ShapeDtypeStruct((m, WIDTH), BF16),
        scratch_shapes=[pltpu.VMEM((N_PAIRS, LANES, LANES), F32), pltpu.VMEM((1, RW_PAD), F32)],
        compiler_params=_params(("arbitrary",)),
    )(p_rw, *params)


def _fox_prep_kernel(p_ref, qn_ref, kn_ref, bf_ref, part_ref, sel_ref,
                     q_out, k_out, vt_out, qa_out, ka_out, bound_out, carry_ref, kmax_ref, bmax_ref):
    tm = p_ref.shape[0]

    @pl.when(pl.program_id(0) == 0)
    def _():
        carry_ref[...] = jnp.zeros_like(carry_ref)
        kmax_ref[...] = jnp.zeros_like(kmax_ref)
        bmax_ref[...] = jnp.zeros_like(bmax_ref)

    w = WIDTH
    ones = _head_ones()
    q, k = p_ref[:, 0:w], p_ref[:, w:2 * w]
    q = q * lax.rsqrt(_headsum(q * q, ones) * (1.0 / HEAD_DIM) + NORM_EPS) * qn_ref[...]
    k = k * lax.rsqrt(_headsum(k * k, ones) * (1.0 / HEAD_DIM) + NORM_EPS) * kn_ref[...]
    qb = (q * (HEAD_DIM ** -0.5 * LOG2E)).astype(BF16)
    kb = k.astype(BF16)
    q_out[...] = qb
    k_out[...] = kb
    vt_out[...] = p_ref[:, 2 * w:3 * w].T.astype(vt_out.dtype)

    qf, kf = qb.astype(F32), kb.astype(F32)
    k_norm = jnp.sqrt(_headsum(kf * kf, ones, 1))
    kmax = jnp.maximum(kmax_ref[...], jnp.max(k_norm, axis=0, keepdims=True))
    kmax_ref[...] = kmax
    bound = jnp.sqrt(_headsum(qf * qf, ones, 1)) * kmax * BOUND_SLACK
    bsel = _mm(_bf(bound), [sel_ref[...]])

    logf = -_softplus(-(p_ref[:, 3 * w:3 * w + LANES] + bf_ref[...]))
    ri = lax.broadcasted_iota(jnp.int32, (tm, tm), 0)
    ci = lax.broadcasted_iota(jnp.int32, (tm, tm), 1)
    cum = _mm(_bf(ri >= ci), _split(logf, 3)) + carry_ref[...]
    carry_ref[...] = cum[tm - 1:tm, :]

    part = jnp.broadcast_to(part_ref[...], (tm, LANES))

    def pick(x):
        parts = [t.astype(F32) for t in _split(x, BIAS_PARTS)]
        return jnp.where(part == 0, parts[0], jnp.where(part == 1, parts[1], parts[2]))

    lane = lax.broadcasted_iota(jnp.int32, (tm, LANES), 1)
    low = lane < BIAS_LANES
    high = jnp.logical_and(lane >= BIAS_LANES, lane < 2 * BIAS_LANES)
    ka_out[...] = jnp.where(low, pick(cum * (-LOG2E)), jnp.where(high, 1.0, 0.0)).astype(ka_out.dtype)
    qa_out[...] = jnp.where(low, 1.0, jnp.where(high, pick(cum * LOG2E - bsel), 0.0)).astype(qa_out.dtype)
    bmax = jnp.maximum(bmax_ref[...], jnp.max(bsel, axis=0, keepdims=True))
    bmax_ref[...] = bmax
    bound_out[...] = jnp.broadcast_to(bmax, bound_out.shape)


def _fox_prep(p_fox, q_norm, k_norm, bf_rep, tm):
    m = p_fox.shape[0]
    tm = min(tm, m)
    qn = jnp.tile(q_norm, N_HEADS).reshape(1, WIDTH)
    kn = jnp.tile(k_norm, N_HEADS).reshape(1, WIDTH)
    part = (jnp.arange(LANES, dtype=jnp.int32) % BIAS_PARTS).reshape(1, LANES)
    heads = np.arange(N_HEADS)
    sel = np.zeros((WIDTH, LANES), np.float32)
    for j in range(BIAS_PARTS):
        sel[heads * HEAD_DIM, BIAS_LANES + BIAS_PARTS * heads + j] = 1.0
    sel = jnp.asarray(sel, BF16)
    full = lambda a: pl.BlockSpec(a.shape, lambda i: (0, 0))
    rows = lambda width: pl.BlockSpec((tm, width), lambda i: (i, 0))
    act = jax.ShapeDtypeStruct((m, WIDTH), BF16)
    aug = jax.ShapeDtypeStruct((m, LANES), BF16)
    return pl.pallas_call(
        _fox_prep_kernel,
        grid=(m // tm,),
        in_specs=[rows(FOX_PAD), full(qn), full(kn), full(bf_rep), full(part), full(sel)],
        out_specs=[rows(WIDTH), rows(WIDTH), pl.BlockSpec((WIDTH, tm), lambda i: (0, i)), rows(LANES), rows(LANES),
                   pl.BlockSpec((8, LANES), lambda i: (0, 0))],
        out_shape=[act, act, jax.ShapeDtypeStruct((WIDTH, m), BF16), aug, aug, jax.ShapeDtypeStruct((8, LANES), F32)],
        scratch_shapes=[pltpu.VMEM((1, LANES), F32), pltpu.VMEM((1, WIDTH), F32), pltpu.VMEM((1, LANES), F32)],
        compiler_params=_params(("arbitrary",)),
    )(p_fox, qn, kn, bf_rep, part, sel)


def _fox_attn_kernel(qi_ref, kb_ref, kind_ref, q_ref, qa_ref, k_ref, ka_ref, vt_ref, o_ref, m_ref, acc_ref, *,
                     running_max):
    tq = q_ref.shape[0]
    pair, step_id = pl.program_id(0), pl.program_id(1)
    kb, kind = kb_ref[step_id], kind_ref[step_id]

    @pl.when(kb == 0)
    def _():
        m_ref[...] = jnp.full_like(m_ref, -jnp.inf)
        acc_ref[...] = jnp.zeros_like(acc_ref)

    def step(kinds):
        q, qa = q_ref[...], qa_ref[...]
        lane_q = lax.broadcasted_iota(jnp.int32, (tq, LANES), 1)
        q_full = []
        for h in range(2):
            hmask = (lane_q < HEAD_DIM) if h == 0 else (lane_q >= HEAD_DIM)
            first = (2 * pair + h) * BIAS_PARTS
            in_group = lambda lo: jnp.logical_and(lane_q >= lo, lane_q < lo + BIAS_PARTS)
            aug = jnp.logical_or(in_group(first), in_group(first + BIAS_LANES))
            q_full.append(jnp.concatenate([jnp.where(hmask, q, jnp.zeros_like(q)),
                                           jnp.where(aug, qa, jnp.zeros_like(qa))], axis=1))
        subs = [slice(b * tq, (b + 1) * tq) for b in range(len(kinds))]
        k_full = [jnp.concatenate([k_ref[sl, :], ka_ref[sl, :]], axis=1) for sl in subs]
        ones_rows = jnp.ones((ONES_ROWS, tq), BF16)
        v_aug = [jnp.concatenate([vt_ref[:, sl], ones_rows], axis=0) for sl in subs]
        scores = lambda b: [_mm([k_full[b]], [q_full[h]], "nt") for h in range(2)]
        s = [scores(b) for b in range(min(AHEAD, len(kinds)))]
        causal = lax.broadcasted_iota(jnp.int32, (tq, tq), 0) <= lax.broadcasted_iota(jnp.int32, (tq, tq), 1)
        m = [m_ref[h] for h in range(2)]
        acc = [acc_ref[h] for h in range(2)]
        for b, diagonal in enumerate(kinds):
            if b + AHEAD < len(kinds):
                s.append(scores(b + AHEAD))
            sb = [jnp.where(causal, x, -jnp.inf) for x in s[b]] if diagonal else s[b]
            if running_max:
                m_new = [jnp.maximum(m[h], jnp.max(sb[h], axis=0, keepdims=True)) for h in range(2)]
                p = [jnp.exp2(sb[h] - m_new[h]).astype(BF16) for h in range(2)]
                acc = [jnp.exp2(m[h] - m_new[h]) * acc[h] for h in range(2)]
                m = m_new
            else:
                p = [jnp.exp2(sb[h]).astype(BF16) for h in range(2)]
            pv = [_mm([v_aug[b]], [p[h]]) for h in range(2)]
            acc = [acc[h] + pv[h] for h in range(2)]
        for h in range(2):
            acc_ref[h] = acc[h]
            if running_max:
                m_ref[h] = m[h]

    def finish():
        row = lax.broadcasted_iota(jnp.int32, (LANES, tq), 0)
        o = [acc_ref[h, 0:LANES, :] / acc_ref[h, LANES:LANES + 1, :] for h in range(2)]
        o_ref[...] = jnp.where(row < HEAD_DIM, o[0], o[1]).T.astype(o_ref.dtype)

    @pl.when(kind == 0)
    def _():
        step((False,) * KEY_BLOCKS)

    for n_full in range(KEY_BLOCKS):
        @pl.when(kind == 1 + n_full)
        def _(n_full=n_full):
            step((False,) * n_full + (True,))
            finish()


def _fox_attn(q, qa, k, ka, vt, tq, running_max):
    m = q.shape[0]
    tq = min(tq, m)
    nq = m // tq
    g = KEY_BLOCKS
    assert nq % g == 0, "key blocks are fetched in groups"
    steps = []
    for i in range(nq):
        steps += [(i, j, 0) for j in range(i // g)] + [(i, i // g, 1 + i % g)]
    qi_tab, kb_tab, kind_tab = (jnp.asarray([st[c] for st in steps], jnp.int32) for c in range(3))
    qspec = pl.BlockSpec((tq, LANES), lambda p, t, qi, kb, kind: (qi[t], p))
    qaspec = pl.BlockSpec((tq, LANES), lambda p, t, qi, kb, kind: (qi[t], 0))
    kspec = pl.BlockSpec((g * tq, LANES), lambda p, t, qi, kb, kind: (kb[t], p))
    kaspec = pl.BlockSpec((g * tq, LANES), lambda p, t, qi, kb, kind: (kb[t], 0))
    vtspec = pl.BlockSpec((LANES, g * tq), lambda p, t, qi, kb, kind: (p, kb[t]))
    return pl.pallas_call(
        functools.partial(_fox_attn_kernel, running_max=running_max),
        grid_spec=pltpu.PrefetchScalarGridSpec(
            num_scalar_prefetch=3,
            grid=(N_PAIRS, len(steps)),
            in_specs=[qspec, qaspec, kspec, kaspec, vtspec],
            out_specs=qspec,
            scratch_shapes=[pltpu.VMEM((2, 1, tq), F32), pltpu.VMEM((2, LANES + ONES_ROWS, tq), F32)],
        ),
        out_shape=jax.ShapeDtypeStruct((m, WIDTH), BF16),
        compiler_params=_params(("parallel", "arbitrary")),
    )(qi_tab, kb_tab, kind_tab, q, qa, k, ka, vt)


def _mem_kv_kernel(mem_ref, g_ref, w_ref, o_ref):
    x = mem_ref[...]
    ms = jnp.mean(x * x, axis=-1, keepdims=True)
    xn = (x * lax.rsqrt(ms + NORM_EPS) * g_ref[...]).astype(BF16)
    o_ref[...] = _dot(xn, w_ref[...]).astype(o_ref.dtype)


def _mem_kv(mem, ln_mem, w_ckv):
    n, d = mem.shape
    ins = [mem, ln_mem.reshape(1, d), w_ckv]
    return pl.pallas_call(
        _mem_kv_kernel,
        grid=(1,),
        in_specs=[pl.BlockSpec(a.shape, lambda i: (0, 0)) for a in ins],
        out_specs=pl.BlockSpec((n, w_ckv.shape[1]), lambda i: (0, 0)),
        out_shape=jax.ShapeDtypeStruct((n, w_ckv.shape[1]), BF16),
        compiler_params=_params(("arbitrary",)),
    )(*ins)


def _post_kernel(ya_ref, yb_ref, pg_ref, x_ref, wa_ref, wb_ref, wo_ref, lnc_ref, wcq_ref, kv_ref, wco_ref, o_ref):
    d = x_ref.shape[1]
    xw = X_HEADS * X_HEAD_DIM
    ga = pg_ref[:, 0:d].astype(F32)
    gb = pg_ref[:, d:2 * d].astype(F32)
    merged = ga * _dot(ya_ref[...], wa_ref[...]) + gb * _dot(yb_ref[...], wb_ref[...])
    h = x_ref[...] + _dot(merged.astype(BF16), wo_ref[...])

    ms = jnp.mean(h * h, axis=-1, keepdims=True)
    hn = (h * lax.rsqrt(ms + NORM_EPS) * lnc_ref[...]).astype(BF16)
    q = _dot(hn, wcq_ref[...])
    outs = []
    for hd in range(X_HEADS):
        cols = slice(hd * X_HEAD_DIM, (hd + 1) * X_HEAD_DIM)
        vcols = slice(xw + hd * X_HEAD_DIM, xw + (hd + 1) * X_HEAD_DIM)
        s = _dot_t(q[:, cols].astype(BF16), kv_ref[:, cols]) * (X_HEAD_DIM ** -0.5)
        s = s - jnp.max(s, axis=1, keepdims=True)
        e = jnp.exp(s)
        pr = e / jnp.sum(e, axis=1, keepdims=True)
        outs.append(_dot(pr.astype(BF16), kv_ref[:, vcols]))
    o = jnp.concatenate(outs, axis=1).astype(BF16)
    o_ref[...] = h + _dot(o, wco_ref[...])


def _post(ya, yb, pg, x, wa, wb, wo, ln_cross, wcq, kv, wco, tm):
    m, d = x.shape
    tm = min(tm, m)
    lnc = ln_cross.reshape(1, d)
    rows = lambda a: pl.BlockSpec((tm, a.shape[1]), lambda i: (i, 0))
    const = lambda a: pl.BlockSpec(a.shape, lambda i: (0, 0), pipeline_mode=pl.Buffered(1))
    return pl.pallas_call(
        _post_kernel,
        grid=(m // tm,),
        in_specs=[rows(ya), rows(yb), rows(pg), rows(x)] + [const(a) for a in (wa, wb, wo, lnc, wcq, kv, wco)],
        out_specs=pl.BlockSpec((tm, d), lambda i: (i, 0)),
        out_shape=jax.ShapeDtypeStruct((m, d), F32),
        compiler_params=_params(("parallel",)),
    )(ya, yb, pg, x, wa, wb, wo, lnc, wcq, kv, wco)


def _mlp_kernel(h_ref, g_ref, wu_ref, wd_ref, gf_ref, o_ref, xn_ref):
    f = pl.program_id(1)

    @pl.when(f == 0)
    def _():
        h = h_ref[...]
        ms = jnp.mean(h * h, axis=-1, keepdims=True)
        xn_ref[...] = (h * lax.rsqrt(ms + NORM_EPS) * g_ref[...]).astype(xn_ref.dtype)
        o_ref[...] = jnp.zeros_like(o_ref)

    u = jnp.maximum(_dot(xn_ref[...], wu_ref[...]), 0.0)
    o_ref[...] += _dot((u * u).astype(BF16), wd_ref[...])

    @pl.when(f == pl.num_programs(1) - 1)
    def _():
        h = h_ref[...] + o_ref[...]
        ms = jnp.mean(h * h, axis=-1, keepdims=True)
        o_ref[...] = h * lax.rsqrt(ms + NORM_EPS) * gf_ref[...]


def _mlp(h, ln_mlp, w_up, w_down, ln_final, tm, tf):
    m, d = h.shape
    dff = w_up.shape[1]
    tm, tf = min(tm, m), min(tf, dff)
    return pl.pallas_call(
        _mlp_kernel,
        grid=(m // tm, dff // tf),
        in_specs=[pl.BlockSpec((tm, d), lambda i, f: (i, 0)),
                  pl.BlockSpec((1, d), lambda i, f: (0, 0)),
                  pl.BlockSpec((d, tf), lambda i, f: (0, f)),
                  pl.BlockSpec((tf, d), lambda i, f: (f, 0)),
                  pl.BlockSpec((1, d), lambda i, f: (0, 0))],
        out_specs=pl.BlockSpec((tm, d), lambda i, f: (i, 0)),
        out_shape=jax.ShapeDtypeStruct((m, d), F32),
        scratch_shapes=[pltpu.VMEM((tm, d), BF16)],
        compiler_params=_params(("parallel", "arbitrary")),
    )(h, ln_mlp.reshape(1, d), w_up, w_down, ln_final.reshape(1, d))


def _pad_cols(a, width):
    return jnp.pad(a, ((0, 0), (0, width - a.shape[1])))


def _pad_rows(a, height):
    return jnp.pad(a, ((0, height - a.shape[0]), (0, 0)))


def _layer(h, mem, ln_mix, w_in, rw_mu, rw_w0, rw_w_up, rw_a0, rw_a_up, rw_g_up, rw_k_k, rw_k_a, rw_r_k,
           rw_ln_w, rw_ln_b, fox_b_f, fox_q_norm, fox_k_norm, w_proj_a, w_proj_b, w_out, ln_cross, ln_mem,
           w_cq, w_ckv, w_co, ln_mlp, w_up, w_down, ln_final):
    w3 = 3 * WIDTH
    o_ad = w3 + DECAY_LORA
    o_gd = o_ad + ICLR_LORA
    rw_cols = o_gd + GATE_LORA
    fox_cols = w3 + N_HEADS

    def rw_layout(a):
        return [a[:, :w3], _pad_cols(a[:, w3:o_ad], DECAY_PAD), _pad_cols(a[:, o_ad:o_gd], ICLR_PAD),
                _pad_cols(a[:, o_gd:rw_cols], GATE_PAD)]

    w_fz = jnp.tile(jnp.repeat(w_in[:, rw_cols + w3:rw_cols + fox_cols], BIAS_PARTS, axis=1), (1, 2))
    w_all = jnp.concatenate(rw_layout(w_in) + [w_in[:, rw_cols:rw_cols + w3], _pad_cols(w_fz, FZ_PAD),
                                               w_in[:, rw_cols + fox_cols:]], axis=1).astype(BF16)
    bf_rep = jnp.pad(jnp.tile(jnp.repeat(fox_b_f, BIAS_PARTS), 2), (0, LANES - 2 * BIAS_LANES)).reshape(1, LANES)
    mu_p = jnp.concatenate(rw_layout(rw_mu.reshape(1, -1)), axis=1).reshape(-1)

    xn = _rmsnorm_bf16(h, ln_mix, 512)
    p_rw = _matmul(xn, w_all, 0, RW_PAD, 1024, 1792)
    p_fox = _matmul(xn, w_all, RW_PAD, FOX_PAD, 1024, 1792)
    gates = _matmul(xn, w_all, RW_PAD + FOX_PAD, 2 * h.shape[1], 2048, 1024, BF16, gate=True)

    lora = lambda a, rows: _pad_rows(a, rows).astype(BF16)
    y_a = _rw_branch(p_rw, mu_p, rw_w0, lora(rw_w_up, DECAY_PAD), rw_a0, lora(rw_a_up, ICLR_PAD),
                     lora(rw_g_up, GATE_PAD), rw_k_k, rw_k_a, rw_r_k.reshape(-1), rw_ln_w, rw_ln_b, 2 * CHUNK)

    fq, fk, fvt, fqa, fka, fbound = _fox_prep(p_fox, fox_q_norm, fox_k_norm, bf_rep, 512)
    y_b = lax.cond(jnp.max(fbound) < SAFE_LOGIT_BOUND,
                   functools.partial(_fox_attn, tq=512, running_max=False),
                   functools.partial(_fox_attn, tq=512, running_max=True), fq, fqa, fk, fka, fvt)

    kv = _mem_kv(mem, ln_mem, w_ckv.astype(BF16))
    h = _post(y_a, y_b, gates, h, w_proj_a.astype(BF16), w_proj_b.astype(BF16), w_out.astype(BF16), ln_cross,
              w_cq.astype(BF16), kv, w_co.astype(BF16), 512)
    return _mlp(h, ln_mlp, w_up.astype(BF16), w_down.astype(BF16), ln_final, 512, 1024)


def kernel(x, mem, ln_mix, w_in, rw_mu, rw_w0, rw_w_up, rw_a0, rw_a_up, rw_g_up, rw_k_k, rw_k_a, rw_r_k, rw_ln_w, rw_ln_b, fox_b_f, fox_q_norm, fox_k_norm, w_proj_a, w_proj_b, w_out, ln_cross, ln_mem, w_cq, w_ckv, w_co, ln_mlp, w_up, w_down, ln_final):
    b, s, d = x.shape
    assert b == 1 and ln_mix.shape[0] == 1, "single sequence, single layer"
    out = _layer(x[0], mem[0], ln_mix[0], w_in[0], rw_mu[0], rw_w0[0], rw_w_up[0], rw_a0[0], rw_a_up[0], rw_g_up[0],
                 rw_k_k[0], rw_k_a[0], rw_r_k[0], rw_ln_w[0], rw_ln_b[0], fox_b_f[0], fox_q_norm[0], fox_k_norm[0],
                 w_proj_a[0], w_proj_b[0], w_out[0], ln_cross[0], ln_mem[0], w_cq[0], w_ckv[0], w_co[0],
                 ln_mlp[0], w_up[0], w_down[0], ln_final)
    return out.reshape(b, s, d)
```

```python
import functools

import numpy as np

import jax
import jax.numpy as jnp
from jax import lax
from jax.experimental import pallas as pl
from jax.experimental.pallas import tpu as pltpu

F32 = jnp.float32
BF16 = jnp.bfloat16

LANES = 128
HEAD_DIM = 64
N_HEADS = 16
WIDTH = N_HEADS * HEAD_DIM
N_PAIRS = WIDTH // LANES
DECAY_LORA, ICLR_LORA, GATE_LORA = 64, 64, 160
LORA_PAD = 128
GATE_PAD = 256
RW_COLS = 3 * WIDTH + DECAY_LORA + ICLR_LORA + GATE_LORA
RW_PAD = 3584
FZ_PAD = 512
FOX_PAD = 3 * WIDTH + FZ_PAD
BOUND_SLACK = 1.02
SAFE_LOGIT_BOUND = 50.0
BIAS_PARTS = 3
BIAS_LANES = BIAS_PARTS * N_HEADS
LOG2E = 1.4426950408889634
KEY_BLOCKS = 4
AHEAD = 2
ONES_ROWS = 16
X_HEADS, X_HEAD_DIM = 4, 128
NORM_EPS = 1e-5
GN_EPS = 64e-5
CHUNK = 64
SUB = 16
VMEM_LIMIT = 56 * 1024 * 1024


def _params(sem, vmem=VMEM_LIMIT):
    return pltpu.CompilerParams(dimension_semantics=sem, vmem_limit_bytes=vmem)


def _dot(a, b):
    return jnp.dot(a, b, preferred_element_type=F32)


def _dot_t(a, b):
    return lax.dot_general(a, b, (((1,), (1,)), ((), ())), preferred_element_type=F32)


def _split(x, terms):
    parts, rem = [], x
    for i in range(terms):
        part = rem.astype(BF16)
        parts.append(part)
        if i + 1 < terms:
            rem = rem - part.astype(F32)
    return parts


_DIMS = {"nn": (((1,), (0,)), ((), ())), "nt": (((1,), (1,)), ((), ())), "tn": (((0,), (0,)), ((), ()))}


def _mm(a_parts, b_parts, kind="nn"):
    order = max(len(a_parts), len(b_parts))
    acc = None
    for i, a in enumerate(a_parts):
        for j, b in enumerate(b_parts):
            if i + j < order:
                term = lax.dot_general(a, b, _DIMS[kind], preferred_element_type=F32)
                acc = term if acc is None else acc + term
    return acc


def _bf(x):
    return [x.astype(BF16)]


def _softplus(x):
    return jnp.maximum(x, 0.0) + jnp.log(1.0 + jnp.exp(-jnp.abs(x)))


def _sigmoid(x):
    return 1.0 / (1.0 + jnp.exp(-x))


def _head_ones():
    r = lax.broadcasted_iota(jnp.int32, (LANES, LANES), 0) // HEAD_DIM
    c = lax.broadcasted_iota(jnp.int32, (LANES, LANES), 1) // HEAD_DIM
    return (r == c).astype(F32)


def _stacked_headsum(xs, ones, terms):
    n = xs[0].shape[0]
    out = _mm(_split(jnp.concatenate(xs, axis=0), terms), ones)
    return [out[i * n:(i + 1) * n] for i in range(len(xs))]


def _headsum(x, ones_f32, terms=2):
    chunks = [x[:, c * LANES:(c + 1) * LANES] for c in range(x.shape[1] // LANES)]
    return jnp.concatenate(_stacked_headsum(chunks, [ones_f32.astype(BF16)], terms), axis=1)


def _rmsnorm_kernel(x_ref, g_ref, o_ref):
    x = x_ref[...]
    ms = jnp.mean(x * x, axis=-1, keepdims=True)
    o_ref[...] = (x * lax.rsqrt(ms + NORM_EPS) * g_ref[...]).astype(o_ref.dtype)


def _rmsnorm_bf16(x, g, tm):
    m, d = x.shape
    tm = min(tm, m)
    return pl.pallas_call(
        _rmsnorm_kernel,
        grid=(m // tm,),
        in_specs=[pl.BlockSpec((tm, d), lambda i: (i, 0)), pl.BlockSpec((1, d), lambda i: (0, 0))],
        out_specs=pl.BlockSpec((tm, d), lambda i: (i, 0)),
        out_shape=jax.ShapeDtypeStruct((m, d), BF16),
        compiler_params=_params(("parallel",)),
    )(x, g.reshape(1, d))


def _weight_layout_kernel(w_ref, selz_ref, rw_out, fox_out, gate_out):
    w3 = 3 * WIDTH
    rw_used = -(-RW_COLS // LANES) * LANES
    rw_out[:, 0:rw_used] = w_ref[:, 0:rw_used].astype(BF16)
    rw_out[:, rw_used:] = jnp.zeros((rw_out.shape[0], RW_PAD - rw_used), BF16)
    fox_out[:, 0:w3] = w_ref[:, RW_COLS:RW_COLS + w3].astype(BF16)
    fz0 = RW_COLS + w3
    base = fz0 // LANES * LANES
    block = w_ref[:, base:base + LANES].astype(BF16)
    fox_out[:, w3:w3 + LANES] = _mm([block], [selz_ref[...]]).astype(BF16)
    fox_out[:, w3 + LANES:] = jnp.zeros((fox_out.shape[0], FZ_PAD - LANES), BF16)
    gate_out[...] = w_ref[:, fz0 + N_HEADS:].astype(BF16)


def _proj_weights(w_in, tr):
    k, n = w_in.shape
    n_gate = n - RW_COLS - 3 * WIDTH - N_HEADS
    heads = np.arange(N_HEADS)
    selz = np.zeros((LANES, LANES), np.float32)
    for j in range(BIAS_PARTS):
        for group in (0, BIAS_LANES):
            selz[(RW_COLS + 3 * WIDTH) % LANES + heads, group + BIAS_PARTS * heads + j] = 1.0
    selz = jnp.asarray(selz, BF16)
    rows = lambda width: pl.BlockSpec((tr, width), lambda i: (i, 0))
    return pl.pallas_call(
        _weight_layout_kernel,
        grid=(k // tr,),
        in_specs=[rows(n), pl.BlockSpec(selz.shape, lambda i: (0, 0))],
        out_specs=[rows(RW_PAD), rows(FOX_PAD), rows(n_gate)],
        out_shape=[jax.ShapeDtypeStruct((k, width), BF16) for width in (RW_PAD, FOX_PAD, n_gate)],
        compiler_params=_params(("parallel",)),
    )(w_in, selz)


def _mm_kernel(a_ref, b_ref, o_ref, *, gate):
    acc = _dot(a_ref[...], b_ref[...])
    o_ref[...] = (_sigmoid(acc) if gate else acc).astype(o_ref.dtype)


def _matmul(a, b, bm, bn, out_dtype=F32, gate=False):
    m, k = a.shape
    n = b.shape[1]
    bm, bn = min(bm, m), min(bn, n)
    assert n % bn == 0 and m % bm == 0
    return pl.pallas_call(
        functools.partial(_mm_kernel, gate=gate),
        grid=(n // bn, m // bm),
        in_specs=[pl.BlockSpec((bm, k), lambda j, i: (i, 0)), pl.BlockSpec((k, bn), lambda j, i: (0, j))],
        out_specs=pl.BlockSpec((bm, bn), lambda j, i: (i, j)),
        out_shape=jax.ShapeDtypeStruct((m, n), out_dtype),
        compiler_params=_params(("parallel", "parallel")),
    )(a, b)


def _rw_inputs(p, prev_row, mu, w0, wup, a0, aup, gup, k_k, k_a, ones):
    rows = lax.broadcasted_iota(jnp.int32, p.shape, 0)
    prev = jnp.where(rows == 0, prev_row, pltpu.roll(p, 1, 0))
    ps = p + (prev - p) * mu
    w = WIDTH
    r, k, v = ps[:, 0:w], ps[:, w:2 * w], ps[:, 2 * w:3 * w]
    o = 3 * w
    lora = ps[:, o:o + LORA_PAD]
    gd = ps[:, o + LORA_PAD:o + LORA_PAD + GATE_PAD]
    z = w0 + _mm(_bf(jnp.tanh(lora)), [wup])
    lw = -jnp.exp(-_softplus(-z) - 0.5)
    iclr = _sigmoid(a0 + _mm(_bf(lora), [aup]))
    gate = _mm(_bf(_sigmoid(gd)), [gup])
    kk = k * k_k
    kk2 = kk * kk
    ss = _stacked_headsum([kk2[:, c * LANES:(c + 1) * LANES] for c in range(N_PAIRS)], ones, 2)
    kk = kk * lax.rsqrt(jnp.maximum(jnp.concatenate(ss, axis=1), 1e-24))
    return r, k * (1.0 + (iclr - 1.0) * k_a), v, lw, -kk, kk * iclr, gate


def _pair_diag(x, head0):
    xb = x.astype(BF16)
    zero = jnp.zeros_like(xb)
    return [jnp.concatenate([jnp.where(head0, xb, zero), jnp.where(head0, zero, xb)], axis=0)]


def _unit_lower_inverse(a_list, sub_mask, eye, head0):
    c = CHUNK
    d = [jnp.where(sub_mask, a, 0.0) for a in a_list]
    e = [a - x for a, x in zip(a_list, d)]
    p = [eye + x for x in d]
    diag = lambda xs: [_pair_diag(x, head0) for x in xs]
    stack = lambda xs, ys: [_bf(jnp.concatenate([x, y], axis=0)) for x, y in zip(xs, ys)]
    dp = [_mm(_bf(x), y) for x, y in zip(d, diag(d))]
    for _ in range(SUB.bit_length() - 3):
        both = [_mm(x, y) for x, y in zip(stack(dp, p), diag(dp))]
        dp = [x[:c] for x in both]
        p = [x + y[c:] for x, y in zip(p, both)]
    p = [x + _mm(_bf(x), y) for x, y in zip(p, diag(dp))]
    f = [_mm(_bf(x), y) for x, y in zip(p, diag(e))]
    g = [eye + x for x in f]
    fp = [_mm(_bf(x), y) for x, y in zip(f, diag(f))]
    for _ in range((CHUNK // SUB).bit_length() - 3):
        both = [_mm(x, y) for x, y in zip(stack(fp, g), diag(fp))]
        fp = [x[:c] for x in both]
        g = [x + y[c:] for x, y in zip(g, both)]
    g = [x + _mm(_bf(x), y) for x, y in zip(g, diag(fp))]
    return [_mm(_bf(x), y) for x, y in zip(g, diag(p))]


def _rw_kernel(p_ref, mu_ref, w0_ref, wup_ref, a0_ref, aup_ref, gup_ref, kkw_ref, ka_ref, rk_ref, lnw_ref, lnb_ref,
               o_ref, h_ref, carry_ref):
    c = CHUNK
    t_tile = p_ref.shape[0]
    n_chunks = t_tile // c

    @pl.when(pl.program_id(0) == 0)
    def _():
        h_ref[...] = jnp.zeros_like(h_ref)
        carry_ref[...] = jnp.zeros_like(carry_ref)

    lane = lax.broadcasted_iota(jnp.int32, (c, LANES), 1)
    head0 = lane < HEAD_DIM
    ri = lax.broadcasted_iota(jnp.int32, (c, LANES), 0)
    ci = jnp.bitwise_and(lane, HEAD_DIM - 1)
    strict = ri > ci
    incl = ri >= ci
    eye = (ri == ci).astype(F32)
    sub_mask = (ri // SUB) == (ci // SUB)
    tril = _bf(lax.broadcasted_iota(jnp.int32, (c, c), 0) >= lax.broadcasted_iota(jnp.int32, (c, c), 1))
    ones_f = _head_ones()
    ones = _bf(ones_f)
    pair_mask = ones_f > 0.5
    diag = lambda x: _pair_diag(x, head0)
    units = [(s, p) for s in range(n_chunks) for p in range(N_PAIRS)]
    rows = lambda s: slice(s * c, (s + 1) * c)
    cols = lambda p: slice(p * LANES, (p + 1) * LANES)

    p_tile = p_ref[...]
    r_all, k_all, v_all, lw_all, a_all, b_all, gate_all = _rw_inputs(
        p_tile, carry_ref[...], mu_ref[...], w0_ref[...], wup_ref[...], a0_ref[...], aup_ref[...], gup_ref[...],
        kkw_ref[...], ka_ref[...], ones)
    carry_ref[...] = p_tile[t_tile - 1:t_tile, :]

    pre = []
    for s in range(n_chunks):
        lw, k, b = lw_all[rows(s), :], k_all[rows(s), :], b_all[rows(s), :]
        cl = _mm(tril, _split(lw, 3))
        cl_end = cl[c - 1:c, :]
        e_neg = jnp.exp(-cl)
        e_end = jnp.exp(cl_end - cl)
        pre.append(dict(rt=r_all[rows(s), :] * jnp.exp(cl), at=a_all[rows(s), :] * jnp.exp(cl - lw),
                        bt=b * e_neg, kt=k * e_neg, b_end=b * e_end, k_end=k * e_end, decay_end=jnp.exp(cl_end)))
    get = lambda name: [pre[s][name][:, cols(p)] for s, p in units]
    tile = lambda x: [x[rows(s), cols(p)] for s, p in units]
    r, k, v, gate = tile(r_all), tile(k_all), tile(v_all), tile(gate_all)
    rt, at = get("rt"), get("at")
    stack = lambda xs, ys: [jnp.concatenate([x, y], axis=0) for x, y in zip(xs, ys)]
    bk_end = [_bf(x) for x in stack(get("b_end"), get("k_end"))]
    v_diag = [diag(x) for x in v]
    mask2 = jnp.concatenate([strict, incl], axis=0)

    lhs = [_bf(x) for x in stack(at, rt)]
    sb = [_mm(x, diag(y), "nt") for x, y in zip(lhs, get("bt"))]
    sk = [_mm(x, diag(y), "nt") for x, y in zip(lhs, get("kt"))]
    skv = [_mm(_bf(jnp.where(mask2, x, 0.0)), y) for x, y in zip(sk, v_diag)]
    rb = [_bf(jnp.where(incl, x[c:], 0.0)) for x in sb]
    t_inv = [_bf(x) for x in _unit_lower_inverse([jnp.where(strict, x[:c], 0.0) for x in sb], sub_mask, eye, head0)]
    w = [_mm(x, diag(y)) for x, y in zip(t_inv, at)]
    tk = [_mm(x, diag(y[:c])) for x, y in zip(t_inv, skv)]
    s0_lhs = [_bf(x) for x in stack(w, rt)]

    ht = [h_ref[p] for p in range(N_PAIRS)]
    u, rs0 = [], []
    for s in range(n_chunks):
        i0 = s * N_PAIRS
        s0 = [_mm(s0_lhs[i0 + p], _bf(ht[p]), "nt") for p in range(N_PAIRS)]
        u += [s0[p][:c] + tk[i0 + p] for p in range(N_PAIRS)]
        rs0 += [s0[p][c:] for p in range(N_PAIRS)]
        upd = [_mm(_bf(jnp.concatenate([u[i0 + p], v[i0 + p]], axis=0)), bk_end[i0 + p], "tn") for p in range(N_PAIRS)]
        ht = [ht[p] * pre[s]["decay_end"][:, cols(p)] + jnp.where(pair_mask, upd[p], 0.0) for p in range(N_PAIRS)]
    for p in range(N_PAIRS):
        h_ref[p] = ht[p]

    y = [a + _mm(x, diag(z)) + b[c:] for a, x, z, b in zip(rs0, rb, u, skv)]
    rkr = [r[i] * k[i] * rk_ref[:, cols(p)] for i, (s, p) in enumerate(units)]
    sums = _stacked_headsum(y + rkr, ones, 2)
    dlt = [a - b * (1.0 / HEAD_DIM) for a, b in zip(y, sums[:len(units)])]
    var = _stacked_headsum([x * x for x in dlt], ones, 2)
    for i, (s, p) in enumerate(units):
        yn = dlt[i] * lax.rsqrt(var[i] * (1.0 / HEAD_DIM) + GN_EPS) * lnw_ref[:, cols(p)] + lnb_ref[:, cols(p)]
        o_ref[rows(s), cols(p)] = ((yn + sums[len(units) + i] * v[i]) * gate[i]).astype(o_ref.dtype)


def _rw_branch(p_rw, mu_p, w0, wup_p, a0, aup_p, gup_p, k_k, k_a, r_k, ln_w, ln_b, t_tile):
    m = p_rw.shape[0]
    t_tile = min(t_tile, m)
    row = lambda a: a.reshape(1, -1)
    full = lambda a: pl.BlockSpec(a.shape, lambda t: (0, 0))
    params = [row(mu_p), row(w0), wup_p, row(a0), aup_p, gup_p, row(k_k), row(k_a), row(r_k), row(ln_w), row(ln_b)]
    return pl.pallas_call(
        _rw_kernel,
        grid=(m // t_tile,),
        in_specs=[pl.BlockSpec((t_tile, RW_PAD), lambda t: (t, 0))] + [full(a) for a in params],
        out_specs=pl.BlockSpec((t_tile, WIDTH), lambda t: (t, 0)),
        out_shape=jax.ShapeDtypeStruct((m, WIDTH), BF16),
        scratch_shapes=[pltpu.VMEM((N_PAIRS, LANES, LANES), F32), pltpu.VMEM((1, RW_PAD), F32)],
        compiler_params=_params(("arbitrary",)),
    )(p_rw, *params)


def _fox_prep_kernel(p_ref, qn_ref, kn_ref, bf_ref, part_ref, sel_ref,
                     q_out, k_out, vt_out, qa_out, ka_out, bound_out, carry_ref, kmax_ref, bmax_ref):
    tm = p_ref.shape[0]

    @pl.when(pl.program_id(0) == 0)
    def _():
        carry_ref[...] = jnp.zeros_like(carry_ref)
        kmax_ref[...] = jnp.zeros_like(kmax_ref)
        bmax_ref[...] = jnp.zeros_like(bmax_ref)

    w = WIDTH
    ones = _head_ones()
    q, k = p_ref[:, 0:w], p_ref[:, w:2 * w]
    q = q * lax.rsqrt(_headsum(q * q, ones) * (1.0 / HEAD_DIM) + NORM_EPS) * qn_ref[...]
    k = k * lax.rsqrt(_headsum(k * k, ones) * (1.0 / HEAD_DIM) + NORM_EPS) * kn_ref[...]
    qb = (q * (HEAD_DIM ** -0.5 * LOG2E)).astype(BF16)
    kb = k.astype(BF16)
    q_out[...] = qb
    k_out[...] = kb
    vt_out[...] = p_ref[:, 2 * w:3 * w].T.astype(vt_out.dtype)

    qf, kf = qb.astype(F32), kb.astype(F32)
    k_norm = jnp.sqrt(_headsum(kf * kf, ones, 1))
    kmax = jnp.maximum(kmax_ref[...], jnp.max(k_norm, axis=0, keepdims=True))
    kmax_ref[...] = kmax
    bound = jnp.sqrt(_headsum(qf * qf, ones, 1)) * kmax * BOUND_SLACK
    bsel = _mm(_bf(bound), [sel_ref[...]])

    logf = -_softplus(-(p_ref[:, 3 * w:3 * w + LANES] + bf_ref[...]))
    ri = lax.broadcasted_iota(jnp.int32, (tm, tm), 0)
    ci = lax.broadcasted_iota(jnp.int32, (tm, tm), 1)
    cum = _mm(_bf(ri >= ci), _split(logf, 3)) + carry_ref[...]
    carry_ref[...] = cum[tm - 1:tm, :]

    part = jnp.broadcast_to(part_ref[...], (tm, LANES))

    def pick(x):
        parts = [t.astype(F32) for t in _split(x, BIAS_PARTS)]
        return jnp.where(part == 0, parts[0], jnp.where(part == 1, parts[1], parts[2]))

    lane = lax.broadcasted_iota(jnp.int32, (tm, LANES), 1)
    low = lane < BIAS_LANES
    high = jnp.logical_and(lane >= BIAS_LANES, lane < 2 * BIAS_LANES)
    ka_out[...] = jnp.where(low, pick(cum * (-LOG2E)), jnp.where(high, 1.0, 0.0)).astype(ka_out.dtype)
    qa_out[...] = jnp.where(low, 1.0, jnp.where(high, pick(cum * LOG2E - bsel), 0.0)).astype(qa_out.dtype)
    bmax = jnp.maximum(bmax_ref[...], jnp.max(bsel, axis=0, keepdims=True))
    bmax_ref[...] = bmax
    bound_out[...] = jnp.broadcast_to(bmax, bound_out.shape)


def _fox_prep(p_fox, q_norm, k_norm, bf_rep, tm):
    m = p_fox.shape[0]
    tm = min(tm, m)
    qn = jnp.tile(q_norm, N_HEADS).reshape(1, WIDTH)
    kn = jnp.tile(k_norm, N_HEADS).reshape(1, WIDTH)
    part = (jnp.arange(LANES, dtype=jnp.int32) % BIAS_PARTS).reshape(1, LANES)
    heads = np.arange(N_HEADS)
    sel = np.zeros((WIDTH, LANES), np.float32)
    for j in range(BIAS_PARTS):
        sel[heads * HEAD_DIM, BIAS_LANES + BIAS_PARTS * heads + j] = 1.0
    sel = jnp.asarray(sel, BF16)
    full = lambda a: pl.BlockSpec(a.shape, lambda i: (0, 0))
    rows = lambda width: pl.BlockSpec((tm, width), lambda i: (i, 0))
    act = jax.ShapeDtypeStruct((m, WIDTH), BF16)
    aug = jax.ShapeDtypeStruct((m, LANES), BF16)
    return pl.pallas_call(
        _fox_prep_kernel,
        grid=(m // tm,),
        in_specs=[rows(FOX_PAD), full(qn), full(kn), full(bf_rep), full(part), full(sel)],
        out_specs=[rows(WIDTH), rows(WIDTH), pl.BlockSpec((WIDTH, tm), lambda i: (0, i)), rows(LANES), rows(LANES),
                   pl.BlockSpec((8, LANES), lambda i: (0, 0))],
        out_shape=[act, act, jax.ShapeDtypeStruct((WIDTH, m), BF16), aug, aug, jax.ShapeDtypeStruct((8, LANES), F32)],
        scratch_shapes=[pltpu.VMEM((1, LANES), F32), pltpu.VMEM((1, WIDTH), F32), pltpu.VMEM((1, LANES), F32)],
        compiler_params=_params(("arbitrary",)),
    )(p_fox, qn, kn, bf_rep, part, sel)


def _fox_attn_kernel(qi_ref, kb_ref, kind_ref, q_ref, qa_ref, k_ref, ka_ref, vt_ref, o_ref, m_ref, acc_ref, *,
                     running_max):
    tq = q_ref.shape[0]
    pair, step_id = pl.program_id(0), pl.program_id(1)
    kb, kind = kb_ref[step_id], kind_ref[step_id]

    @pl.when(kb == 0)
    def _():
        m_ref[...] = jnp.full_like(m_ref, -jnp.inf)
        acc_ref[...] = jnp.zeros_like(acc_ref)

    def step(kinds):
        q, qa = q_ref[...], qa_ref[...]
        lane_q = lax.broadcasted_iota(jnp.int32, (tq, LANES), 1)
        q_full = []
        for h in range(2):
            hmask = (lane_q < HEAD_DIM) if h == 0 else (lane_q >= HEAD_DIM)
            first = (2 * pair + h) * BIAS_PARTS
            in_group = lambda lo: jnp.logical_and(lane_q >= lo, lane_q < lo + BIAS_PARTS)
            aug = jnp.logical_or(in_group(first), in_group(first + BIAS_LANES))
            q_full.append(jnp.concatenate([jnp.where(hmask, q, jnp.zeros_like(q)),
                                           jnp.where(aug, qa, jnp.zeros_like(qa))], axis=1))
        subs = [slice(b * tq, (b + 1) * tq) for b in range(len(kinds))]
        k_full = [jnp.concatenate([k_ref[sl, :], ka_ref[sl, :]], axis=1) for sl in subs]
        ones_rows = jnp.ones((ONES_ROWS, tq), BF16)
        v_aug = [jnp.concatenate([vt_ref[:, sl], ones_rows], axis=0) for sl in subs]
        scores = lambda b: [_mm([k_full[b]], [q_full[h]], "nt") for h in range(2)]
        s = [scores(b) for b in range(min(AHEAD, len(kinds)))]
        causal = lax.broadcasted_iota(jnp.int32, (tq, tq), 0) <= lax.broadcasted_iota(jnp.int32, (tq, tq), 1)
        m = [m_ref[h] for h in range(2)]
        acc = [acc_ref[h] for h in range(2)]
        for b, diagonal in enumerate(kinds):
            if b + AHEAD < len(kinds):
                s.append(scores(b + AHEAD))
            sb = [jnp.where(causal, x, -jnp.inf) for x in s[b]] if diagonal else s[b]
            if running_max:
                m_new = [jnp.maximum(m[h], jnp.max(sb[h], axis=0, keepdims=True)) for h in range(2)]
                p = [jnp.exp2(sb[h] - m_new[h]).astype(BF16) for h in range(2)]
                acc = [jnp.exp2(m[h] - m_new[h]) * acc[h] for h in range(2)]
                m = m_new
            else:
                p = [jnp.exp2(sb[h]).astype(BF16) for h in range(2)]
            pv = [_mm([v_aug[b]], [p[h]]) for h in range(2)]
            acc = [acc[h] + pv[h] for h in range(2)]
        for h in range(2):
            acc_ref[h] = acc[h]
            if running_max:
                m_ref[h] = m[h]

    def finish():
        row = lax.broadcasted_iota(jnp.int32, (LANES, tq), 0)
        o = [acc_ref[h, 0:LANES, :] / acc_ref[h, LANES:LANES + 1, :] for h in range(2)]
        o_ref[...] = jnp.where(row < HEAD_DIM, o[0], o[1]).T.astype(o_ref.dtype)

    @pl.when(kind == 0)
    def _():
        step((False,) * KEY_BLOCKS)

    for n_full in range(KEY_BLOCKS):
        @pl.when(kind == 1 + n_full)
        def _(n_full=n_full):
            step((False,) * n_full + (True,))
            finish()


def _fox_attn(q, qa, k, ka, vt, tq, running_max):
    m = q.shape[0]
    tq = min(tq, m)
    nq = m // tq
    g = KEY_BLOCKS
    assert nq % g == 0, "key blocks are fetched in groups"
    steps = []
    for i in range(nq):
        steps += [(i, j, 0) for j in range(i // g)] + [(i, i // g, 1 + i % g)]
    qi_tab, kb_tab, kind_tab = (jnp.asarray([st[c] for st in steps], jnp.int32) for c in range(3))
    qspec = pl.BlockSpec((tq, LANES), lambda p, t, qi, kb, kind: (qi[t], p))
    qaspec = pl.BlockSpec((tq, LANES), lambda p, t, qi, kb, kind: (qi[t], 0))
    kspec = pl.BlockSpec((g * tq, LANES), lambda p, t, qi, kb, kind: (kb[t], p))
    kaspec = pl.BlockSpec((g * tq, LANES), lambda p, t, qi, kb, kind: (kb[t], 0))
    vtspec = pl.BlockSpec((LANES, g * tq), lambda p, t, qi, kb, kind: (p, kb[t]))
    return pl.pallas_call(
        functools.partial(_fox_attn_kernel, running_max=running_max),
        grid_spec=pltpu.PrefetchScalarGridSpec(
            num_scalar_prefetch=3,
            grid=(N_PAIRS, len(steps)),
            in_specs=[qspec, qaspec, kspec, kaspec, vtspec],
            out_specs=qspec,
            scratch_shapes=[pltpu.VMEM((2, 1, tq), F32), pltpu.VMEM((2, LANES + ONES_ROWS, tq), F32)],
        ),
        out_shape=jax.ShapeDtypeStruct((m, WIDTH), BF16),
        compiler_params=_params(("parallel", "arbitrary")),
    )(qi_tab, kb_tab, kind_tab, q, qa, k, ka, vt)


def _mem_kv_kernel(mem_ref, g_ref, w_ref, o_ref):
    x = mem_ref[...]
    ms = jnp.mean(x * x, axis=-1, keepdims=True)
    xn = (x * lax.rsqrt(ms + NORM_EPS) * g_ref[...]).astype(BF16)
    o_ref[...] = _dot(xn, w_ref[...]).astype(o_ref.dtype)


def _mem_kv(mem, ln_mem, w_ckv):
    n, d = mem.shape
    ins = [mem, ln_mem.reshape(1, d), w_ckv]
    return pl.pallas_call(
        _mem_kv_kernel,
        grid=(1,),
        in_specs=[pl.BlockSpec(a.shape, lambda i: (0, 0)) for a in ins],
        out_specs=pl.BlockSpec((n, w_ckv.shape[1]), lambda i: (0, 0)),
        out_shape=jax.ShapeDtypeStruct((n, w_ckv.shape[1]), BF16),
        compiler_params=_params(("arbitrary",)),
    )(*ins)


def _post_kernel(ya_ref, yb_ref, pg_ref, x_ref, wa_ref, wb_ref, wo_ref, lnc_ref, wcq_ref, kv_ref, wco_ref, o_ref):
    d = x_ref.shape[1]
    xw = X_HEADS * X_HEAD_DIM
    ga = pg_ref[:, 0:d].astype(F32)
    gb = pg_ref[:, d:2 * d].astype(F32)
    merged = ga * _dot(ya_ref[...], wa_ref[...]) + gb * _dot(yb_ref[...], wb_ref[...])
    h = x_ref[...] + _dot(merged.astype(BF16), wo_ref[...])

    ms = jnp.mean(h * h, axis=-1, keepdims=True)
    hn = (h * lax.rsqrt(ms + NORM_EPS) * lnc_ref[...]).astype(BF16)
    q = _dot(hn, wcq_ref[...])
    outs = []
    for hd in range(X_HEADS):
        cols = slice(hd * X_HEAD_DIM, (hd + 1) * X_HEAD_DIM)
        vcols = slice(xw + hd * X_HEAD_DIM, xw + (hd + 1) * X_HEAD_DIM)
        s = _dot_t(q[:, cols].astype(BF16), kv_ref[:, cols]) * (X_HEAD_DIM ** -0.5)
        s = s - jnp.max(s, axis=1, keepdims=True)
        e = jnp.exp(s)
        pr = e / jnp.sum(e, axis=1, keepdims=True)
        outs.append(_dot(pr.astype(BF16), kv_ref[:, vcols]))
    o = jnp.concatenate(outs, axis=1).astype(BF16)
    o_ref[...] = h + _dot(o, wco_ref[...])


def _post(ya, yb, pg, x, wa, wb, wo, ln_cross, wcq, kv, wco, tm):
    m, d = x.shape
    tm = min(tm, m)
    lnc = ln_cross.reshape(1, d)
    rows = lambda a: pl.BlockSpec((tm, a.shape[1]), lambda i: (i, 0))
    const = lambda a: pl.BlockSpec(a.shape, lambda i: (0, 0), pipeline_mode=pl.Buffered(1))
    return pl.pallas_call(
        _post_kernel,
        grid=(m // tm,),
        in_specs=[rows(ya), rows(yb), rows(pg), rows(x)] + [const(a) for a in (wa, wb, wo, lnc, wcq, kv, wco)],
        out_specs=pl.BlockSpec((tm, d), lambda i: (i, 0)),
        out_shape=jax.ShapeDtypeStruct((m, d), F32),
        compiler_params=_params(("parallel",)),
    )(ya, yb, pg, x, wa, wb, wo, lnc, wcq, kv, wco)


def _mlp_kernel(h_ref, g_ref, wu_ref, wd_ref, gf_ref, o_ref, xn_ref):
    f = pl.program_id(1)

    @pl.when(f == 0)
    def _():
        h = h_ref[...]
        ms = jnp.mean(h * h, axis=-1, keepdims=True)
        xn_ref[...] = (h * lax.rsqrt(ms + NORM_EPS) * g_ref[...]).astype(xn_ref.dtype)
        o_ref[...] = jnp.zeros_like(o_ref)

    u = jnp.maximum(_dot(xn_ref[...], wu_ref[...]), 0.0)
    o_ref[...] += _dot((u * u).astype(BF16), wd_ref[...])

    @pl.when(f == pl.num_programs(1) - 1)
    def _():
        h = h_ref[...] + o_ref[...]
        ms = jnp.mean(h * h, axis=-1, keepdims=True)
        o_ref[...] = h * lax.rsqrt(ms + NORM_EPS) * gf_ref[...]


def _mlp(h, ln_mlp, w_up, w_down, ln_final, tm, tf):
    m, d = h.shape
    dff = w_up.shape[1]
    tm, tf = min(tm, m), min(tf, dff)
    return pl.pallas_call(
        _mlp_kernel,
        grid=(m // tm, dff // tf),
        in_specs=[pl.BlockSpec((tm, d), lambda i, f: (i, 0)),
                  pl.BlockSpec((1, d), lambda i, f: (0, 0)),
                  pl.BlockSpec((d, tf), lambda i, f: (0, f)),
                  pl.BlockSpec((tf, d), lambda i, f: (f, 0)),
                  pl.BlockSpec((1, d), lambda i, f: (0, 0))],
        out_specs=pl.BlockSpec((tm, d), lambda i, f: (i, 0)),
        out_shape=jax.ShapeDtypeStruct((m, d), F32),
        scratch_shapes=[pltpu.VMEM((tm, d), BF16)],
        compiler_params=_params(("parallel", "arbitrary")),
    )(h, ln_mlp.reshape(1, d), w_up, w_down, ln_final.reshape(1, d))


def _pad_rows(a, height):
    return jnp.pad(a, ((0, height - a.shape[0]), (0, 0)))


def _layer(h, mem, ln_mix, w_in, rw_mu, rw_w0, rw_w_up, rw_a0, rw_a_up, rw_g_up, rw_k_k, rw_k_a, rw_r_k,
           rw_ln_w, rw_ln_b, fox_b_f, fox_q_norm, fox_k_norm, w_proj_a, w_proj_b, w_out, ln_cross, ln_mem,
           w_cq, w_ckv, w_co, ln_mlp, w_up, w_down, ln_final):
    w_rw, w_fox, w_gate = _proj_weights(w_in, 256)
    mu_p = jnp.pad(rw_mu, (0, RW_PAD - RW_COLS))
    bf_rep = jnp.pad(jnp.tile(jnp.repeat(fox_b_f, BIAS_PARTS), 2), (0, LANES - 2 * BIAS_LANES)).reshape(1, LANES)
    wup_p = _pad_rows(rw_w_up, LORA_PAD).astype(BF16)
    aup_p = jnp.pad(rw_a_up, ((DECAY_LORA, 0), (0, 0))).astype(BF16)
    gup_p = _pad_rows(rw_g_up, GATE_PAD).astype(BF16)

    xn = _rmsnorm_bf16(h, ln_mix, 512)
    p_rw = _matmul(xn, w_rw, 1024, 1792)
    p_fox = _matmul(xn, w_fox, 1024, 1792)
    gates = _matmul(xn, w_gate, 1024, 2048, BF16, gate=True)

    y_a = _rw_branch(p_rw, mu_p, rw_w0, wup_p, rw_a0, aup_p, gup_p, rw_k_k, rw_k_a, rw_r_k.reshape(-1), rw_ln_w,
                     rw_ln_b, 2 * CHUNK)

    fq, fk, fvt, fqa, fka, fbound = _fox_prep(p_fox, fox_q_norm, fox_k_norm, bf_rep, 512)
    y_b = lax.cond(jnp.max(fbound) < SAFE_LOGIT_BOUND,
                   functools.partial(_fox_attn, tq=512, running_max=False),
                   functools.partial(_fox_attn, tq=512, running_max=True), fq, fqa, fk, fka, fvt)

    kv = _mem_kv(mem, ln_mem, w_ckv.astype(BF16))
    h = _post(y_a, y_b, gates, h, w_proj_a.astype(BF16), w_proj_b.astype(BF16), w_out.astype(BF16), ln_cross,
              w_cq.astype(BF16), kv, w_co.astype(BF16), 512)
    return _mlp(h, ln_mlp, w_up.astype(BF16), w_down.astype(BF16), ln_final, 512, 1024)


def kernel(x, mem, ln_mix, w_in, rw_mu, rw_w0, rw_w_up, rw_a0, rw_a_up, rw_g_up, rw_k_k, rw_k_a, rw_r_k, rw_ln_w, rw_ln_b, fox_b_f, fox_q_norm, fox_k_norm, w_proj_a, w_proj_b, w_out, ln_cross, ln_mem, w_cq, w_ckv, w_co, ln_mlp, w_up, w_down, ln_final):
    b, s, d = x.shape
    assert b == 1 and ln_mix.shape[0] == 1, "single sequence, single layer"
    out = _layer(x[0], mem[0], ln_mix[0], w_in[0], rw_mu[0], rw_w0[0], rw_w_up[0], rw_a0[0], rw_a_up[0], rw_g_up[0],
                 rw_k_k[0], rw_k_a[0], rw_r_k[0], rw_ln_w[0], rw_ln_b[0], fox_b_f[0], fox_q_norm[0], fox_k_norm[0],
                 w_proj_a[0], w_proj_b[0], w_out[0], ln_cross[0], ln_mem[0], w_cq[0], w_ckv[0], w_co[0],
                 ln_mlp[0], w_up[0], w_down[0], ln_final)
    return out.reshape(b, s, d)
```

```python
import functools

import numpy as np

import jax
import jax.numpy as jnp
from jax import lax
from jax.experimental import pallas as pl
from jax.experimental.pallas import tpu as pltpu

F32 = jnp.float32
BF16 = jnp.bfloat16

LANES = 128
MXU_COLS = 256
HEAD_DIM = 64
N_HEADS = 16
WIDTH = N_HEADS * HEAD_DIM
N_PAIRS = WIDTH // LANES
DECAY_LORA, ICLR_LORA, GATE_LORA = 64, 64, 160
LORA_PAD = 128
GATE_PAD = 256
RW_COLS = 3 * WIDTH + DECAY_LORA + ICLR_LORA + GATE_LORA
RW_PAD = 3584
FZ_PAD = 512
FOX_PAD = 3 * WIDTH + FZ_PAD
BOUND_SLACK = 1.02
SAFE_LOGIT_BOUND = 50.0
BIAS_PARTS = 3
BIAS_LANES = BIAS_PARTS * N_HEADS
LOG2E = 1.4426950408889634
KEY_BLOCKS = 4
AHEAD = 2
ONES_ROWS = 16
X_HEADS, X_HEAD_DIM = 4, 128
NORM_EPS = 1e-5
GN_EPS = 64e-5
CHUNK = 64
SUB = 16
VMEM_LIMIT = 56 * 1024 * 1024


def _params(sem, vmem=VMEM_LIMIT):
    return pltpu.CompilerParams(dimension_semantics=sem, vmem_limit_bytes=vmem)


def _dot(a, b):
    return jnp.dot(a, b, preferred_element_type=F32)


def _dot_t(a, b):
    return lax.dot_general(a, b, (((1,), (1,)), ((), ())), preferred_element_type=F32)


def _split(x, terms):
    parts, rem = [], x
    for i in range(terms):
        part = rem.astype(BF16)
        parts.append(part)
        if i + 1 < terms:
            rem = rem - part.astype(F32)
    return parts


_DIMS = {"nn": (((1,), (0,)), ((), ())), "nt": (((1,), (1,)), ((), ())), "tn": (((0,), (0,)), ((), ()))}


def _mm(a_parts, b_parts, kind="nn"):
    order = max(len(a_parts), len(b_parts))
    acc = None
    for i, a in enumerate(a_parts):
        for j, b in enumerate(b_parts):
            if i + j < order:
                term = lax.dot_general(a, b, _DIMS[kind], preferred_element_type=F32)
                acc = term if acc is None else acc + term
    return acc


def _bf(x):
    return [x.astype(BF16)]


def _softplus(x):
    return jnp.maximum(x, 0.0) + jnp.log(1.0 + jnp.exp(-jnp.abs(x)))


def _sigmoid(x):
    return 1.0 / (1.0 + jnp.exp(-x))


def _head_ones():
    r = lax.broadcasted_iota(jnp.int32, (LANES, LANES), 0) // HEAD_DIM
    c = lax.broadcasted_iota(jnp.int32, (LANES, LANES), 1) // HEAD_DIM
    return (r == c).astype(F32)


def _stacked_headsum(xs, ones, terms):
    n = xs[0].shape[0]
    out = _mm(_split(jnp.concatenate(xs, axis=0), terms), ones)
    return [out[i * n:(i + 1) * n] for i in range(len(xs))]


def _headsum(x, ones_f32, terms=2):
    chunks = [x[:, c * LANES:(c + 1) * LANES] for c in range(x.shape[1] // LANES)]
    return jnp.concatenate(_stacked_headsum(chunks, [ones_f32.astype(BF16)], terms), axis=1)


def _rmsnorm_kernel(x_ref, g_ref, o_ref):
    x = x_ref[...]
    ms = jnp.mean(x * x, axis=-1, keepdims=True)
    o_ref[...] = (x * lax.rsqrt(ms + NORM_EPS) * g_ref[...]).astype(o_ref.dtype)


def _rmsnorm_bf16(x, g, tm):
    m, d = x.shape
    tm = min(tm, m)
    return pl.pallas_call(
        _rmsnorm_kernel,
        grid=(m // tm,),
        in_specs=[pl.BlockSpec((tm, d), lambda i: (i, 0)), pl.BlockSpec((1, d), lambda i: (0, 0))],
        out_specs=pl.BlockSpec((tm, d), lambda i: (i, 0)),
        out_shape=jax.ShapeDtypeStruct((m, d), BF16),
        compiler_params=_params(("parallel",)),
    )(x, g.reshape(1, d))


def _proj_kernel(a_ref, wt_ref, o_ref, w_bf16, *, gate):
    @pl.when(pl.program_id(1) == 0)
    def _():
        for c0 in range(0, wt_ref.shape[0], MXU_COLS):
            w_bf16[:, c0:c0 + MXU_COLS] = wt_ref[c0:c0 + MXU_COLS, :].T.astype(BF16)

    acc = _dot(a_ref[...], w_bf16[...])
    o_ref[...] = (_sigmoid(acc) if gate else acc).astype(o_ref.dtype)


def _project(a, w_t, row0, n, bm, bn, out_dtype=F32, gate=False):
    m, k = a.shape
    bm = min(bm, m)
    assert row0 % 8 == 0 and n % bn == 0 and bn % MXU_COLS == 0 and m % bm == 0 and row0 + n <= w_t.shape[0]
    return pl.pallas_call(
        functools.partial(_proj_kernel, gate=gate),
        grid=(n // bn, m // bm),
        in_specs=[pl.BlockSpec((bm, k), lambda j, i: (i, 0)),
                  pl.BlockSpec((pl.Element(bn), pl.Element(k)), lambda j, i: (pl.multiple_of(row0 + j * bn, 8), 0),
                               pipeline_mode=pl.Buffered(1))],
        out_specs=pl.BlockSpec((bm, bn), lambda j, i: (i, j)),
        out_shape=jax.ShapeDtypeStruct((m, n), out_dtype),
        scratch_shapes=[pltpu.VMEM((k, bn), BF16)],
        compiler_params=_params(("arbitrary", "arbitrary")),
    )(a, w_t)


def _rw_inputs(p, prev_row, mu, w0, wup, a0, aup, gup, k_k, k_a, ones):
    rows = lax.broadcasted_iota(jnp.int32, p.shape, 0)
    prev = jnp.where(rows == 0, prev_row, pltpu.roll(p, 1, 0))
    ps = p + (prev - p) * mu
    w = WIDTH
    r, k, v = ps[:, 0:w], ps[:, w:2 * w], ps[:, 2 * w:3 * w]
    o = 3 * w
    lora = ps[:, o:o + LORA_PAD]
    gd = ps[:, o + LORA_PAD:o + LORA_PAD + GATE_PAD]
    z = w0 + _mm(_bf(jnp.tanh(lora)), [wup])
    lw = -jnp.exp(-_softplus(-z) - 0.5)
    iclr = _sigmoid(a0 + _mm(_bf(lora), [aup]))
    gate = _mm(_bf(_sigmoid(gd)), [gup])
    kk = k * k_k
    kk2 = kk * kk
    ss = _stacked_headsum([kk2[:, c * LANES:(c + 1) * LANES] for c in range(N_PAIRS)], ones, 2)
    kk = kk * lax.rsqrt(jnp.maximum(jnp.concatenate(ss, axis=1), 1e-24))
    return r, k * (1.0 + (iclr - 1.0) * k_a), v, lw, -kk, kk * iclr, gate


def _pair_diag(x, head0):
    xb = x.astype(BF16)
    zero = jnp.zeros_like(xb)
    return [jnp.concatenate([jnp.where(head0, xb, zero), jnp.where(head0, zero, xb)], axis=0)]


def _unit_lower_inverse(a_list, sub_mask, eye, head0):
    c = CHUNK
    d = [jnp.where(sub_mask, a, 0.0) for a in a_list]
    e = [a - x for a, x in zip(a_list, d)]
    p = [eye + x for x in d]
    diag = lambda xs: [_pair_diag(x, head0) for x in xs]
    stack = lambda xs, ys: [_bf(jnp.concatenate([x, y], axis=0)) for x, y in zip(xs, ys)]
    dp = [_mm(_bf(x), y) for x, y in zip(d, diag(d))]
    for _ in range(SUB.bit_length() - 3):
        both = [_mm(x, y) for x, y in zip(stack(dp, p), diag(dp))]
        dp = [x[:c] for x in both]
        p = [x + y[c:] for x, y in zip(p, both)]
    p = [x + _mm(_bf(x), y) for x, y in zip(p, diag(dp))]
    f = [_mm(_bf(x), y) for x, y in zip(p, diag(e))]
    g = [eye + x for x in f]
    fp = [_mm(_bf(x), y) for x, y in zip(f, diag(f))]
    for _ in range((CHUNK // SUB).bit_length() - 3):
        both = [_mm(x, y) for x, y in zip(stack(fp, g), diag(fp))]
        fp = [x[:c] for x in both]
        g = [x + y[c:] for x, y in zip(g, both)]
    g = [x + _mm(_bf(x), y) for x, y in zip(g, diag(fp))]
    return [_mm(_bf(x), y) for x, y in zip(g, diag(p))]


def _rw_kernel(p_ref, mu_ref, w0_ref, wup_ref, a0_ref, aup_ref, gup_ref, kkw_ref, ka_ref, rk_ref, lnw_ref, lnb_ref,
               o_ref, h_ref, carry_ref):
    c = CHUNK
    t_tile = p_ref.shape[0]
    n_chunks = t_tile // c

    @pl.when(pl.program_id(0) == 0)
    def _():
        h_ref[...] = jnp.zeros_like(h_ref)
        carry_ref[...] = jnp.zeros_like(carry_ref)

    lane = lax.broadcasted_iota(jnp.int32, (c, LANES), 1)
    head0 = lane < HEAD_DIM
    ri = lax.broadcasted_iota(jnp.int32, (c, LANES), 0)
    ci = jnp.bitwise_and(lane, HEAD_DIM - 1)
    strict = ri > ci
    incl = ri >= ci
    eye = (ri == ci).astype(F32)
    sub_mask = (ri // SUB) == (ci // SUB)
    tril = _bf(lax.broadcasted_iota(jnp.int32, (c, c), 0) >= lax.broadcasted_iota(jnp.int32, (c, c), 1))
    ones_f = _head_ones()
    ones = _bf(ones_f)
    pair_mask = ones_f > 0.5
    diag = lambda x: _pair_diag(x, head0)
    units = [(s, p) for s in range(n_chunks) for p in range(N_PAIRS)]
    rows = lambda s: slice(s * c, (s + 1) * c)
    cols = lambda p: slice(p * LANES, (p + 1) * LANES)

    p_tile = p_ref[...]
    r_all, k_all, v_all, lw_all, a_all, b_all, gate_all = _rw_inputs(
        p_tile, carry_ref[...], mu_ref[...], w0_ref[...], wup_ref[...], a0_ref[...], aup_ref[...], gup_ref[...],
        kkw_ref[...], ka_ref[...], ones)
    carry_ref[...] = p_tile[t_tile - 1:t_tile, :]

    pre = []
    for s in range(n_chunks):
        lw, k, b = lw_all[rows(s), :], k_all[rows(s), :], b_all[rows(s), :]
        cl = _mm(tril, _split(lw, 3))
        cl_end = cl[c - 1:c, :]
        e_neg = jnp.exp(-cl)
        e_end = jnp.exp(cl_end - cl)
        pre.append(dict(rt=r_all[rows(s), :] * jnp.exp(cl), at=a_all[rows(s), :] * jnp.exp(cl - lw),
                        bt=b * e_neg, kt=k * e_neg, b_end=b * e_end, k_end=k * e_end, decay_end=jnp.exp(cl_end)))
    get = lambda name: [pre[s][name][:, cols(p)] for s, p in units]
    tile = lambda x: [x[rows(s), cols(p)] for s, p in units]
    r, k, v, gate = tile(r_all), tile(k_all), tile(v_all), tile(gate_all)
    rt, at = get("rt"), get("at")
    stack = lambda xs, ys: [jnp.concatenate([x, y], axis=0) for x, y in zip(xs, ys)]
    bk_end = [_bf(x) for x in stack(get("b_end"), get("k_end"))]
    v_diag = [diag(x) for x in v]
    mask2 = jnp.concatenate([strict, incl], axis=0)

    lhs = [_bf(x) for x in stack(at, rt)]
    sb = [_mm(x, diag(y), "nt") for x, y in zip(lhs, get("bt"))]
    sk = [_mm(x, diag(y), "nt") for x, y in zip(lhs, get("kt"))]
    skv = [_mm(_bf(jnp.where(mask2, x, 0.0)), y) for x, y in zip(sk, v_diag)]
    rb = [_bf(jnp.where(incl, x[c:], 0.0)) for x in sb]
    t_inv = [_bf(x) for x in _unit_lower_inverse([jnp.where(strict, x[:c], 0.0) for x in sb], sub_mask, eye, head0)]
    w = [_mm(x, diag(y)) for x, y in zip(t_inv, at)]
    tk = [_mm(x, diag(y[:c])) for x, y in zip(t_inv, skv)]
    s0_lhs = [_bf(x) for x in stack(w, rt)]

    ht = [h_ref[p] for p in range(N_PAIRS)]
    u, rs0 = [], []
    for s in range(n_chunks):
        i0 = s * N_PAIRS
        s0 = [_mm(s0_lhs[i0 + p], _bf(ht[p]), "nt") for p in range(N_PAIRS)]
        u += [s0[p][:c] + tk[i0 + p] for p in range(N_PAIRS)]
        rs0 += [s0[p][c:] for p in range(N_PAIRS)]
        upd = [_mm(_bf(jnp.concatenate([u[i0 + p], v[i0 + p]], axis=0)), bk_end[i0 + p], "tn") for p in range(N_PAIRS)]
        ht = [ht[p] * pre[s]["decay_end"][:, cols(p)] + jnp.where(pair_mask, upd[p], 0.0) for p in range(N_PAIRS)]
    for p in range(N_PAIRS):
        h_ref[p] = ht[p]

    y = [a + _mm(x, diag(z)) + b[c:] for a, x, z, b in zip(rs0, rb, u, skv)]
    rkr = [r[i] * k[i] * rk_ref[:, cols(p)] for i, (s, p) in enumerate(units)]
    sums = _stacked_headsum(y + rkr, ones, 2)
    dlt = [a - b * (1.0 / HEAD_DIM) for a, b in zip(y, sums[:len(units)])]
    var = _stacked_headsum([x * x for x in dlt], ones, 2)
    for i, (s, p) in enumerate(units):
        yn = dlt[i] * lax.rsqrt(var[i] * (1.0 / HEAD_DIM) + GN_EPS) * lnw_ref[:, cols(p)] + lnb_ref[:, cols(p)]
        o_ref[rows(s), cols(p)] = ((yn + sums[len(units) + i] * v[i]) * gate[i]).astype(o_ref.dtype)


def _rw_branch(p_rw, mu_p, w0, wup_p, a0, aup_p, gup_p, k_k, k_a, r_k, ln_w, ln_b, t_tile):
    m = p_rw.shape[0]
    t_tile = min(t_tile, m)
    row = lambda a: a.reshape(1, -1)
    full = lambda a: pl.BlockSpec(a.shape, lambda t: (0, 0))
    params = [row(mu_p), row(w0), wup_p, row(a0), aup_p, gup_p, row(k_k), row(k_a), row(r_k), row(ln_w), row(ln_b)]
    return pl.pallas_call(
        _rw_kernel,
        grid=(m // t_tile,),
        in_specs=[pl.BlockSpec((t_tile, RW_PAD), lambda t: (t, 0))] + [full(a) for a in params],
        out_specs=pl.BlockSpec((t_tile, WIDTH), lambda t: (t, 0)),
        out_shape=jax.ShapeDtypeStruct((m, WIDTH), BF16),
        scratch_shapes=[pltpu.VMEM((N_PAIRS, LANES, LANES), F32), pltpu.VMEM((1, RW_PAD), F32)],
        compiler_params=_params(("arbitrary",)),
    )(p_rw, *params)


def _fox_prep_kernel(p_ref, qn_ref, kn_ref, bf_ref, part_ref, sel_ref, selz_ref,
                     q_out, k_out, vt_out, qa_out, ka_out, bound_out, carry_ref, kmax_ref, bmax_ref):
    tm = p_ref.shape[0]

    @pl.when(pl.program_id(0) == 0)
    def _():
        carry_ref[...] = jnp.zeros_like(carry_ref)
        kmax_ref[...] = jnp.zeros_like(kmax_ref)
        bmax_ref[...] = jnp.zeros_like(bmax_ref)

    w = WIDTH
    ones = _head_ones()
    q, k = p_ref[:, 0:w], p_ref[:, w:2 * w]
    q = q * lax.rsqrt(_headsum(q * q, ones) * (1.0 / HEAD_DIM) + NORM_EPS) * qn_ref[...]
    k = k * lax.rsqrt(_headsum(k * k, ones) * (1.0 / HEAD_DIM) + NORM_EPS) * kn_ref[...]
    qb = (q * (HEAD_DIM ** -0.5 * LOG2E)).astype(BF16)
    kb = k.astype(BF16)
    q_out[...] = qb
    k_out[...] = kb
    vt_out[...] = p_ref[:, 2 * w:3 * w].T.astype(vt_out.dtype)

    qf, kf = qb.astype(F32), kb.astype(F32)
    k_norm = jnp.sqrt(_headsum(kf * kf, ones, 1))
    kmax = jnp.maximum(kmax_ref[...], jnp.max(k_norm, axis=0, keepdims=True))
    kmax_ref[...] = kmax
    bound = jnp.sqrt(_headsum(qf * qf, ones, 1)) * kmax * BOUND_SLACK
    bsel = _mm(_bf(bound), [sel_ref[...]])

    fz = _mm(_split(p_ref[:, 3 * w:3 * w + LANES], 3), [selz_ref[...]])
    logf = -_softplus(-(fz + bf_ref[...]))
    ri = lax.broadcasted_iota(jnp.int32, (tm, tm), 0)
    ci = lax.broadcasted_iota(jnp.int32, (tm, tm), 1)
    cum = _mm(_bf(ri >= ci), _split(logf, 3)) + carry_ref[...]
    carry_ref[...] = cum[tm - 1:tm, :]

    part = jnp.broadcast_to(part_ref[...], (tm, LANES))

    def pick(x):
        parts = [t.astype(F32) for t in _split(x, BIAS_PARTS)]
        return jnp.where(part == 0, parts[0], jnp.where(part == 1, parts[1], parts[2]))

    lane = lax.broadcasted_iota(jnp.int32, (tm, LANES), 1)
    low = lane < BIAS_LANES
    high = jnp.logical_and(lane >= BIAS_LANES, lane < 2 * BIAS_LANES)
    ka_out[...] = jnp.where(low, pick(cum * (-LOG2E)), jnp.where(high, 1.0, 0.0)).astype(ka_out.dtype)
    qa_out[...] = jnp.where(low, 1.0, jnp.where(high, pick(cum * LOG2E - bsel), 0.0)).astype(qa_out.dtype)
    bmax = jnp.maximum(bmax_ref[...], jnp.max(bsel, axis=0, keepdims=True))
    bmax_ref[...] = bmax
    bound_out[...] = jnp.broadcast_to(bmax, bound_out.shape)


def _fox_prep(p_fox, q_norm, k_norm, bf_rep, tm):
    m = p_fox.shape[0]
    tm = min(tm, m)
    qn = jnp.tile(q_norm, N_HEADS).reshape(1, WIDTH)
    kn = jnp.tile(k_norm, N_HEADS).reshape(1, WIDTH)
    part = (jnp.arange(LANES, dtype=jnp.int32) % BIAS_PARTS).reshape(1, LANES)
    heads = np.arange(N_HEADS)
    sel = np.zeros((WIDTH, LANES), np.float32)
    for j in range(BIAS_PARTS):
        sel[heads * HEAD_DIM, BIAS_LANES + BIAS_PARTS * heads + j] = 1.0
    sel = jnp.asarray(sel, BF16)
    selz = np.zeros((LANES, LANES), np.float32)
    for j in range(BIAS_PARTS):
        for group in (0, BIAS_LANES):
            selz[heads, group + BIAS_PARTS * heads + j] = 1.0
    selz = jnp.asarray(selz, BF16)
    full = lambda a: pl.BlockSpec(a.shape, lambda i: (0, 0))
    rows = lambda width: pl.BlockSpec((tm, width), lambda i: (i, 0))
    act = jax.ShapeDtypeStruct((m, WIDTH), BF16)
    aug = jax.ShapeDtypeStruct((m, LANES), BF16)
    return pl.pallas_call(
        _fox_prep_kernel,
        grid=(m // tm,),
        in_specs=[rows(FOX_PAD), full(qn), full(kn), full(bf_rep), full(part), full(sel), full(selz)],
        out_specs=[rows(WIDTH), rows(WIDTH), pl.BlockSpec((WIDTH, tm), lambda i: (0, i)), rows(LANES), rows(LANES),
                   pl.BlockSpec((8, LANES), lambda i: (0, 0))],
        out_shape=[act, act, jax.ShapeDtypeStruct((WIDTH, m), BF16), aug, aug, jax.ShapeDtypeStruct((8, LANES), F32)],
        scratch_shapes=[pltpu.VMEM((1, LANES), F32), pltpu.VMEM((1, WIDTH), F32), pltpu.VMEM((1, LANES), F32)],
        compiler_params=_params(("arbitrary",)),
    )(p_fox, qn, kn, bf_rep, part, sel, selz)


def _fox_attn_kernel(qi_ref, kb_ref, kind_ref, q_ref, qa_ref, k_ref, ka_ref, vt_ref, o_ref, m_ref, acc_ref, *,
                     running_max):
    tq = q_ref.shape[0]
    pair, step_id = pl.program_id(0), pl.program_id(1)
    kb, kind = kb_ref[step_id], kind_ref[step_id]

    @pl.when(kb == 0)
    def _():
        m_ref[...] = jnp.full_like(m_ref, -jnp.inf)
        acc_ref[...] = jnp.zeros_like(acc_ref)

    def step(kinds):
        q, qa = q_ref[...], qa_ref[...]
        lane_q = lax.broadcasted_iota(jnp.int32, (tq, LANES), 1)
        q_full = []
        for h in range(2):
            hmask = (lane_q < HEAD_DIM) if h == 0 else (lane_q >= HEAD_DIM)
            first = (2 * pair + h) * BIAS_PARTS
            in_group = lambda lo: jnp.logical_and(lane_q >= lo, lane_q < lo + BIAS_PARTS)
            aug = jnp.logical_or(in_group(first), in_group(first + BIAS_LANES))
            q_full.append(jnp.concatenate([jnp.where(hmask, q, jnp.zeros_like(q)),
                                           jnp.where(aug, qa, jnp.zeros_like(qa))], axis=1))
        subs = [slice(b * tq, (b + 1) * tq) for b in range(len(kinds))]
        k_full = [jnp.concatenate([k_ref[sl, :], ka_ref[sl, :]], axis=1) for sl in subs]
        ones_rows = jnp.ones((ONES_ROWS, tq), BF16)
        v_aug = [jnp.concatenate([vt_ref[:, sl], ones_rows], axis=0) for sl in subs]
        scores = lambda b: [_mm([k_full[b]], [q_full[h]], "nt") for h in range(2)]
        s = [scores(b) for b in range(min(AHEAD, len(kinds)))]
        causal = lax.broadcasted_iota(jnp.int32, (tq, tq), 0) <= lax.broadcasted_iota(jnp.int32, (tq, tq), 1)
        m = [m_ref[h] for h in range(2)]
        acc = [acc_ref[h] for h in range(2)]
        for b, diagonal in enumerate(kinds):
            if b + AHEAD < len(kinds):
                s.append(scores(b + AHEAD))
            sb = [jnp.where(causal, x, -jnp.inf) for x in s[b]] if diagonal else s[b]
            if running_max:
                m_new = [jnp.maximum(m[h], jnp.max(sb[h], axis=0, keepdims=True)) for h in range(2)]
                p = [jnp.exp2(sb[h] - m_new[h]).astype(BF16) for h in range(2)]
                acc = [jnp.exp2(m[h] - m_new[h]) * acc[h] for h in range(2)]
                m = m_new
            else:
                p = [jnp.exp2(sb[h]).astype(BF16) for h in range(2)]
            pv = [_mm([v_aug[b]], [p[h]]) for h in range(2)]
            acc = [acc[h] + pv[h] for h in range(2)]
        for h in range(2):
            acc_ref[h] = acc[h]
            if running_max:
                m_ref[h] = m[h]

    def finish():
        row = lax.broadcasted_iota(jnp.int32, (LANES, tq), 0)
        o = [acc_ref[h, 0:LANES, :] / acc_ref[h, LANES:LANES + 1, :] for h in range(2)]
        o_ref[...] = jnp.where(row < HEAD_DIM, o[0], o[1]).T.astype(o_ref.dtype)

    @pl.when(kind == 0)
    def _():
        step((False,) * KEY_BLOCKS)

    for n_full in range(KEY_BLOCKS):
        @pl.when(kind == 1 + n_full)
        def _(n_full=n_full):
            step((False,) * n_full + (True,))
            finish()


def _fox_attn(q, qa, k, ka, vt, tq, running_max):
    m = q.shape[0]
    tq = min(tq, m)
    nq = m // tq
    g = KEY_BLOCKS
    assert nq % g == 0, "key blocks are fetched in groups"
    steps = []
    for i in range(nq):
        steps += [(i, j, 0) for j in range(i // g)] + [(i, i // g, 1 + i % g)]
    qi_tab, kb_tab, kind_tab = (jnp.asarray([st[c] for st in steps], jnp.int32) for c in range(3))
    qspec = pl.BlockSpec((tq, LANES), lambda p, t, qi, kb, kind: (qi[t], p))
    qaspec = pl.BlockSpec((tq, LANES), lambda p, t, qi, kb, kind: (qi[t], 0))
    kspec = pl.BlockSpec((g * tq, LANES), lambda p, t, qi, kb, kind: (kb[t], p))
    kaspec = pl.BlockSpec((g * tq, LANES), lambda p, t, qi, kb, kind: (kb[t], 0))
    vtspec = pl.BlockSpec((LANES, g * tq), lambda p, t, qi, kb, kind: (p, kb[t]))
    return pl.pallas_call(
        functools.partial(_fox_attn_kernel, running_max=running_max),
        grid_spec=pltpu.PrefetchScalarGridSpec(
            num_scalar_prefetch=3,
            grid=(N_PAIRS, len(steps)),
            in_specs=[qspec, qaspec, kspec, kaspec, vtspec],
            out_specs=qspec,
            scratch_shapes=[pltpu.VMEM((2, 1, tq), F32), pltpu.VMEM((2, LANES + ONES_ROWS, tq), F32)],
        ),
        out_shape=jax.ShapeDtypeStruct((m, WIDTH), BF16),
        compiler_params=_params(("parallel", "arbitrary")),
    )(qi_tab, kb_tab, kind_tab, q, qa, k, ka, vt)


def _mem_kv_kernel(mem_ref, g_ref, w_ref, o_ref):
    x = mem_ref[...]
    ms = jnp.mean(x * x, axis=-1, keepdims=True)
    xn = (x * lax.rsqrt(ms + NORM_EPS) * g_ref[...]).astype(BF16)
    o_ref[...] = _dot(xn, w_ref[...]).astype(o_ref.dtype)


def _mem_kv(mem, ln_mem, w_ckv):
    n, d = mem.shape
    ins = [mem, ln_mem.reshape(1, d), w_ckv]
    return pl.pallas_call(
        _mem_kv_kernel,
        grid=(1,),
        in_specs=[pl.BlockSpec(a.shape, lambda i: (0, 0)) for a in ins],
        out_specs=pl.BlockSpec((n, w_ckv.shape[1]), lambda i: (0, 0)),
        out_shape=jax.ShapeDtypeStruct((n, w_ckv.shape[1]), BF16),
        compiler_params=_params(("arbitrary",)),
    )(*ins)


def _post_kernel(ya_ref, yb_ref, pg_ref, x_ref, wa_ref, wb_ref, wo_ref, lnc_ref, wcq_ref, kv_ref, wco_ref, o_ref):
    d = x_ref.shape[1]
    xw = X_HEADS * X_HEAD_DIM
    ga = pg_ref[:, 0:d].astype(F32)
    gb = pg_ref[:, d:2 * d].astype(F32)
    merged = ga * _dot(ya_ref[...], wa_ref[...]) + gb * _dot(yb_ref[...], wb_ref[...])
    h = x_ref[...] + _dot(merged.astype(BF16), wo_ref[...])

    ms = jnp.mean(h * h, axis=-1, keepdims=True)
    hn = (h * lax.rsqrt(ms + NORM_EPS) * lnc_ref[...]).astype(BF16)
    q = _dot(hn, wcq_ref[...])
    outs = []
    for hd in range(X_HEADS):
        cols = slice(hd * X_HEAD_DIM, (hd + 1) * X_HEAD_DIM)
        vcols = slice(xw + hd * X_HEAD_DIM, xw + (hd + 1) * X_HEAD_DIM)
        s = _dot_t(q[:, cols].astype(BF16), kv_ref[:, cols]) * (X_HEAD_DIM ** -0.5)
        s = s - jnp.max(s, axis=1, keepdims=True)
        e = jnp.exp(s)
        pr = e / jnp.sum(e, axis=1, keepdims=True)
        outs.append(_dot(pr.astype(BF16), kv_ref[:, vcols]))
    o = jnp.concatenate(outs, axis=1).astype(BF16)
    o_ref[...] = h + _dot(o, wco_ref[...])


def _post(ya, yb, pg, x, wa, wb, wo, ln_cross, wcq, kv, wco, tm):
    m, d = x.shape
    tm = min(tm, m)
    lnc = ln_cross.reshape(1, d)
    rows = lambda a: pl.BlockSpec((tm, a.shape[1]), lambda i: (i, 0))
    const = lambda a: pl.BlockSpec(a.shape, lambda i: (0, 0), pipeline_mode=pl.Buffered(1))
    return pl.pallas_call(
        _post_kernel,
        grid=(m // tm,),
        in_specs=[rows(ya), rows(yb), rows(pg), rows(x)] + [const(a) for a in (wa, wb, wo, lnc, wcq, kv, wco)],
        out_specs=pl.BlockSpec((tm, d), lambda i: (i, 0)),
        out_shape=jax.ShapeDtypeStruct((m, d), F32),
        compiler_params=_params(("parallel",)),
    )(ya, yb, pg, x, wa, wb, wo, lnc, wcq, kv, wco)


def _mlp_kernel(h_ref, g_ref, wu_ref, wd_ref, gf_ref, o_ref, xn_ref):
    f = pl.program_id(1)

    @pl.when(f == 0)
    def _():
        h = h_ref[...]
        ms = jnp.mean(h * h, axis=-1, keepdims=True)
        xn_ref[...] = (h * lax.rsqrt(ms + NORM_EPS) * g_ref[...]).astype(xn_ref.dtype)
        o_ref[...] = jnp.zeros_like(o_ref)

    u = jnp.maximum(_dot(xn_ref[...], wu_ref[...]), 0.0)
    o_ref[...] += _dot((u * u).astype(BF16), wd_ref[...])

    @pl.when(f == pl.num_programs(1) - 1)
    def _():
        h = h_ref[...] + o_ref[...]
        ms = jnp.mean(h * h, axis=-1, keepdims=True)
        o_ref[...] = h * lax.rsqrt(ms + NORM_EPS) * gf_ref[...]


def _mlp(h, ln_mlp, w_up, w_down, ln_final, tm, tf):
    m, d = h.shape
    dff = w_up.shape[1]
    tm, tf = min(tm, m), min(tf, dff)
    return pl.pallas_call(
        _mlp_kernel,
        grid=(m // tm, dff // tf),
        in_specs=[pl.BlockSpec((tm, d), lambda i, f: (i, 0)),
                  pl.BlockSpec((1, d), lambda i, f: (0, 0)),
                  pl.BlockSpec((d, tf), lambda i, f: (0, f)),
                  pl.BlockSpec((tf, d), lambda i, f: (f, 0)),
                  pl.BlockSpec((1, d), lambda i, f: (0, 0))],
        out_specs=pl.BlockSpec((tm, d), lambda i, f: (i, 0)),
        out_shape=jax.ShapeDtypeStruct((m, d), F32),
        scratch_shapes=[pltpu.VMEM((tm, d), BF16)],
        compiler_params=_params(("parallel", "arbitrary")),
    )(h, ln_mlp.reshape(1, d), w_up, w_down, ln_final.reshape(1, d))


def _pad_rows(a, height):
    return jnp.pad(a, ((0, height - a.shape[0]), (0, 0)))


def _layer(h, mem, ln_mix, w_in, rw_mu, rw_w0, rw_w_up, rw_a0, rw_a_up, rw_g_up, rw_k_k, rw_k_a, rw_r_k,
           rw_ln_w, rw_ln_b, fox_b_f, fox_q_norm, fox_k_norm, w_proj_a, w_proj_b, w_out, ln_cross, ln_mem,
           w_cq, w_ckv, w_co, ln_mlp, w_up, w_down, ln_final):
    w_t = w_in.T
    fox_row0 = RW_COLS
    gate_row0 = RW_COLS + 3 * WIDTH + N_HEADS
    mu_p = jnp.pad(rw_mu, (0, RW_PAD - RW_COLS))
    bf_rep = jnp.pad(jnp.tile(jnp.repeat(fox_b_f, BIAS_PARTS), 2), (0, LANES - 2 * BIAS_LANES)).reshape(1, LANES)
    wup_p = _pad_rows(rw_w_up, LORA_PAD).astype(BF16)
    aup_p = jnp.pad(rw_a_up, ((DECAY_LORA, 0), (0, 0))).astype(BF16)
    gup_p = _pad_rows(rw_g_up, GATE_PAD).astype(BF16)

    xn = _rmsnorm_bf16(h, ln_mix, 512)
    p_rw = _project(xn, w_t, 0, RW_PAD, 1024, 1792)
    p_fox = _project(xn, w_t, fox_row0, FOX_PAD, 1024, 1792)
    gates = _project(xn, w_t, gate_row0, w_t.shape[0] - gate_row0, 1024, 2048, BF16, gate=True)

    y_a = _rw_branch(p_rw, mu_p, rw_w0, wup_p, rw_a0, aup_p, gup_p, rw_k_k, rw_k_a, rw_r_k.reshape(-1), rw_ln_w,
                     rw_ln_b, 2 * CHUNK)

    fq, fk, fvt, fqa, fka, fbound = _fox_prep(p_fox, fox_q_norm, fox_k_norm, bf_rep, 512)
    y_b = lax.cond(jnp.max(fbound) < SAFE_LOGIT_BOUND,
                   functools.partial(_fox_attn, tq=512, running_max=False),
                   functools.partial(_fox_attn, tq=512, running_max=True), fq, fqa, fk, fka, fvt)

    kv = _mem_kv(mem, ln_mem, w_ckv.astype(BF16))
    h = _post(y_a, y_b, gates, h, w_proj_a.astype(BF16), w_proj_b.astype(BF16), w_out.astype(BF16), ln_cross,
              w_cq.astype(BF16), kv, w_co.astype(BF16), 512)
    return _mlp(h, ln_mlp, w_up.astype(BF16), w_down.astype(BF16), ln_final, 512, 1024)


def kernel(x, mem, ln_mix, w_in, rw_mu, rw_w0, rw_w_up, rw_a0, rw_a_up, rw_g_up, rw_k_k, rw_k_a, rw_r_k, rw_ln_w, rw_ln_b, fox_b_f, fox_q_norm, fox_k_norm, w_proj_a, w_proj_b, w_out, ln_cross, ln_mem, w_cq, w_ckv, w_co, ln_mlp, w_up, w_down, ln_final):
    b, s, d = x.shape
    assert b == 1 and ln_mix.shape[0] == 1, "single sequence, single layer"
    out = _layer(x[0], mem[0], ln_mix[0], w_in[0], rw_mu[0], rw_w0[0], rw_w_up[0], rw_a0[0], rw_a_up[0], rw_g_up[0],
                 rw_k_k[0], rw_k_a[0], rw_r_k[0], rw_ln_w[0], rw_ln_b[0], fox_b_f[0], fox_q_norm[0], fox_k_norm[0],
                 w_proj_a[0], w_proj_b[0], w_out[0], ln_cross[0], ln_mem[0], w_cq[0], w_ckv[0], w_co[0],
                 ln_mlp[0], w_up[0], w_down[0], ln_final)
    return out.reshape(b, s, d)
```

```python
import functools

import numpy as np

import jax
import jax.numpy as jnp
from jax import lax
from jax.experimental import pallas as pl
from jax.experimental.pallas import tpu as pltpu

F32 = jnp.float32
BF16 = jnp.bfloat16

LANES = 128
MXU_COLS = 256
HEAD_DIM = 64
N_HEADS = 16
WIDTH = N_HEADS * HEAD_DIM
N_PAIRS = WIDTH // LANES
DECAY_LORA, ICLR_LORA, GATE_LORA = 64, 64, 160
LORA_PAD = 128
GATE_PAD = 256
RW_COLS = 3 * WIDTH + DECAY_LORA + ICLR_LORA + GATE_LORA
RW_PAD = 3584
FZ_PAD = 512
FOX_PAD = 3 * WIDTH + FZ_PAD
BOUND_SLACK = 1.02
SAFE_LOGIT_BOUND = 50.0
BIAS_PARTS = 3
BIAS_LANES = BIAS_PARTS * N_HEADS
LOG2E = 1.4426950408889634
KEY_BLOCKS = 8
AHEAD = 2
ONES_ROWS = 16
X_HEADS, X_HEAD_DIM = 4, 128
NORM_EPS = 1e-5
GN_EPS = 64e-5
CHUNK = 64
SUB = 16
VMEM_LIMIT = 56 * 1024 * 1024


def _params(sem, vmem=VMEM_LIMIT):
    return pltpu.CompilerParams(dimension_semantics=sem, vmem_limit_bytes=vmem)


def _dot(a, b):
    return jnp.dot(a, b, preferred_element_type=F32)


def _dot_t(a, b):
    return lax.dot_general(a, b, (((1,), (1,)), ((), ())), preferred_element_type=F32)


def _split(x, terms):
    parts, rem = [], x
    for i in range(terms):
        part = rem.astype(BF16)
        parts.append(part)
        if i + 1 < terms:
            rem = rem - part.astype(F32)
    return parts


_DIMS = {"nn": (((1,), (0,)), ((), ())), "nt": (((1,), (1,)), ((), ())), "tn": (((0,), (0,)), ((), ()))}


def _mm(a_parts, b_parts, kind="nn"):
    order = max(len(a_parts), len(b_parts))
    acc = None
    for i, a in enumerate(a_parts):
        for j, b in enumerate(b_parts):
            if i + j < order:
                term = lax.dot_general(a, b, _DIMS[kind], preferred_element_type=F32)
                acc = term if acc is None else acc + term
    return acc


def _bf(x):
    return [x.astype(BF16)]


def _softplus(x):
    return jnp.maximum(x, 0.0) + jnp.log(1.0 + jnp.exp(-jnp.abs(x)))


def _sigmoid(x):
    return 1.0 / (1.0 + jnp.exp(-x))


def _head_ones():
    r = lax.broadcasted_iota(jnp.int32, (LANES, LANES), 0) // HEAD_DIM
    c = lax.broadcasted_iota(jnp.int32, (LANES, LANES), 1) // HEAD_DIM
    return (r == c).astype(F32)


def _stacked_headsum(xs, ones, terms):
    n = xs[0].shape[0]
    out = _mm(_split(jnp.concatenate(xs, axis=0), terms), ones)
    return [out[i * n:(i + 1) * n] for i in range(len(xs))]


def _headsum(x, ones_f32, terms=2):
    chunks = [x[:, c * LANES:(c + 1) * LANES] for c in range(x.shape[1] // LANES)]
    return jnp.concatenate(_stacked_headsum(chunks, [ones_f32.astype(BF16)], terms), axis=1)


def _rmsnorm_kernel(x_ref, g_ref, o_ref):
    x = x_ref[...]
    ms = jnp.mean(x * x, axis=-1, keepdims=True)
    o_ref[...] = (x * lax.rsqrt(ms + NORM_EPS) * g_ref[...]).astype(o_ref.dtype)


def _rmsnorm_bf16(x, g, tm):
    m, d = x.shape
    tm = min(tm, m)
    return pl.pallas_call(
        _rmsnorm_kernel,
        grid=(m // tm,),
        in_specs=[pl.BlockSpec((tm, d), lambda i: (i, 0)), pl.BlockSpec((1, d), lambda i: (0, 0))],
        out_specs=pl.BlockSpec((tm, d), lambda i: (i, 0)),
        out_shape=jax.ShapeDtypeStruct((m, d), BF16),
        compiler_params=_params(("parallel",)),
    )(x, g.reshape(1, d))


def _proj_kernel(a_ref, wt_ref, o_ref, w_bf16, *, gate):
    @pl.when(pl.program_id(1) == 0)
    def _():
        for c0 in range(0, wt_ref.shape[0], MXU_COLS):
            w_bf16[:, c0:c0 + MXU_COLS] = wt_ref[c0:c0 + MXU_COLS, :].T.astype(BF16)

    acc = _dot(a_ref[...], w_bf16[...])
    o_ref[...] = (_sigmoid(acc) if gate else acc).astype(o_ref.dtype)


def _project(a, w_t, row0, n, bm, bn, out_dtype=F32, gate=False):
    m, k = a.shape
    bm = min(bm, m)
    assert row0 % 8 == 0 and n % bn == 0 and bn % MXU_COLS == 0 and m % bm == 0 and row0 + n <= w_t.shape[0]
    return pl.pallas_call(
        functools.partial(_proj_kernel, gate=gate),
        grid=(n // bn, m // bm),
        in_specs=[pl.BlockSpec((bm, k), lambda j, i: (i, 0)),
                  pl.BlockSpec((pl.Element(bn), pl.Element(k)), lambda j, i: (pl.multiple_of(row0 + j * bn, 8), 0),
                               pipeline_mode=pl.Buffered(1))],
        out_specs=pl.BlockSpec((bm, bn), lambda j, i: (i, j)),
        out_shape=jax.ShapeDtypeStruct((m, n), out_dtype),
        scratch_shapes=[pltpu.VMEM((k, bn), BF16)],
        compiler_params=_params(("arbitrary", "arbitrary")),
    )(a, w_t)


def _rw_inputs(p, prev_row, mu, w0, wup, a0, aup, gup, k_k, k_a, ones):
    rows = lax.broadcasted_iota(jnp.int32, p.shape, 0)
    prev = jnp.where(rows == 0, prev_row, pltpu.roll(p, 1, 0))
    ps = p + (prev - p) * mu
    w = WIDTH
    r, k, v = ps[:, 0:w], ps[:, w:2 * w], ps[:, 2 * w:3 * w]
    o = 3 * w
    lora = ps[:, o:o + LORA_PAD]
    gd = ps[:, o + LORA_PAD:o + LORA_PAD + GATE_PAD]
    z = w0 + _mm(_bf(jnp.tanh(lora)), [wup])
    lw = -jnp.exp(-_softplus(-z) - 0.5)
    iclr = _sigmoid(a0 + _mm(_bf(lora), [aup]))
    gate = _mm(_bf(_sigmoid(gd)), [gup])
    kk = k * k_k
    kk2 = kk * kk
    ss = _stacked_headsum([kk2[:, c * LANES:(c + 1) * LANES] for c in range(N_PAIRS)], ones, 2)
    kk = kk * lax.rsqrt(jnp.maximum(jnp.concatenate(ss, axis=1), 1e-24))
    return r, k * (1.0 + (iclr - 1.0) * k_a), v, lw, -kk, kk * iclr, gate


def _pair_diag(x, head0):
    xb = x.astype(BF16)
    zero = jnp.zeros_like(xb)
    return [jnp.concatenate([jnp.where(head0, xb, zero), jnp.where(head0, zero, xb)], axis=0)]


def _unit_lower_inverse(a_list, sub_mask, eye, head0):
    c = CHUNK
    d = [jnp.where(sub_mask, a, 0.0) for a in a_list]
    e = [a - x for a, x in zip(a_list, d)]
    p = [eye + x for x in d]
    diag = lambda xs: [_pair_diag(x, head0) for x in xs]
    stack = lambda xs, ys: [_bf(jnp.concatenate([x, y], axis=0)) for x, y in zip(xs, ys)]
    dp = [_mm(_bf(x), y) for x, y in zip(d, diag(d))]
    for _ in range(SUB.bit_length() - 3):
        both = [_mm(x, y) for x, y in zip(stack(dp, p), diag(dp))]
        dp = [x[:c] for x in both]
        p = [x + y[c:] for x, y in zip(p, both)]
    p = [x + _mm(_bf(x), y) for x, y in zip(p, diag(dp))]
    f = [_mm(_bf(x), y) for x, y in zip(p, diag(e))]
    g = [eye + x for x in f]
    fp = [_mm(_bf(x), y) for x, y in zip(f, diag(f))]
    for _ in range((CHUNK // SUB).bit_length() - 3):
        both = [_mm(x, y) for x, y in zip(stack(fp, g), diag(fp))]
        fp = [x[:c] for x in both]
        g = [x + y[c:] for x, y in zip(g, both)]
    g = [x + _mm(_bf(x), y) for x, y in zip(g, diag(fp))]
    return [_mm(_bf(x), y) for x, y in zip(g, diag(p))]


def _rw_kernel(p_ref, mu_ref, w0_ref, wup_ref, a0_ref, aup_ref, gup_ref, kkw_ref, ka_ref, rk_ref, lnw_ref, lnb_ref,
               o_ref, h_ref, carry_ref):
    c = CHUNK
    t_tile = p_ref.shape[0]
    n_chunks = t_tile // c

    @pl.when(pl.program_id(0) == 0)
    def _():
        h_ref[...] = jnp.zeros_like(h_ref)
        carry_ref[...] = jnp.zeros_like(carry_ref)

    lane = lax.broadcasted_iota(jnp.int32, (c, LANES), 1)
    head0 = lane < HEAD_DIM
    ri = lax.broadcasted_iota(jnp.int32, (c, LANES), 0)
    ci = jnp.bitwise_and(lane, HEAD_DIM - 1)
    strict = ri > ci
    incl = ri >= ci
    eye = (ri == ci).astype(F32)
    sub_mask = (ri // SUB) == (ci // SUB)
    tril = _bf(lax.broadcasted_iota(jnp.int32, (c, c), 0) >= lax.broadcasted_iota(jnp.int32, (c, c), 1))
    ones_f = _head_ones()
    ones = _bf(ones_f)
    pair_mask = ones_f > 0.5
    diag = lambda x: _pair_diag(x, head0)
    units = [(s, p) for s in range(n_chunks) for p in range(N_PAIRS)]
    rows = lambda s: slice(s * c, (s + 1) * c)
    cols = lambda p: slice(p * LANES, (p + 1) * LANES)

    p_tile = p_ref[...]
    r_all, k_all, v_all, lw_all, a_all, b_all, gate_all = _rw_inputs(
        p_tile, carry_ref[...], mu_ref[...], w0_ref[...], wup_ref[...], a0_ref[...], aup_ref[...], gup_ref[...],
        kkw_ref[...], ka_ref[...], ones)
    carry_ref[...] = p_tile[t_tile - 1:t_tile, :]

    pre = []
    for s in range(n_chunks):
        lw, k, b = lw_all[rows(s), :], k_all[rows(s), :], b_all[rows(s), :]
        cl = _mm(tril, _split(lw, 3))
        cl_end = cl[c - 1:c, :]
        e_neg = jnp.exp(-cl)
        e_end = jnp.exp(cl_end - cl)
        pre.append(dict(rt=r_all[rows(s), :] * jnp.exp(cl), at=a_all[rows(s), :] * jnp.exp(cl - lw),
                        bt=b * e_neg, kt=k * e_neg, b_end=b * e_end, k_end=k * e_end, decay_end=jnp.exp(cl_end)))
    get = lambda name: [pre[s][name][:, cols(p)] for s, p in units]
    tile = lambda x: [x[rows(s), cols(p)] for s, p in units]
    r, k, v, gate = tile(r_all), tile(k_all), tile(v_all), tile(gate_all)
    rt, at = get("rt"), get("at")
    stack = lambda xs, ys: [jnp.concatenate([x, y], axis=0) for x, y in zip(xs, ys)]
    bk_end = [_bf(x) for x in stack(get("b_end"), get("k_end"))]
    v_diag = [diag(x) for x in v]
    mask2 = jnp.concatenate([strict, incl], axis=0)

    lhs = [_bf(x) for x in stack(at, rt)]
    sb = [_mm(x, diag(y), "nt") for x, y in zip(lhs, get("bt"))]
    sk = [_mm(x, diag(y), "nt") for x, y in zip(lhs, get("kt"))]
    skv = [_mm(_bf(jnp.where(mask2, x, 0.0)), y) for x, y in zip(sk, v_diag)]
    rb = [_bf(jnp.where(incl, x[c:], 0.0)) for x in sb]
    t_inv = [_bf(x) for x in _unit_lower_inverse([jnp.where(strict, x[:c], 0.0) for x in sb], sub_mask, eye, head0)]
    w = [_mm(x, diag(y)) for x, y in zip(t_inv, at)]
    tk = [_mm(x, diag(y[:c])) for x, y in zip(t_inv, skv)]
    s0_lhs = [_bf(x) for x in stack(w, rt)]

    ht = [h_ref[p] for p in range(N_PAIRS)]
    u, rs0 = [], []
    for s in range(n_chunks):
        i0 = s * N_PAIRS
        s0 = [_mm(s0_lhs[i0 + p], _bf(ht[p]), "nt") for p in range(N_PAIRS)]
        u += [s0[p][:c] + tk[i0 + p] for p in range(N_PAIRS)]
        rs0 += [s0[p][c:] for p in range(N_PAIRS)]
        upd = [_mm(_bf(jnp.concatenate([u[i0 + p], v[i0 + p]], axis=0)), bk_end[i0 + p], "tn") for p in range(N_PAIRS)]
        ht = [ht[p] * pre[s]["decay_end"][:, cols(p)] + jnp.where(pair_mask, upd[p], 0.0) for p in range(N_PAIRS)]
    for p in range(N_PAIRS):
        h_ref[p] = ht[p]

    y = [a + _mm(x, diag(z)) + b[c:] for a, x, z, b in zip(rs0, rb, u, skv)]
    rkr = [r[i] * k[i] * rk_ref[:, cols(p)] for i, (s, p) in enumerate(units)]
    sums = _stacked_headsum(y + rkr, ones, 2)
    dlt = [a - b * (1.0 / HEAD_DIM) for a, b in zip(y, sums[:len(units)])]
    var = _stacked_headsum([x * x for x in dlt], ones, 2)
    for i, (s, p) in enumerate(units):
        yn = dlt[i] * lax.rsqrt(var[i] * (1.0 / HEAD_DIM) + GN_EPS) * lnw_ref[:, cols(p)] + lnb_ref[:, cols(p)]
        o_ref[rows(s), cols(p)] = ((yn + sums[len(units) + i] * v[i]) * gate[i]).astype(o_ref.dtype)


def _rw_branch(p_rw, mu_p, w0, wup_p, a0, aup_p, gup_p, k_k, k_a, r_k, ln_w, ln_b, t_tile):
    m = p_rw.shape[0]
    t_tile = min(t_tile, m)
    row = lambda a: a.reshape(1, -1)
    full = lambda a: pl.BlockSpec(a.shape, lambda t: (0, 0))
    params = [row(mu_p), row(w0), wup_p, row(a0), aup_p, gup_p, row(k_k), row(k_a), row(r_k), row(ln_w), row(ln_b)]
    return pl.pallas_call(
        _rw_kernel,
        grid=(m // t_tile,),
        in_specs=[pl.BlockSpec((t_tile, RW_PAD), lambda t: (t, 0))] + [full(a) for a in params],
        out_specs=pl.BlockSpec((t_tile, WIDTH), lambda t: (t, 0)),
        out_shape=jax.ShapeDtypeStruct((m, WIDTH), BF16),
        scratch_shapes=[pltpu.VMEM((N_PAIRS, LANES, LANES), F32), pltpu.VMEM((1, RW_PAD), F32)],
        compiler_params=_params(("arbitrary",)),
    )(p_rw, *params)


def _fox_prep_kernel(p_ref, qn_ref, kn_ref, bf_ref, part_ref, sel_ref, selz_ref,
                     q_out, k_out, vt_out, qa_out, ka_out, bound_out, carry_ref, kmax_ref, bmax_ref):
    tm = p_ref.shape[0]

    @pl.when(pl.program_id(0) == 0)
    def _():
        carry_ref[...] = jnp.zeros_like(carry_ref)
        kmax_ref[...] = jnp.zeros_like(kmax_ref)
        bmax_ref[...] = jnp.zeros_like(bmax_ref)

    w = WIDTH
    ones = _head_ones()
    q, k = p_ref[:, 0:w], p_ref[:, w:2 * w]
    q = q * lax.rsqrt(_headsum(q * q, ones) * (1.0 / HEAD_DIM) + NORM_EPS) * qn_ref[...]
    k = k * lax.rsqrt(_headsum(k * k, ones) * (1.0 / HEAD_DIM) + NORM_EPS) * kn_ref[...]
    qb = (q * (HEAD_DIM ** -0.5 * LOG2E)).astype(BF16)
    kb = k.astype(BF16)
    q_out[...] = qb
    k_out[...] = kb
    vt_out[...] = p_ref[:, 2 * w:3 * w].T.astype(vt_out.dtype)

    qf, kf = qb.astype(F32), kb.astype(F32)
    k_norm = jnp.sqrt(_headsum(kf * kf, ones, 1))
    kmax = jnp.maximum(kmax_ref[...], jnp.max(k_norm, axis=0, keepdims=True))
    kmax_ref[...] = kmax
    bound = jnp.sqrt(_headsum(qf * qf, ones, 1)) * kmax * BOUND_SLACK
    bsel = _mm(_bf(bound), [sel_ref[...]])

    fz = _mm(_split(p_ref[:, 3 * w:3 * w + LANES], 3), [selz_ref[...]])
    logf = -_softplus(-(fz + bf_ref[...]))
    ri = lax.broadcasted_iota(jnp.int32, (tm, tm), 0)
    ci = lax.broadcasted_iota(jnp.int32, (tm, tm), 1)
    cum = _mm(_bf(ri >= ci), _split(logf, 3)) + carry_ref[...]
    carry_ref[...] = cum[tm - 1:tm, :]

    part = jnp.broadcast_to(part_ref[...], (tm, LANES))

    def pick(x):
        parts = [t.astype(F32) for t in _split(x, BIAS_PARTS)]
        return jnp.where(part == 0, parts[0], jnp.where(part == 1, parts[1], parts[2]))

    lane = lax.broadcasted_iota(jnp.int32, (tm, LANES), 1)
    low = lane < BIAS_LANES
    high = jnp.logical_and(lane >= BIAS_LANES, lane < 2 * BIAS_LANES)
    ka_out[...] = jnp.where(low, pick(cum * (-LOG2E)), jnp.where(high, 1.0, 0.0)).astype(ka_out.dtype)
    qa_out[...] = jnp.where(low, 1.0, jnp.where(high, pick(cum * LOG2E - bsel), 0.0)).astype(qa_out.dtype)
    bmax = jnp.maximum(bmax_ref[...], jnp.max(bsel, axis=0, keepdims=True))
    bmax_ref[...] = bmax
    bound_out[...] = jnp.broadcast_to(bmax, bound_out.shape)


def _fox_prep(p_fox, q_norm, k_norm, bf_rep, tm):
    m = p_fox.shape[0]
    tm = min(tm, m)
    qn = jnp.tile(q_norm, N_HEADS).reshape(1, WIDTH)
    kn = jnp.tile(k_norm, N_HEADS).reshape(1, WIDTH)
    part = (jnp.arange(LANES, dtype=jnp.int32) % BIAS_PARTS).reshape(1, LANES)
    heads = np.arange(N_HEADS)
    sel = np.zeros((WIDTH, LANES), np.float32)
    for j in range(BIAS_PARTS):
        sel[heads * HEAD_DIM, BIAS_LANES + BIAS_PARTS * heads + j] = 1.0
    sel = jnp.asarray(sel, BF16)
    selz = np.zeros((LANES, LANES), np.float32)
    for j in range(BIAS_PARTS):
        for group in (0, BIAS_LANES):
            selz[heads, group + BIAS_PARTS * heads + j] = 1.0
    selz = jnp.asarray(selz, BF16)
    full = lambda a: pl.BlockSpec(a.shape, lambda i: (0, 0))
    rows = lambda width: pl.BlockSpec((tm, width), lambda i: (i, 0))
    act = jax.ShapeDtypeStruct((m, WIDTH), BF16)
    aug = jax.ShapeDtypeStruct((m, LANES), BF16)
    return pl.pallas_call(
        _fox_prep_kernel,
        grid=(m // tm,),
        in_specs=[rows(FOX_PAD), full(qn), full(kn), full(bf_rep), full(part), full(sel), full(selz)],
        out_specs=[rows(WIDTH), rows(WIDTH), pl.BlockSpec((WIDTH, tm), lambda i: (0, i)), rows(LANES), rows(LANES),
                   pl.BlockSpec((8, LANES), lambda i: (0, 0))],
        out_shape=[act, act, jax.ShapeDtypeStruct((WIDTH, m), BF16), aug, aug, jax.ShapeDtypeStruct((8, LANES), F32)],
        scratch_shapes=[pltpu.VMEM((1, LANES), F32), pltpu.VMEM((1, WIDTH), F32), pltpu.VMEM((1, LANES), F32)],
        compiler_params=_params(("arbitrary",)),
    )(p_fox, qn, kn, bf_rep, part, sel, selz)


def _fox_attn_kernel(qi_ref, kb_ref, kind_ref, q_ref, qa_ref, k_ref, ka_ref, vt_ref, o_ref, m_ref, acc_ref, *,
                     running_max):
    tq = q_ref.shape[0]
    pair, step_id = pl.program_id(0), pl.program_id(1)
    kb, kind = kb_ref[step_id], kind_ref[step_id]

    @pl.when(kb == 0)
    def _():
        m_ref[...] = jnp.full_like(m_ref, -jnp.inf)
        acc_ref[...] = jnp.zeros_like(acc_ref)

    def step(kinds):
        q, qa = q_ref[...], qa_ref[...]
        lane_q = lax.broadcasted_iota(jnp.int32, (tq, LANES), 1)
        q_full = []
        for h in range(2):
            hmask = (lane_q < HEAD_DIM) if h == 0 else (lane_q >= HEAD_DIM)
            first = (2 * pair + h) * BIAS_PARTS
            in_group = lambda lo: jnp.logical_and(lane_q >= lo, lane_q < lo + BIAS_PARTS)
            aug = jnp.logical_or(in_group(first), in_group(first + BIAS_LANES))
            q_full.append(jnp.concatenate([jnp.where(hmask, q, jnp.zeros_like(q)),
                                           jnp.where(aug, qa, jnp.zeros_like(qa))], axis=1))
        subs = [slice(b * tq, (b + 1) * tq) for b in range(len(kinds))]
        k_full = [jnp.concatenate([k_ref[sl, :], ka_ref[sl, :]], axis=1) for sl in subs]
        ones_rows = jnp.ones((ONES_ROWS, tq), BF16)
        v_aug = [jnp.concatenate([vt_ref[:, sl], ones_rows], axis=0) for sl in subs]
        scores = lambda b: [_mm([k_full[b]], [q_full[h]], "nt") for h in range(2)]
        s = [scores(b) for b in range(min(AHEAD, len(kinds)))]
        causal = lax.broadcasted_iota(jnp.int32, (tq, tq), 0) <= lax.broadcasted_iota(jnp.int32, (tq, tq), 1)
        m = [m_ref[h] for h in range(2)]
        acc = [acc_ref[h] for h in range(2)]
        for b, diagonal in enumerate(kinds):
            if b + AHEAD < len(kinds):
                s.append(scores(b + AHEAD))
            sb = [jnp.where(causal, x, -jnp.inf) for x in s[b]] if diagonal else s[b]
            if running_max:
                m_new = [jnp.maximum(m[h], jnp.max(sb[h], axis=0, keepdims=True)) for h in range(2)]
                p = [jnp.exp2(sb[h] - m_new[h]).astype(BF16) for h in range(2)]
                acc = [jnp.exp2(m[h] - m_new[h]) * acc[h] for h in range(2)]
                m = m_new
            else:
                p = [jnp.exp2(sb[h]).astype(BF16) for h in range(2)]
            pv = [_mm([v_aug[b]], [p[h]]) for h in range(2)]
            acc = [acc[h] + pv[h] for h in range(2)]
        for h in range(2):
            acc_ref[h] = acc[h]
            if running_max:
                m_ref[h] = m[h]

    def finish():
        row = lax.broadcasted_iota(jnp.int32, (LANES, tq), 0)
        o = [acc_ref[h, 0:LANES, :] / acc_ref[h, LANES:LANES + 1, :] for h in range(2)]
        o_ref[...] = jnp.where(row < HEAD_DIM, o[0], o[1]).T.astype(o_ref.dtype)

    @pl.when(kind == 0)
    def _():
        step((False,) * KEY_BLOCKS)

    for n_full in range(KEY_BLOCKS):
        @pl.when(kind == 1 + n_full)
        def _(n_full=n_full):
            step((False,) * n_full + (True,))
            finish()


def _fox_attn(q, qa, k, ka, vt, tq, running_max):
    m = q.shape[0]
    tq = min(tq, m)
    nq = m // tq
    g = KEY_BLOCKS
    assert nq % g == 0, "key blocks are fetched in groups"
    steps = []
    for i in range(nq):
        steps += [(i, j, 0) for j in range(i // g)] + [(i, i // g, 1 + i % g)]
    qi_tab, kb_tab, kind_tab = (jnp.asarray([st[c] for st in steps], jnp.int32) for c in range(3))
    qspec = pl.BlockSpec((tq, LANES), lambda p, t, qi, kb, kind: (qi[t], p))
    qaspec = pl.BlockSpec((tq, LANES), lambda p, t, qi, kb, kind: (qi[t], 0))
    kspec = pl.BlockSpec((g * tq, LANES), lambda p, t, qi, kb, kind: (kb[t], p))
    kaspec = pl.BlockSpec((g * tq, LANES), lambda p, t, qi, kb, kind: (kb[t], 0))
    vtspec = pl.BlockSpec((LANES, g * tq), lambda p, t, qi, kb, kind: (p, kb[t]))
    return pl.pallas_call(
        functools.partial(_fox_attn_kernel, running_max=running_max),
        grid_spec=pltpu.PrefetchScalarGridSpec(
            num_scalar_prefetch=3,
            grid=(N_PAIRS, len(steps)),
            in_specs=[qspec, qaspec, kspec, kaspec, vtspec],
            out_specs=qspec,
            scratch_shapes=[pltpu.VMEM((2, 1, tq), F32), pltpu.VMEM((2, LANES + ONES_ROWS, tq), F32)],
        ),
        out_shape=jax.ShapeDtypeStruct((m, WIDTH), BF16),
        compiler_params=_params(("parallel", "arbitrary")),
    )(qi_tab, kb_tab, kind_tab, q, qa, k, ka, vt)


def _mem_kv_kernel(mem_ref, g_ref, w_ref, o_ref):
    x = mem_ref[...]
    ms = jnp.mean(x * x, axis=-1, keepdims=True)
    xn = (x * lax.rsqrt(ms + NORM_EPS) * g_ref[...]).astype(BF16)
    o_ref[...] = _dot(xn, w_ref[...]).astype(o_ref.dtype)


def _mem_kv(mem, ln_mem, w_ckv):
    n, d = mem.shape
    ins = [mem, ln_mem.reshape(1, d), w_ckv]
    return pl.pallas_call(
        _mem_kv_kernel,
        grid=(1,),
        in_specs=[pl.BlockSpec(a.shape, lambda i: (0, 0)) for a in ins],
        out_specs=pl.BlockSpec((n, w_ckv.shape[1]), lambda i: (0, 0)),
        out_shape=jax.ShapeDtypeStruct((n, w_ckv.shape[1]), BF16),
        compiler_params=_params(("arbitrary",)),
    )(*ins)


def _post_kernel(ya_ref, yb_ref, pg_ref, x_ref, wa_ref, wb_ref, wo_ref, lnc_ref, wcq_ref, kv_ref, wco_ref, o_ref):
    d = x_ref.shape[1]
    xw = X_HEADS * X_HEAD_DIM
    ga = pg_ref[:, 0:d].astype(F32)
    gb = pg_ref[:, d:2 * d].astype(F32)
    merged = ga * _dot(ya_ref[...], wa_ref[...]) + gb * _dot(yb_ref[...], wb_ref[...])
    h = x_ref[...] + _dot(merged.astype(BF16), wo_ref[...])

    ms = jnp.mean(h * h, axis=-1, keepdims=True)
    hn = (h * lax.rsqrt(ms + NORM_EPS) * lnc_ref[...]).astype(BF16)
    q = _dot(hn, wcq_ref[...])
    outs = []
    for hd in range(X_HEADS):
        cols = slice(hd * X_HEAD_DIM, (hd + 1) * X_HEAD_DIM)
        vcols = slice(xw + hd * X_HEAD_DIM, xw + (hd + 1) * X_HEAD_DIM)
        s = _dot_t(q[:, cols].astype(BF16), kv_ref[:, cols]) * (X_HEAD_DIM ** -0.5)
        s = s - jnp.max(s, axis=1, keepdims=True)
        e = jnp.exp(s)
        pr = e / jnp.sum(e, axis=1, keepdims=True)
        outs.append(_dot(pr.astype(BF16), kv_ref[:, vcols]))
    o = jnp.concatenate(outs, axis=1).astype(BF16)
    o_ref[...] = h + _dot(o, wco_ref[...])


def _post(ya, yb, pg, x, wa, wb, wo, ln_cross, wcq, kv, wco, tm):
    m, d = x.shape
    tm = min(tm, m)
    lnc = ln_cross.reshape(1, d)
    rows = lambda a: pl.BlockSpec((tm, a.shape[1]), lambda i: (i, 0))
    const = lambda a: pl.BlockSpec(a.shape, lambda i: (0, 0), pipeline_mode=pl.Buffered(1))
    return pl.pallas_call(
        _post_kernel,
        grid=(m // tm,),
        in_specs=[rows(ya), rows(yb), rows(pg), rows(x)] + [const(a) for a in (wa, wb, wo, lnc, wcq, kv, wco)],
        out_specs=pl.BlockSpec((tm, d), lambda i: (i, 0)),
        out_shape=jax.ShapeDtypeStruct((m, d), F32),
        compiler_params=_params(("parallel",)),
    )(ya, yb, pg, x, wa, wb, wo, lnc, wcq, kv, wco)


def _mlp_kernel(h_ref, g_ref, wu_ref, wd_ref, gf_ref, o_ref, xn_ref):
    f = pl.program_id(1)

    @pl.when(f == 0)
    def _():
        h = h_ref[...]
        ms = jnp.mean(h * h, axis=-1, keepdims=True)
        xn_ref[...] = (h * lax.rsqrt(ms + NORM_EPS) * g_ref[...]).astype(xn_ref.dtype)
        o_ref[...] = jnp.zeros_like(o_ref)

    u = jnp.maximum(_dot(xn_ref[...], wu_ref[...]), 0.0)
    o_ref[...] += _dot((u * u).astype(BF16), wd_ref[...])

    @pl.when(f == pl.num_programs(1) - 1)
    def _():
        h = h_ref[...] + o_ref[...]
        ms = jnp.mean(h * h, axis=-1, keepdims=True)
        o_ref[...] = h * lax.rsqrt(ms + NORM_EPS) * gf_ref[...]


def _mlp(h, ln_mlp, w_up, w_down, ln_final, tm, tf):
    m, d = h.shape
    dff = w_up.shape[1]
    tm, tf = min(tm, m), min(tf, dff)
    return pl.pallas_call(
        _mlp_kernel,
        grid=(m // tm, dff // tf),
        in_specs=[pl.BlockSpec((tm, d), lambda i, f: (i, 0)),
                  pl.BlockSpec((1, d), lambda i, f: (0, 0)),
                  pl.BlockSpec((d, tf), lambda i, f: (0, f)),
                  pl.BlockSpec((tf, d), lambda i, f: (f, 0)),
                  pl.BlockSpec((1, d), lambda i, f: (0, 0))],
        out_specs=pl.BlockSpec((tm, d), lambda i, f: (i, 0)),
        out_shape=jax.ShapeDtypeStruct((m, d), F32),
        scratch_shapes=[pltpu.VMEM((tm, d), BF16)],
        compiler_params=_params(("parallel", "arbitrary")),
    )(h, ln_mlp.reshape(1, d), w_up, w_down, ln_final.reshape(1, d))


def _pad_rows(a, height):
    return jnp.pad(a, ((0, height - a.shape[0]), (0, 0)))


def _layer(h, mem, ln_mix, w_in, rw_mu, rw_w0, rw_w_up, rw_a0, rw_a_up, rw_g_up, rw_k_k, rw_k_a, rw_r_k,
           rw_ln_w, rw_ln_b, fox_b_f, fox_q_norm, fox_k_norm, w_proj_a, w_proj_b, w_out, ln_cross, ln_mem,
           w_cq, w_ckv, w_co, ln_mlp, w_up, w_down, ln_final):
    w_t = w_in.T
    fox_row0 = RW_COLS
    gate_row0 = RW_COLS + 3 * WIDTH + N_HEADS
    mu_p = jnp.pad(rw_mu, (0, RW_PAD - RW_COLS))
    bf_rep = jnp.pad(jnp.tile(jnp.repeat(fox_b_f, BIAS_PARTS), 2), (0, LANES - 2 * BIAS_LANES)).reshape(1, LANES)
    wup_p = _pad_rows(rw_w_up, LORA_PAD).astype(BF16)
    aup_p = jnp.pad(rw_a_up, ((DECAY_LORA, 0), (0, 0))).astype(BF16)
    gup_p = _pad_rows(rw_g_up, GATE_PAD).astype(BF16)

    xn = _rmsnorm_bf16(h, ln_mix, 512)
    p_rw = _project(xn, w_t, 0, RW_PAD, 1024, 1792)
    p_fox = _project(xn, w_t, fox_row0, FOX_PAD, 1024, 1792)
    gates = _project(xn, w_t, gate_row0, w_t.shape[0] - gate_row0, 1024, 2048, BF16, gate=True)

    y_a = _rw_branch(p_rw, mu_p, rw_w0, wup_p, rw_a0, aup_p, gup_p, rw_k_k, rw_k_a, rw_r_k.reshape(-1), rw_ln_w,
                     rw_ln_b, 2 * CHUNK)

    fq, fk, fvt, fqa, fka, fbound = _fox_prep(p_fox, fox_q_norm, fox_k_norm, bf_rep, 512)
    y_b = lax.cond(jnp.max(fbound) < SAFE_LOGIT_BOUND,
                   functools.partial(_fox_attn, tq=512, running_max=False),
                   functools.partial(_fox_attn, tq=512, running_max=True), fq, fqa, fk, fka, fvt)

    kv = _mem_kv(mem, ln_mem, w_ckv.astype(BF16))
    h = _post(y_a, y_b, gates, h, w_proj_a.astype(BF16), w_proj_b.astype(BF16), w_out.astype(BF16), ln_cross,
              w_cq.astype(BF16), kv, w_co.astype(BF16), 512)
    return _mlp(h, ln_mlp, w_up.astype(BF16), w_down.astype(BF16), ln_final, 512, 1024)


def kernel(x, mem, ln_mix, w_in, rw_mu, rw_w0, rw_w_up, rw_a0, rw_a_up, rw_g_up, rw_k_k, rw_k_a, rw_r_k, rw_ln_w, rw_ln_b, fox_b_f, fox_q_norm, fox_k_norm, w_proj_a, w_proj_b, w_out, ln_cross, ln_mem, w_cq, w_ckv, w_co, ln_mlp, w_up, w_down, ln_final):
    b, s, d = x.shape
    assert b == 1 and ln_mix.shape[0] == 1, "single sequence, single layer"
    out = _layer(x[0], mem[0], ln_mix[0], w_in[0], rw_mu[0], rw_w0[0], rw_w_up[0], rw_a0[0], rw_a_up[0], rw_g_up[0],
                 rw_k_k[0], rw_k_a[0], rw_r_k[0], rw_ln_w[0], rw_ln_b[0], fox_b_f[0], fox_q_norm[0], fox_k_norm[0],
                 w_proj_a[0], w_proj_b[0], w_out[0], ln_cross[0], ln_mem[0], w_cq[0], w_ckv[0], w_co[0],
                 ln_mlp[0], w_up[0], w_down[0], ln_final)
    return out.reshape(b, s, d)
```

```python
import functools

import numpy as np

import jax
import jax.numpy as jnp
from jax import lax
from jax.experimental import pallas as pl
from jax.experimental.pallas import tpu as pltpu

F32 = jnp.float32
BF16 = jnp.bfloat16

LANES = 128
MXU_COLS = 256
HEAD_DIM = 64
N_HEADS = 16
WIDTH = N_HEADS * HEAD_DIM
N_PAIRS = WIDTH // LANES
DECAY_LORA, ICLR_LORA, GATE_LORA = 64, 64, 160
LORA_PAD = 128
GATE_PAD = 256
RW_COLS = 3 * WIDTH + DECAY_LORA + ICLR_LORA + GATE_LORA
RW_PAD = 3584
FZ_PAD = 512
FOX_PAD = 3 * WIDTH + FZ_PAD
BOUND_SLACK = 1.02
SAFE_LOGIT_BOUND = 50.0
BIAS_PARTS = 3
BIAS_LANES = BIAS_PARTS * N_HEADS
LOG2E = 1.4426950408889634
KEY_BLOCKS = 8
AHEAD = 2
ONES_ROWS = 16
X_HEADS, X_HEAD_DIM = 4, 128
NORM_EPS = 1e-5
GN_EPS = 64e-5
CHUNK = 64
SUB = 16
VMEM_LIMIT = 56 * 1024 * 1024


def _params(sem, vmem=VMEM_LIMIT):
    return pltpu.CompilerParams(dimension_semantics=sem, vmem_limit_bytes=vmem)


def _dot(a, b):
    return jnp.dot(a, b, preferred_element_type=F32)


def _dot_t(a, b):
    return lax.dot_general(a, b, (((1,), (1,)), ((), ())), preferred_element_type=F32)


def _split(x, terms):
    parts, rem = [], x
    for i in range(terms):
        part = rem.astype(BF16)
        parts.append(part)
        if i + 1 < terms:
            rem = rem - part.astype(F32)
    return parts


_DIMS = {"nn": (((1,), (0,)), ((), ())), "nt": (((1,), (1,)), ((), ())), "tn": (((0,), (0,)), ((), ()))}


def _mm(a_parts, b_parts, kind="nn"):
    order = max(len(a_parts), len(b_parts))
    acc = None
    for i, a in enumerate(a_parts):
        for j, b in enumerate(b_parts):
            if i + j < order:
                term = lax.dot_general(a, b, _DIMS[kind], preferred_element_type=F32)
                acc = term if acc is None else acc + term
    return acc


def _bf(x):
    return [x.astype(BF16)]


def _softplus(x):
    return jnp.maximum(x, 0.0) + jnp.log(1.0 + jnp.exp(-jnp.abs(x)))


def _sigmoid(x):
    return 1.0 / (1.0 + jnp.exp(-x))


def _head_ones():
    r = lax.broadcasted_iota(jnp.int32, (LANES, LANES), 0) // HEAD_DIM
    c = lax.broadcasted_iota(jnp.int32, (LANES, LANES), 1) // HEAD_DIM
    return (r == c).astype(F32)


def _stacked_headsum(xs, ones, terms):
    n = xs[0].shape[0]
    out = _mm(_split(jnp.concatenate(xs, axis=0), terms), ones)
    return [out[i * n:(i + 1) * n] for i in range(len(xs))]


def _headsum(x, ones_f32, terms=2):
    chunks = [x[:, c * LANES:(c + 1) * LANES] for c in range(x.shape[1] // LANES)]
    return jnp.concatenate(_stacked_headsum(chunks, [ones_f32.astype(BF16)], terms), axis=1)


def _rmsnorm_kernel(x_ref, g_ref, o_ref):
    x = x_ref[...]
    ms = jnp.mean(x * x, axis=-1, keepdims=True)
    o_ref[...] = (x * lax.rsqrt(ms + NORM_EPS) * g_ref[...]).astype(o_ref.dtype)


def _rmsnorm_bf16(x, g, tm):
    m, d = x.shape
    tm = min(tm, m)
    return pl.pallas_call(
        _rmsnorm_kernel,
        grid=(m // tm,),
        in_specs=[pl.BlockSpec((tm, d), lambda i: (i, 0)), pl.BlockSpec((1, d), lambda i: (0, 0))],
        out_specs=pl.BlockSpec((tm, d), lambda i: (i, 0)),
        out_shape=jax.ShapeDtypeStruct((m, d), BF16),
        compiler_params=_params(("parallel",)),
    )(x, g.reshape(1, d))


def _proj_kernel(a_ref, wt_ref, o_ref, w_bf16, *, gate):
    @pl.when(pl.program_id(1) == 0)
    def _():
        for c0 in range(0, wt_ref.shape[0], MXU_COLS):
            w_bf16[:, c0:c0 + MXU_COLS] = wt_ref[c0:c0 + MXU_COLS, :].T.astype(BF16)

    acc = _dot(a_ref[...], w_bf16[...])
    o_ref[...] = (_sigmoid(acc) if gate else acc).astype(o_ref.dtype)


def _project(a, w_t, row0, n, bm, bn, out_dtype=F32, gate=False):
    m, k = a.shape
    bm = min(bm, m)
    assert row0 % 8 == 0 and n % bn == 0 and bn % MXU_COLS == 0 and m % bm == 0 and row0 + n <= w_t.shape[0]
    return pl.pallas_call(
        functools.partial(_proj_kernel, gate=gate),
        grid=(n // bn, m // bm),
        in_specs=[pl.BlockSpec((bm, k), lambda j, i: (i, 0)),
                  pl.BlockSpec((pl.Element(bn), pl.Element(k)), lambda j, i: (pl.multiple_of(row0 + j * bn, 8), 0),
                               pipeline_mode=pl.Buffered(1))],
        out_specs=pl.BlockSpec((bm, bn), lambda j, i: (i, j)),
        out_shape=jax.ShapeDtypeStruct((m, n), out_dtype),
        scratch_shapes=[pltpu.VMEM((k, bn), BF16)],
        compiler_params=_params(("arbitrary", "arbitrary")),
    )(a, w_t)


def _rw_inputs(p, prev_row, mu, w0, wup, a0, aup, gup, k_k, k_a, ones):
    rows = lax.broadcasted_iota(jnp.int32, p.shape, 0)
    prev = jnp.where(rows == 0, prev_row, pltpu.roll(p, 1, 0))
    ps = p + (prev - p) * mu
    w = WIDTH
    r, k, v = ps[:, 0:w], ps[:, w:2 * w], ps[:, 2 * w:3 * w]
    o = 3 * w
    lora = ps[:, o:o + LORA_PAD]
    gd = ps[:, o + LORA_PAD:o + LORA_PAD + GATE_PAD]
    z = w0 + _mm(_bf(jnp.tanh(lora)), [wup])
    lw = -jnp.exp(-_softplus(-z) - 0.5)
    iclr = _sigmoid(a0 + _mm(_bf(lora), [aup]))
    gate = _mm(_bf(_sigmoid(gd)), [gup])
    kk = k * k_k
    kk2 = kk * kk
    ss = _stacked_headsum([kk2[:, c * LANES:(c + 1) * LANES] for c in range(N_PAIRS)], ones, 2)
    kk = kk * lax.rsqrt(jnp.maximum(jnp.concatenate(ss, axis=1), 1e-24))
    return r, k * (1.0 + (iclr - 1.0) * k_a), v, lw, -kk, kk * iclr, gate


def _pair_diag(x, head0):
    xb = x.astype(BF16)
    zero = jnp.zeros_like(xb)
    return [jnp.concatenate([jnp.where(head0, xb, zero), jnp.where(head0, zero, xb)], axis=0)]


def _unit_lower_inverse(a_list, sub_mask, eye, head0):
    c = CHUNK
    d = [jnp.where(sub_mask, a, 0.0) for a in a_list]
    e = [a - x for a, x in zip(a_list, d)]
    p = [eye + x for x in d]
    diag = lambda xs: [_pair_diag(x, head0) for x in xs]
    stack = lambda xs, ys: [_bf(jnp.concatenate([x, y], axis=0)) for x, y in zip(xs, ys)]
    dp = [_mm(_bf(x), y) for x, y in zip(d, diag(d))]
    for _ in range(SUB.bit_length() - 3):
        both = [_mm(x, y) for x, y in zip(stack(dp, p), diag(dp))]
        dp = [x[:c] for x in both]
        p = [x + y[c:] for x, y in zip(p, both)]
    p = [x + _mm(_bf(x), y) for x, y in zip(p, diag(dp))]
    f = [_mm(_bf(x), y) for x, y in zip(p, diag(e))]
    g = [eye + x for x in f]
    fp = [_mm(_bf(x), y) for x, y in zip(f, diag(f))]
    for _ in range((CHUNK // SUB).bit_length() - 3):
        both = [_mm(x, y) for x, y in zip(stack(fp, g), diag(fp))]
        fp = [x[:c] for x in both]
        g = [x + y[c:] for x, y in zip(g, both)]
    g = [x + _mm(_bf(x), y) for x, y in zip(g, diag(fp))]
    return [_mm(_bf(x), y) for x, y in zip(g, diag(p))]


def _rw_kernel(p_ref, mu_ref, w0_ref, wup_ref, a0_ref, aup_ref, gup_ref, kkw_ref, ka_ref, rk_ref, lnw_ref, lnb_ref,
               o_ref, h_ref, carry_ref):
    c = CHUNK
    t_tile = p_ref.shape[0]
    n_chunks = t_tile // c

    @pl.when(pl.program_id(0) == 0)
    def _():
        h_ref[...] = jnp.zeros_like(h_ref)
        carry_ref[...] = jnp.zeros_like(carry_ref)

    lane = lax.broadcasted_iota(jnp.int32, (c, LANES), 1)
    head0 = lane < HEAD_DIM
    ri = lax.broadcasted_iota(jnp.int32, (c, LANES), 0)
    ci = jnp.bitwise_and(lane, HEAD_DIM - 1)
    strict = ri > ci
    incl = ri >= ci
    eye = (ri == ci).astype(F32)
    sub_mask = (ri // SUB) == (ci // SUB)
    tril = _bf(lax.broadcasted_iota(jnp.int32, (c, c), 0) >= lax.broadcasted_iota(jnp.int32, (c, c), 1))
    ones_f = _head_ones()
    ones = _bf(ones_f)
    pair_mask = ones_f > 0.5
    diag = lambda x: _pair_diag(x, head0)
    units = [(s, p) for s in range(n_chunks) for p in range(N_PAIRS)]
    rows = lambda s: slice(s * c, (s + 1) * c)
    cols = lambda p: slice(p * LANES, (p + 1) * LANES)

    p_tile = p_ref[...]
    r_all, k_all, v_all, lw_all, a_all, b_all, gate_all = _rw_inputs(
        p_tile, carry_ref[...], mu_ref[...], w0_ref[...], wup_ref[...], a0_ref[...], aup_ref[...], gup_ref[...],
        kkw_ref[...], ka_ref[...], ones)
    carry_ref[...] = p_tile[t_tile - 1:t_tile, :]

    pre = []
    for s in range(n_chunks):
        lw, k, b = lw_all[rows(s), :], k_all[rows(s), :], b_all[rows(s), :]
        cl = _mm(tril, _split(lw, 3))
        cl_end = cl[c - 1:c, :]
        e_neg = jnp.exp(-cl)
        e_end = jnp.exp(cl_end - cl)
        pre.append(dict(rt=r_all[rows(s), :] * jnp.exp(cl), at=a_all[rows(s), :] * jnp.exp(cl - lw),
                        bt=b * e_neg, kt=k * e_neg, b_end=b * e_end, k_end=k * e_end, decay_end=jnp.exp(cl_end)))
    get = lambda name: [pre[s][name][:, cols(p)] for s, p in units]
    tile = lambda x: [x[rows(s), cols(p)] for s, p in units]
    r, k, v, gate = tile(r_all), tile(k_all), tile(v_all), tile(gate_all)
    rt, at = get("rt"), get("at")
    stack = lambda xs, ys: [jnp.concatenate([x, y], axis=0) for x, y in zip(xs, ys)]
    bk_end = [_bf(x) for x in stack(get("b_end"), get("k_end"))]
    v_diag = [diag(x) for x in v]
    mask2 = jnp.concatenate([strict, incl], axis=0)

    lhs = [_bf(x) for x in stack(at, rt)]
    sb = [_mm(x, diag(y), "nt") for x, y in zip(lhs, get("bt"))]
    sk = [_mm(x, diag(y), "nt") for x, y in zip(lhs, get("kt"))]
    skv = [_mm(_bf(jnp.where(mask2, x, 0.0)), y) for x, y in zip(sk, v_diag)]
    rb = [_bf(jnp.where(incl, x[c:], 0.0)) for x in sb]
    t_inv = [_bf(x) for x in _unit_lower_inverse([jnp.where(strict, x[:c], 0.0) for x in sb], sub_mask, eye, head0)]
    w = [_mm(x, diag(y)) for x, y in zip(t_inv, at)]
    tk = [_mm(x, diag(y[:c])) for x, y in zip(t_inv, skv)]
    s0_lhs = [_bf(x) for x in stack(w, rt)]

    ht = [h_ref[p] for p in range(N_PAIRS)]
    u, rs0 = [], []
    for s in range(n_chunks):
        i0 = s * N_PAIRS
        s0 = [_mm(s0_lhs[i0 + p], _bf(ht[p]), "nt") for p in range(N_PAIRS)]
        u += [s0[p][:c] + tk[i0 + p] for p in range(N_PAIRS)]
        rs0 += [s0[p][c:] for p in range(N_PAIRS)]
        upd = [_mm(_bf(jnp.concatenate([u[i0 + p], v[i0 + p]], axis=0)), bk_end[i0 + p], "tn") for p in range(N_PAIRS)]
        ht = [ht[p] * pre[s]["decay_end"][:, cols(p)] + jnp.where(pair_mask, upd[p], 0.0) for p in range(N_PAIRS)]
    for p in range(N_PAIRS):
        h_ref[p] = ht[p]

    y = [a + _mm(x, diag(z)) + b[c:] for a, x, z, b in zip(rs0, rb, u, skv)]
    rkr = [r[i] * k[i] * rk_ref[:, cols(p)] for i, (s, p) in enumerate(units)]
    sums = _stacked_headsum(y + rkr, ones, 2)
    dlt = [a - b * (1.0 / HEAD_DIM) for a, b in zip(y, sums[:len(units)])]
    var = _stacked_headsum([x * x for x in dlt], ones, 2)
    for i, (s, p) in enumerate(units):
        yn = dlt[i] * lax.rsqrt(var[i] * (1.0 / HEAD_DIM) + GN_EPS) * lnw_ref[:, cols(p)] + lnb_ref[:, cols(p)]
        o_ref[rows(s), cols(p)] = ((yn + sums[len(units) + i] * v[i]) * gate[i]).astype(o_ref.dtype)


def _rw_branch(p_rw, mu_p, w0, wup_p, a0, aup_p, gup_p, k_k, k_a, r_k, ln_w, ln_b, t_tile):
    m = p_rw.shape[0]
    t_tile = min(t_tile, m)
    row = lambda a: a.reshape(1, -1)
    full = lambda a: pl.BlockSpec(a.shape, lambda t: (0, 0))
    params = [row(mu_p), row(w0), wup_p, row(a0), aup_p, gup_p, row(k_k), row(k_a), row(r_k), row(ln_w), row(ln_b)]
    return pl.pallas_call(
        _rw_kernel,
        grid=(m // t_tile,),
        in_specs=[pl.BlockSpec((t_tile, RW_PAD), lambda t: (t, 0))] + [full(a) for a in params],
        out_specs=pl.BlockSpec((t_tile, WIDTH), lambda t: (t, 0)),
        out_shape=jax.ShapeDtypeStruct((m, WIDTH), BF16),
        scratch_shapes=[pltpu.VMEM((N_PAIRS, LANES, LANES), F32), pltpu.VMEM((1, RW_PAD), F32)],
        compiler_params=_params(("arbitrary",)),
    )(p_rw, *params)


def _fox_prep_kernel(p_ref, qn_ref, kn_ref, bf_ref, part_ref, sel_ref, selz_ref,
                     q_out, k_out, vt_out, qa_out, ka_out, bound_out, carry_ref, kmax_ref, bmax_ref):
    tm = p_ref.shape[0]

    @pl.when(pl.program_id(0) == 0)
    def _():
        carry_ref[...] = jnp.zeros_like(carry_ref)
        kmax_ref[...] = jnp.zeros_like(kmax_ref)
        bmax_ref[...] = jnp.zeros_like(bmax_ref)

    w = WIDTH
    ones = _head_ones()
    q, k = p_ref[:, 0:w], p_ref[:, w:2 * w]
    q = q * lax.rsqrt(_headsum(q * q, ones) * (1.0 / HEAD_DIM) + NORM_EPS) * qn_ref[...]
    k = k * lax.rsqrt(_headsum(k * k, ones) * (1.0 / HEAD_DIM) + NORM_EPS) * kn_ref[...]
    qb = (q * (HEAD_DIM ** -0.5 * LOG2E)).astype(BF16)
    kb = k.astype(BF16)
    q_out[...] = qb
    k_out[...] = kb
    vt_out[...] = p_ref[:, 2 * w:3 * w].T.astype(vt_out.dtype)

    qf, kf = qb.astype(F32), kb.astype(F32)
    k_sq = _headsum(kf * kf, ones, 1)
    kmax_sq = jnp.maximum(kmax_ref[...], jnp.max(k_sq, axis=0, keepdims=True))
    kmax_ref[...] = kmax_sq
    bound_sq = _headsum(qf * qf, ones, 1) * kmax_sq
    bsel = jnp.sqrt(_mm(_bf(bound_sq), [sel_ref[...]])) * BOUND_SLACK

    fz = _mm(_split(p_ref[:, 3 * w:3 * w + LANES], 3), [selz_ref[...]])
    logf = -_softplus(-(fz + bf_ref[...]))
    ri = lax.broadcasted_iota(jnp.int32, (tm, tm), 0)
    ci = lax.broadcasted_iota(jnp.int32, (tm, tm), 1)
    cum = _mm(_bf(ri >= ci), _split(logf, 3)) + carry_ref[...]
    carry_ref[...] = cum[tm - 1:tm, :]

    part = jnp.broadcast_to(part_ref[...], (tm, LANES))

    def pick(x):
        parts = [t.astype(F32) for t in _split(x, BIAS_PARTS)]
        return jnp.where(part == 0, parts[0], jnp.where(part == 1, parts[1], parts[2]))

    lane = lax.broadcasted_iota(jnp.int32, (tm, LANES), 1)
    low = lane < BIAS_LANES
    high = jnp.logical_and(lane >= BIAS_LANES, lane < 2 * BIAS_LANES)
    ka_out[...] = jnp.where(low, pick(cum * (-LOG2E)), jnp.where(high, 1.0, 0.0)).astype(ka_out.dtype)
    qa_out[...] = jnp.where(low, 1.0, jnp.where(high, pick(cum * LOG2E - bsel), 0.0)).astype(qa_out.dtype)
    bmax = jnp.maximum(bmax_ref[...], jnp.max(bsel, axis=0, keepdims=True))
    bmax_ref[...] = bmax
    bound_out[...] = jnp.broadcast_to(bmax, bound_out.shape)


def _fox_prep(p_fox, q_norm, k_norm, bf_rep, tm):
    m = p_fox.shape[0]
    tm = min(tm, m)
    qn = jnp.tile(q_norm, N_HEADS).reshape(1, WIDTH)
    kn = jnp.tile(k_norm, N_HEADS).reshape(1, WIDTH)
    part = (jnp.arange(LANES, dtype=jnp.int32) % BIAS_PARTS).reshape(1, LANES)
    heads = np.arange(N_HEADS)
    sel = np.zeros((WIDTH, LANES), np.float32)
    for j in range(BIAS_PARTS):
        sel[heads * HEAD_DIM, BIAS_LANES + BIAS_PARTS * heads + j] = 1.0
    sel = jnp.asarray(sel, BF16)
    selz = np.zeros((LANES, LANES), np.float32)
    for j in range(BIAS_PARTS):
        for group in (0, BIAS_LANES):
            selz[heads, group + BIAS_PARTS * heads + j] = 1.0
    selz = jnp.asarray(selz, BF16)
    full = lambda a: pl.BlockSpec(a.shape, lambda i: (0, 0))
    rows = lambda width: pl.BlockSpec((tm, width), lambda i: (i, 0))
    act = jax.ShapeDtypeStruct((m, WIDTH), BF16)
    aug = jax.ShapeDtypeStruct((m, LANES), BF16)
    return pl.pallas_call(
        _fox_prep_kernel,
        grid=(m // tm,),
        in_specs=[rows(FOX_PAD), full(qn), full(kn), full(bf_rep), full(part), full(sel), full(selz)],
        out_specs=[rows(WIDTH), rows(WIDTH), pl.BlockSpec((WIDTH, tm), lambda i: (0, i)), rows(LANES), rows(LANES),
                   pl.BlockSpec((8, LANES), lambda i: (0, 0))],
        out_shape=[act, act, jax.ShapeDtypeStruct((WIDTH, m), BF16), aug, aug, jax.ShapeDtypeStruct((8, LANES), F32)],
        scratch_shapes=[pltpu.VMEM((1, LANES), F32), pltpu.VMEM((1, WIDTH), F32), pltpu.VMEM((1, LANES), F32)],
        compiler_params=_params(("arbitrary",)),
    )(p_fox, qn, kn, bf_rep, part, sel, selz)


def _fox_attn_kernel(qi_ref, kb_ref, kind_ref, q_ref, qa_ref, k_ref, ka_ref, vt_ref, o_ref, m_ref, acc_ref, *,
                     running_max):
    tq = q_ref.shape[0]
    pair, step_id = pl.program_id(0), pl.program_id(1)
    kb, kind = kb_ref[step_id], kind_ref[step_id]

    @pl.when(kb == 0)
    def _():
        m_ref[...] = jnp.full_like(m_ref, -jnp.inf)
        acc_ref[...] = jnp.zeros_like(acc_ref)

    def step(kinds):
        q, qa = q_ref[...], qa_ref[...]
        lane_q = lax.broadcasted_iota(jnp.int32, (tq, LANES), 1)
        q_full = []
        for h in range(2):
            hmask = (lane_q < HEAD_DIM) if h == 0 else (lane_q >= HEAD_DIM)
            first = (2 * pair + h) * BIAS_PARTS
            in_group = lambda lo: jnp.logical_and(lane_q >= lo, lane_q < lo + BIAS_PARTS)
            aug = jnp.logical_or(in_group(first), in_group(first + BIAS_LANES))
            q_full.append(jnp.concatenate([jnp.where(hmask, q, jnp.zeros_like(q)),
                                           jnp.where(aug, qa, jnp.zeros_like(qa))], axis=1))
        subs = [slice(b * tq, (b + 1) * tq) for b in range(len(kinds))]
        k_full = [jnp.concatenate([k_ref[sl, :], ka_ref[sl, :]], axis=1) for sl in subs]
        ones_rows = jnp.ones((ONES_ROWS, tq), BF16)
        v_aug = [jnp.concatenate([vt_ref[:, sl], ones_rows], axis=0) for sl in subs]
        scores = lambda b: [_mm([k_full[b]], [q_full[h]], "nt") for h in range(2)]
        s = [scores(b) for b in range(min(AHEAD, len(kinds)))]
        causal = lax.broadcasted_iota(jnp.int32, (tq, tq), 0) <= lax.broadcasted_iota(jnp.int32, (tq, tq), 1)
        m = [m_ref[h] for h in range(2)]
        acc = [acc_ref[h] for h in range(2)]
        for b, diagonal in enumerate(kinds):
            if b + AHEAD < len(kinds):
                s.append(scores(b + AHEAD))
            sb = [jnp.where(causal, x, -jnp.inf) for x in s[b]] if diagonal else s[b]
            if running_max:
                m_new = [jnp.maximum(m[h], jnp.max(sb[h], axis=0, keepdims=True)) for h in range(2)]
                p = [jnp.exp2(sb[h] - m_new[h]).astype(BF16) for h in range(2)]
                acc = [jnp.exp2(m[h] - m_new[h]) * acc[h] for h in range(2)]
                m = m_new
            else:
                p = [jnp.exp2(sb[h]).astype(BF16) for h in range(2)]
            pv = [_mm([v_aug[b]], [p[h]]) for h in range(2)]
            acc = [acc[h] + pv[h] for h in range(2)]
        for h in range(2):
            acc_ref[h] = acc[h]
            if running_max:
                m_ref[h] = m[h]

    def finish():
        row = lax.broadcasted_iota(jnp.int32, (LANES, tq), 0)
        o = [acc_ref[h, 0:LANES, :] / acc_ref[h, LANES:LANES + 1, :] for h in range(2)]
        o_ref[...] = jnp.where(row < HEAD_DIM, o[0], o[1]).T.astype(o_ref.dtype)

    @pl.when(kind == 0)
    def _():
        step((False,) * KEY_BLOCKS)

    for n_full in range(KEY_BLOCKS):
        @pl.when(kind == 1 + n_full)
        def _(n_full=n_full):
            step((False,) * n_full + (True,))
            finish()


def _fox_attn(q, qa, k, ka, vt, tq, running_max):
    m = q.shape[0]
    tq = min(tq, m)
    nq = m // tq
    g = KEY_BLOCKS
    assert nq % g == 0, "key blocks are fetched in groups"
    steps = []
    for i in range(nq):
        steps += [(i, j, 0) for j in range(i // g)] + [(i, i // g, 1 + i % g)]
    qi_tab, kb_tab, kind_tab = (jnp.asarray([st[c] for st in steps], jnp.int32) for c in range(3))
    qspec = pl.BlockSpec((tq, LANES), lambda p, t, qi, kb, kind: (qi[t], p))
    qaspec = pl.BlockSpec((tq, LANES), lambda p, t, qi, kb, kind: (qi[t], 0))
    kspec = pl.BlockSpec((g * tq, LANES), lambda p, t, qi, kb, kind: (kb[t], p))
    kaspec = pl.BlockSpec((g * tq, LANES), lambda p, t, qi, kb, kind: (kb[t], 0))
    vtspec = pl.BlockSpec((LANES, g * tq), lambda p, t, qi, kb, kind: (p, kb[t]))
    return pl.pallas_call(
        functools.partial(_fox_attn_kernel, running_max=running_max),
        grid_spec=pltpu.PrefetchScalarGridSpec(
            num_scalar_prefetch=3,
            grid=(N_PAIRS, len(steps)),
            in_specs=[qspec, qaspec, kspec, kaspec, vtspec],
            out_specs=qspec,
            scratch_shapes=[pltpu.VMEM((2, 1, tq), F32), pltpu.VMEM((2, LANES + ONES_ROWS, tq), F32)],
        ),
        out_shape=jax.ShapeDtypeStruct((m, WIDTH), BF16),
        compiler_params=_params(("parallel", "arbitrary")),
    )(qi_tab, kb_tab, kind_tab, q, qa, k, ka, vt)


def _mem_kv_kernel(mem_ref, g_ref, w_ref, o_ref):
    x = mem_ref[...]
    ms = jnp.mean(x * x, axis=-1, keepdims=True)
    xn = (x * lax.rsqrt(ms + NORM_EPS) * g_ref[...]).astype(BF16)
    o_ref[...] = _dot(xn, w_ref[...]).astype(o_ref.dtype)


def _mem_kv(mem, ln_mem, w_ckv):
    n, d = mem.shape
    ins = [mem, ln_mem.reshape(1, d), w_ckv]
    return pl.pallas_call(
        _mem_kv_kernel,
        grid=(1,),
        in_specs=[pl.BlockSpec(a.shape, lambda i: (0, 0)) for a in ins],
        out_specs=pl.BlockSpec((n, w_ckv.shape[1]), lambda i: (0, 0)),
        out_shape=jax.ShapeDtypeStruct((n, w_ckv.shape[1]), BF16),
        compiler_params=_params(("arbitrary",)),
    )(*ins)


def _post_kernel(ya_ref, yb_ref, pg_ref, x_ref, wa_ref, wb_ref, wo_ref, lnc_ref, wcq_ref, kv_ref, wco_ref, o_ref):
    d = x_ref.shape[1]
    xw = X_HEADS * X_HEAD_DIM
    ga = pg_ref[:, 0:d].astype(F32)
    gb = pg_ref[:, d:2 * d].astype(F32)
    merged = ga * _dot(ya_ref[...], wa_ref[...]) + gb * _dot(yb_ref[...], wb_ref[...])
    h = x_ref[...] + _dot(merged.astype(BF16), wo_ref[...])

    ms = jnp.mean(h * h, axis=-1, keepdims=True)
    hn = (h * lax.rsqrt(ms + NORM_EPS) * lnc_ref[...]).astype(BF16)
    q = _dot(hn, wcq_ref[...])
    outs = []
    for hd in range(X_HEADS):
        cols = slice(hd * X_HEAD_DIM, (hd + 1) * X_HEAD_DIM)
        vcols = slice(xw + hd * X_HEAD_DIM, xw + (hd + 1) * X_HEAD_DIM)
        s = _dot_t(q[:, cols].astype(BF16), kv_ref[:, cols]) * (X_HEAD_DIM ** -0.5)
        s = s - jnp.max(s, axis=1, keepdims=True)
        e = jnp.exp(s)
        pr = e / jnp.sum(e, axis=1, keepdims=True)
        outs.append(_dot(pr.astype(BF16), kv_ref[:, vcols]))
    o = jnp.concatenate(outs, axis=1).astype(BF16)
    o_ref[...] = h + _dot(o, wco_ref[...])


def _post(ya, yb, pg, x, wa, wb, wo, ln_cross, wcq, kv, wco, tm):
    m, d = x.shape
    tm = min(tm, m)
    lnc = ln_cross.reshape(1, d)
    rows = lambda a: pl.BlockSpec((tm, a.shape[1]), lambda i: (i, 0))
    const = lambda a: pl.BlockSpec(a.shape, lambda i: (0, 0), pipeline_mode=pl.Buffered(1))
    return pl.pallas_call(
        _post_kernel,
        grid=(m // tm,),
        in_specs=[rows(ya), rows(yb), rows(pg), rows(x)] + [const(a) for a in (wa, wb, wo, lnc, wcq, kv, wco)],
        out_specs=pl.BlockSpec((tm, d), lambda i: (i, 0)),
        out_shape=jax.ShapeDtypeStruct((m, d), F32),
        compiler_params=_params(("parallel",)),
    )(ya, yb, pg, x, wa, wb, wo, lnc, wcq, kv, wco)


def _mlp_kernel(h_ref, g_ref, wu_ref, wd_ref, gf_ref, o_ref, xn_ref):
    f = pl.program_id(1)

    @pl.when(f == 0)
    def _():
        h = h_ref[...]
        ms = jnp.mean(h * h, axis=-1, keepdims=True)
        xn_ref[...] = (h * lax.rsqrt(ms + NORM_EPS) * g_ref[...]).astype(xn_ref.dtype)
        o_ref[...] = jnp.zeros_like(o_ref)

    u = jnp.maximum(_dot(xn_ref[...], wu_ref[...]), 0.0)
    o_ref[...] += _dot((u * u).astype(BF16), wd_ref[...])

    @pl.when(f == pl.num_programs(1) - 1)
    def _():
        h = h_ref[...] + o_ref[...]
        ms = jnp.mean(h * h, axis=-1, keepdims=True)
        o_ref[...] = h * lax.rsqrt(ms + NORM_EPS) * gf_ref[...]


def _mlp(h, ln_mlp, w_up, w_down, ln_final, tm, tf):
    m, d = h.shape
    dff = w_up.shape[1]
    tm, tf = min(tm, m), min(tf, dff)
    return pl.pallas_call(
        _mlp_kernel,
        grid=(m // tm, dff // tf),
        in_specs=[pl.BlockSpec((tm, d), lambda i, f: (i, 0)),
                  pl.BlockSpec((1, d), lambda i, f: (0, 0)),
                  pl.BlockSpec((d, tf), lambda i, f: (0, f)),
                  pl.BlockSpec((tf, d), lambda i, f: (f, 0)),
                  pl.BlockSpec((1, d), lambda i, f: (0, 0))],
        out_specs=pl.BlockSpec((tm, d), lambda i, f: (i, 0)),
        out_shape=jax.ShapeDtypeStruct((m, d), F32),
        scratch_shapes=[pltpu.VMEM((tm, d), BF16)],
        compiler_params=_params(("parallel", "arbitrary")),
    )(h, ln_mlp.reshape(1, d), w_up, w_down, ln_final.reshape(1, d))


def _pad_rows(a, height):
    return jnp.pad(a, ((0, height - a.shape[0]), (0, 0)))


def _layer(h, mem, ln_mix, w_in, rw_mu, rw_w0, rw_w_up, rw_a0, rw_a_up, rw_g_up, rw_k_k, rw_k_a, rw_r_k,
           rw_ln_w, rw_ln_b, fox_b_f, fox_q_norm, fox_k_norm, w_proj_a, w_proj_b, w_out, ln_cross, ln_mem,
           w_cq, w_ckv, w_co, ln_mlp, w_up, w_down, ln_final):
    w_t = w_in.T
    fox_row0 = RW_COLS
    gate_row0 = RW_COLS + 3 * WIDTH + N_HEADS
    mu_p = jnp.pad(rw_mu, (0, RW_PAD - RW_COLS))
    bf_rep = jnp.pad(jnp.tile(jnp.repeat(fox_b_f, BIAS_PARTS), 2), (0, LANES - 2 * BIAS_LANES)).reshape(1, LANES)
    wup_p = _pad_rows(rw_w_up, LORA_PAD).astype(BF16)
    aup_p = jnp.pad(rw_a_up, ((DECAY_LORA, 0), (0, 0))).astype(BF16)
    gup_p = _pad_rows(rw_g_up, GATE_PAD).astype(BF16)

    xn = _rmsnorm_bf16(h, ln_mix, 512)
    p_rw = _project(xn, w_t, 0, RW_PAD, 1024, 1792)
    p_fox = _project(xn, w_t, fox_row0, FOX_PAD, 1024, 1792)
    gates = _project(xn, w_t, gate_row0, w_t.shape[0] - gate_row0, 1024, 2048, BF16, gate=True)

    y_a = _rw_branch(p_rw, mu_p, rw_w0, wup_p, rw_a0, aup_p, gup_p, rw_k_k, rw_k_a, rw_r_k.reshape(-1), rw_ln_w,
                     rw_ln_b, 4 * CHUNK)

    fq, fk, fvt, fqa, fka, fbound = _fox_prep(p_fox, fox_q_norm, fox_k_norm, bf_rep, 512)
    y_b = lax.cond(jnp.max(fbound) < SAFE_LOGIT_BOUND,
                   functools.partial(_fox_attn, tq=512, running_max=False),
                   functools.partial(_fox_attn, tq=512, running_max=True), fq, fqa, fk, fka, fvt)

    kv = _mem_kv(mem, ln_mem, w_ckv.astype(BF16))
    h = _post(y_a, y_b, gates, h, w_proj_a.astype(BF16), w_proj_b.astype(BF16), w_out.astype(BF16), ln_cross,
              w_cq.astype(BF16), kv, w_co.astype(BF16), 512)
    return _mlp(h, ln_mlp, w_up.astype(BF16), w_down.astype(BF16), ln_final, 512, 1024)


def kernel(x, mem, ln_mix, w_in, rw_mu, rw_w0, rw_w_up, rw_a0, rw_a_up, rw_g_up, rw_k_k, rw_k_a, rw_r_k, rw_ln_w, rw_ln_b, fox_b_f, fox_q_norm, fox_k_norm, w_proj_a, w_proj_b, w_out, ln_cross, ln_mem, w_cq, w_ckv, w_co, ln_mlp, w_up, w_down, ln_final):
    b, s, d = x.shape
    assert b == 1 and ln_mix.shape[0] == 1, "single sequence, single layer"
    out = _layer(x[0], mem[0], ln_mix[0], w_in[0], rw_mu[0], rw_w0[0], rw_w_up[0], rw_a0[0], rw_a_up[0], rw_g_up[0],
                 rw_k_k[0], rw_k_a[0], rw_r_k[0], rw_ln_w[0], rw_ln_b[0], fox_b_f[0], fox_q_norm[0], fox_k_norm[0],
                 w_proj_a[0], w_proj_b[0], w_out[0], ln_cross[0], ln_mem[0], w_cq[0], w_ckv[0], w_co[0],
                 ln_mlp[0], w_up[0], w_down[0], ln_final)
    return out.reshape(b, s, d)
```

```python
import functools

import numpy as np

import jax
import jax.numpy as jnp
from jax import lax
from jax.experimental import pallas as pl
from jax.experimental.pallas import tpu as pltpu

F32 = jnp.float32
BF16 = jnp.bfloat16

LANES = 128
MXU_COLS = 256
HEAD_DIM = 64
N_HEADS = 16
WIDTH = N_HEADS * HEAD_DIM
N_PAIRS = WIDTH // LANES
DECAY_LORA, ICLR_LORA, GATE_LORA = 64, 64, 160
LORA_PAD = 128
GATE_PAD = 256
RW_COLS = 3 * WIDTH + DECAY_LORA + ICLR_LORA + GATE_LORA
RW_PAD = 3584
FZ_PAD = 512
FOX_PAD = 3 * WIDTH + FZ_PAD
BOUND_SLACK = 1.02
SAFE_LOGIT_BOUND = 50.0
BIAS_PARTS = 3
BIAS_LANES = BIAS_PARTS * N_HEADS
LOG2E = 1.4426950408889634
KEY_BLOCKS = 8
AHEAD = 2
X_HEADS, X_HEAD_DIM = 4, 128
NORM_EPS = 1e-5
GN_EPS = 64e-5
CHUNK = 64
DECAY_SCALE = -0.6065306597126334
SUB = 16
VMEM_LIMIT = 56 * 1024 * 1024


def _params(sem, vmem=VMEM_LIMIT):
    return pltpu.CompilerParams(dimension_semantics=sem, vmem_limit_bytes=vmem)


def _dot(a, b):
    return jnp.dot(a, b, preferred_element_type=F32)


def _dot_t(a, b):
    return lax.dot_general(a, b, (((1,), (1,)), ((), ())), preferred_element_type=F32)


def _split(x, terms):
    parts, rem = [], x
    for i in range(terms):
        part = rem.astype(BF16)
        parts.append(part)
        if i + 1 < terms:
            rem = rem - part.astype(F32)
    return parts


_DIMS = {"nn": (((1,), (0,)), ((), ())), "nt": (((1,), (1,)), ((), ())), "tn": (((0,), (0,)), ((), ()))}


def _mm(a_parts, b_parts, kind="nn"):
    order = max(len(a_parts), len(b_parts))
    acc = None
    for i, a in enumerate(a_parts):
        for j, b in enumerate(b_parts):
            if i + j < order:
                term = lax.dot_general(a, b, _DIMS[kind], preferred_element_type=F32)
                acc = term if acc is None else acc + term
    return acc


def _bf(x):
    return [x.astype(BF16)]


def _softplus(x):
    return jnp.maximum(x, 0.0) + jnp.log(1.0 + jnp.exp(-jnp.abs(x)))


def _sigmoid(x):
    return 1.0 / (1.0 + jnp.exp(-x))


def _head_ones():
    r = lax.broadcasted_iota(jnp.int32, (LANES, LANES), 0) // HEAD_DIM
    c = lax.broadcasted_iota(jnp.int32, (LANES, LANES), 1) // HEAD_DIM
    return (r == c).astype(F32)


def _stacked_headsum(xs, ones, terms):
    n = xs[0].shape[0]
    out = _mm(_split(jnp.concatenate(xs, axis=0), terms), ones)
    return [out[i * n:(i + 1) * n] for i in range(len(xs))]


def _headsum(x, ones_f32, terms=2):
    chunks = [x[:, c * LANES:(c + 1) * LANES] for c in range(x.shape[1] // LANES)]
    return jnp.concatenate(_stacked_headsum(chunks, [ones_f32.astype(BF16)], terms), axis=1)


def _rmsnorm_kernel(x_ref, g_ref, o_ref):
    x = x_ref[...]
    ms = jnp.mean(x * x, axis=-1, keepdims=True)
    o_ref[...] = (x * lax.rsqrt(ms + NORM_EPS) * g_ref[...]).astype(o_ref.dtype)


def _rmsnorm_bf16(x, g, tm):
    m, d = x.shape
    tm = min(tm, m)
    return pl.pallas_call(
        _rmsnorm_kernel,
        grid=(m // tm,),
        in_specs=[pl.BlockSpec((tm, d), lambda i: (i, 0)), pl.BlockSpec((1, d), lambda i: (0, 0))],
        out_specs=pl.BlockSpec((tm, d), lambda i: (i, 0)),
        out_shape=jax.ShapeDtypeStruct((m, d), BF16),
        compiler_params=_params(("parallel",)),
    )(x, g.reshape(1, d))


def _proj_kernel(a_ref, wt_ref, o_ref, w_bf16, *, gate):
    @pl.when(pl.program_id(1) == 0)
    def _():
        for c0 in range(0, wt_ref.shape[0], MXU_COLS):
            w_bf16[:, c0:c0 + MXU_COLS] = wt_ref[c0:c0 + MXU_COLS, :].T.astype(BF16)

    acc = _dot(a_ref[...], w_bf16[...])
    o_ref[...] = (_sigmoid(acc) if gate else acc).astype(o_ref.dtype)


def _project(a, w_t, row0, n, bm, bn, out_dtype=F32, gate=False):
    m, k = a.shape
    bm = min(bm, m)
    assert row0 % 8 == 0 and n % bn == 0 and bn % MXU_COLS == 0 and m % bm == 0 and row0 + n <= w_t.shape[0]
    return pl.pallas_call(
        functools.partial(_proj_kernel, gate=gate),
        grid=(n // bn, m // bm),
        in_specs=[pl.BlockSpec((bm, k), lambda j, i: (i, 0)),
                  pl.BlockSpec((pl.Element(bn), pl.Element(k)), lambda j, i: (pl.multiple_of(row0 + j * bn, 8), 0),
                               pipeline_mode=pl.Buffered(1))],
        out_specs=pl.BlockSpec((bm, bn), lambda j, i: (i, j)),
        out_shape=jax.ShapeDtypeStruct((m, n), out_dtype),
        scratch_shapes=[pltpu.VMEM((k, bn), BF16)],
        compiler_params=_params(("arbitrary", "arbitrary")),
    )(a, w_t)


def _rw_inputs(p, prev_row, mu, w0, wup, a0, aup, gup, k_k, k_a, ones):
    rows = lax.broadcasted_iota(jnp.int32, p.shape, 0)
    prev = jnp.where(rows == 0, prev_row, pltpu.roll(p, 1, 0))
    ps = p + (prev - p) * mu
    w = WIDTH
    r, k, v = ps[:, 0:w], ps[:, w:2 * w], ps[:, 2 * w:3 * w]
    o = 3 * w
    lora = ps[:, o:o + LORA_PAD]
    gd = ps[:, o + LORA_PAD:o + LORA_PAD + GATE_PAD]
    z = w0 + _mm(_bf(jnp.tanh(lora)), [wup])
    lw = DECAY_SCALE * _sigmoid(z)
    iclr = _sigmoid(a0 + _mm(_bf(lora), [aup]))
    gate = _mm(_bf(_sigmoid(gd)), [gup])
    kk = k * k_k
    kk2 = kk * kk
    ss = _stacked_headsum([kk2[:, c * LANES:(c + 1) * LANES] for c in range(N_PAIRS)], ones, 2)
    kk = kk * lax.rsqrt(jnp.maximum(jnp.concatenate(ss, axis=1), 1e-24))
    return r, k * (1.0 + (iclr - 1.0) * k_a), v, lw, -kk, kk * iclr, gate


def _pair_diag(x, head0):
    xb = x.astype(BF16)
    zero = jnp.zeros_like(xb)
    return [jnp.concatenate([jnp.where(head0, xb, zero), jnp.where(head0, zero, xb)], axis=0)]


def _unit_lower_inverse(a_list, sub_mask, eye, head0):
    c = CHUNK
    d = [jnp.where(sub_mask, a, 0.0) for a in a_list]
    e = [a - x for a, x in zip(a_list, d)]
    p = [eye + x for x in d]
    diag = lambda xs: [_pair_diag(x, head0) for x in xs]
    stack = lambda xs, ys: [_bf(jnp.concatenate([x, y], axis=0)) for x, y in zip(xs, ys)]
    dp = [_mm(_bf(x), y) for x, y in zip(d, diag(d))]
    for _ in range(SUB.bit_length() - 3):
        both = [_mm(x, y) for x, y in zip(stack(dp, p), diag(dp))]
        dp = [x[:c] for x in both]
        p = [x + y[c:] for x, y in zip(p, both)]
    p = [x + _mm(_bf(x), y) for x, y in zip(p, diag(dp))]
    f = [_mm(_bf(x), y) for x, y in zip(p, diag(e))]
    g = [eye + x for x in f]
    fp = [_mm(_bf(x), y) for x, y in zip(f, diag(f))]
    for _ in range((CHUNK // SUB).bit_length() - 3):
        both = [_mm(x, y) for x, y in zip(stack(fp, g), diag(fp))]
        fp = [x[:c] for x in both]
        g = [x + y[c:] for x, y in zip(g, both)]
    g = [x + _mm(_bf(x), y) for x, y in zip(g, diag(fp))]
    return [_mm(_bf(x), y) for x, y in zip(g, diag(p))]


def _rw_kernel(p_ref, mu_ref, w0_ref, wup_ref, a0_ref, aup_ref, gup_ref, kkw_ref, ka_ref, rk_ref, lnw_ref, lnb_ref,
               o_ref, h_ref, carry_ref):
    c = CHUNK
    t_tile = p_ref.shape[0]
    n_chunks = t_tile // c

    @pl.when(pl.program_id(0) == 0)
    def _():
        h_ref[...] = jnp.zeros_like(h_ref)
        carry_ref[...] = jnp.zeros_like(carry_ref)

    lane = lax.broadcasted_iota(jnp.int32, (c, LANES), 1)
    head0 = lane < HEAD_DIM
    ri = lax.broadcasted_iota(jnp.int32, (c, LANES), 0)
    ci = jnp.bitwise_and(lane, HEAD_DIM - 1)
    strict = ri > ci
    incl = ri >= ci
    eye = (ri == ci).astype(F32)
    sub_mask = (ri // SUB) == (ci // SUB)
    tril = _bf(lax.broadcasted_iota(jnp.int32, (c, c), 0) >= lax.broadcasted_iota(jnp.int32, (c, c), 1))
    ones_f = _head_ones()
    ones = _bf(ones_f)
    pair_mask = ones_f > 0.5
    diag = lambda x: _pair_diag(x, head0)
    units = [(s, p) for s in range(n_chunks) for p in range(N_PAIRS)]
    rows = lambda s: slice(s * c, (s + 1) * c)
    cols = lambda p: slice(p * LANES, (p + 1) * LANES)

    p_tile = p_ref[...]
    r_all, k_all, v_all, lw_all, a_all, b_all, gate_all = _rw_inputs(
        p_tile, carry_ref[...], mu_ref[...], w0_ref[...], wup_ref[...], a0_ref[...], aup_ref[...], gup_ref[...],
        kkw_ref[...], ka_ref[...], ones)
    carry_ref[...] = p_tile[t_tile - 1:t_tile, :]

    pre = []
    for s in range(n_chunks):
        lw, k, b = lw_all[rows(s), :], k_all[rows(s), :], b_all[rows(s), :]
        cl = _mm(tril, _split(lw, 3))
        cl_end = cl[c - 1:c, :]
        e_neg = jnp.exp(-cl)
        e_end = jnp.exp(cl_end - cl)
        pre.append(dict(rt=r_all[rows(s), :] * jnp.exp(cl), at=a_all[rows(s), :] * jnp.exp(cl - lw),
                        bt=b * e_neg, kt=k * e_neg, b_end=b * e_end, k_end=k * e_end, decay_end=jnp.exp(cl_end)))
    get = lambda name: [pre[s][name][:, cols(p)] for s, p in units]
    tile = lambda x: [x[rows(s), cols(p)] for s, p in units]
    r, k, v, gate = tile(r_all), tile(k_all), tile(v_all), tile(gate_all)
    rt, at = get("rt"), get("at")
    stack = lambda xs, ys: [jnp.concatenate([x, y], axis=0) for x, y in zip(xs, ys)]
    bk_end = [_bf(x) for x in stack(get("b_end"), get("k_end"))]
    v_diag = [diag(x) for x in v]
    mask2 = jnp.concatenate([strict, incl], axis=0)

    lhs = [_bf(x) for x in stack(at, rt)]
    sb = [_mm(x, diag(y), "nt") for x, y in zip(lhs, get("bt"))]
    sk = [_mm(x, diag(y), "nt") for x, y in zip(lhs, get("kt"))]
    skv = [_mm(_bf(jnp.where(mask2, x, 0.0)), y) for x, y in zip(sk, v_diag)]
    rb = [_bf(jnp.where(incl, x[c:], 0.0)) for x in sb]
    t_inv = [_bf(x) for x in _unit_lower_inverse([jnp.where(strict, x[:c], 0.0) for x in sb], sub_mask, eye, head0)]
    w = [_mm(x, diag(y)) for x, y in zip(t_inv, at)]
    tk = [_mm(x, diag(y[:c])) for x, y in zip(t_inv, skv)]
    s0_lhs = [_bf(x) for x in stack(w, rt)]

    ht = [h_ref[p] for p in range(N_PAIRS)]
    u, rs0 = [], []
    for s in range(n_chunks):
        i0 = s * N_PAIRS
        s0 = [_mm(s0_lhs[i0 + p], _bf(ht[p]), "nt") for p in range(N_PAIRS)]
        u += [s0[p][:c] + tk[i0 + p] for p in range(N_PAIRS)]
        rs0 += [s0[p][c:] for p in range(N_PAIRS)]
        upd = [_mm(_bf(jnp.concatenate([u[i0 + p], v[i0 + p]], axis=0)), bk_end[i0 + p], "tn") for p in range(N_PAIRS)]
        ht = [ht[p] * pre[s]["decay_end"][:, cols(p)] + jnp.where(pair_mask, upd[p], 0.0) for p in range(N_PAIRS)]
    for p in range(N_PAIRS):
        h_ref[p] = ht[p]

    y = [a + _mm(x, diag(z)) + b[c:] for a, x, z, b in zip(rs0, rb, u, skv)]
    rkr = [r[i] * k[i] * rk_ref[:, cols(p)] for i, (s, p) in enumerate(units)]
    sums = _stacked_headsum(y + rkr, ones, 2)
    dlt = [a - b * (1.0 / HEAD_DIM) for a, b in zip(y, sums[:len(units)])]
    var = _stacked_headsum([x * x for x in dlt], ones, 2)
    for i, (s, p) in enumerate(units):
        yn = dlt[i] * lax.rsqrt(var[i] * (1.0 / HEAD_DIM) + GN_EPS) * lnw_ref[:, cols(p)] + lnb_ref[:, cols(p)]
        o_ref[rows(s), cols(p)] = ((yn + sums[len(units) + i] * v[i]) * gate[i]).astype(o_ref.dtype)


def _rw_branch(p_rw, mu_p, w0, wup_p, a0, aup_p, gup_p, k_k, k_a, r_k, ln_w, ln_b, t_tile):
    m = p_rw.shape[0]
    t_tile = min(t_tile, m)
    row = lambda a: a.reshape(1, -1)
    full = lambda a: pl.BlockSpec(a.shape, lambda t: (0, 0))
    params = [row(mu_p), row(w0), wup_p, row(a0), aup_p, gup_p, row(k_k), row(k_a), row(r_k), row(ln_w), row(ln_b)]
    return pl.pallas_call(
        _rw_kernel,
        grid=(m // t_tile,),
        in_specs=[pl.BlockSpec((t_tile, RW_PAD), lambda t: (t, 0))] + [full(a) for a in params],
        out_specs=pl.BlockSpec((t_tile, WIDTH), lambda t: (t, 0)),
        out_shape=jax.ShapeDtypeStruct((m, WIDTH), BF16),
        scratch_shapes=[pltpu.VMEM((N_PAIRS, LANES, LANES), F32), pltpu.VMEM((1, RW_PAD), F32)],
        compiler_params=_params(("arbitrary",)),
    )(p_rw, *params)


def _fox_prep_kernel(p_ref, qn_ref, kn_ref, bf_ref, part_ref, sel_ref, selz_ref,
                     q_out, k_out, vt_out, qa_out, ka_out, bound_out, carry_ref, kmax_ref, bmax_ref):
    tm = p_ref.shape[0]

    @pl.when(pl.program_id(0) == 0)
    def _():
        carry_ref[...] = jnp.zeros_like(carry_ref)
        kmax_ref[...] = jnp.zeros_like(kmax_ref)
        bmax_ref[...] = jnp.zeros_like(bmax_ref)

    w = WIDTH
    ones = _head_ones()
    q, k = p_ref[:, 0:w], p_ref[:, w:2 * w]
    q = q * lax.rsqrt(_headsum(q * q, ones) * (1.0 / HEAD_DIM) + NORM_EPS) * qn_ref[...]
    k = k * lax.rsqrt(_headsum(k * k, ones) * (1.0 / HEAD_DIM) + NORM_EPS) * kn_ref[...]
    qb = (q * (HEAD_DIM ** -0.5 * LOG2E)).astype(BF16)
    kb = k.astype(BF16)
    q_out[...] = qb
    k_out[...] = kb
    vt_out[...] = p_ref[:, 2 * w:3 * w].T.astype(vt_out.dtype)

    qf, kf = qb.astype(F32), kb.astype(F32)
    k_sq = _headsum(kf * kf, ones, 1)
    kmax_sq = jnp.maximum(kmax_ref[...], jnp.max(k_sq, axis=0, keepdims=True))
    kmax_ref[...] = kmax_sq
    bound_sq = _headsum(qf * qf, ones, 1) * kmax_sq
    bsel = jnp.sqrt(_mm(_bf(bound_sq), [sel_ref[...]])) * BOUND_SLACK

    fz = _mm(_split(p_ref[:, 3 * w:3 * w + LANES], 3), [selz_ref[...]])
    logf = -_softplus(-(fz + bf_ref[...]))
    ri = lax.broadcasted_iota(jnp.int32, (tm, tm), 0)
    ci = lax.broadcasted_iota(jnp.int32, (tm, tm), 1)
    cum = _mm(_bf(ri >= ci), _split(logf, 3)) + carry_ref[...]
    carry_ref[...] = cum[tm - 1:tm, :]

    part = jnp.broadcast_to(part_ref[...], (tm, LANES))

    def pick(x):
        parts = [t.astype(F32) for t in _split(x, BIAS_PARTS)]
        return jnp.where(part == 0, parts[0], jnp.where(part == 1, parts[1], parts[2]))

    lane = lax.broadcasted_iota(jnp.int32, (tm, LANES), 1)
    low = lane < BIAS_LANES
    high = jnp.logical_and(lane >= BIAS_LANES, lane < 2 * BIAS_LANES)
    ka_out[...] = jnp.where(low, pick(cum * (-LOG2E)), jnp.where(high, 1.0, 0.0)).astype(ka_out.dtype)
    qa_out[...] = jnp.where(low, 1.0, jnp.where(high, pick(cum * LOG2E - bsel), 0.0)).astype(qa_out.dtype)
    bmax = jnp.maximum(bmax_ref[...], jnp.max(bsel, axis=0, keepdims=True))
    bmax_ref[...] = bmax
    bound_out[...] = jnp.broadcast_to(bmax, bound_out.shape)


def _fox_prep(p_fox, q_norm, k_norm, bf_rep, tm):
    m = p_fox.shape[0]
    tm = min(tm, m)
    qn = jnp.tile(q_norm, N_HEADS).reshape(1, WIDTH)
    kn = jnp.tile(k_norm, N_HEADS).reshape(1, WIDTH)
    part = (jnp.arange(LANES, dtype=jnp.int32) % BIAS_PARTS).reshape(1, LANES)
    heads = np.arange(N_HEADS)
    sel = np.zeros((WIDTH, LANES), np.float32)
    for j in range(BIAS_PARTS):
        sel[heads * HEAD_DIM, BIAS_LANES + BIAS_PARTS * heads + j] = 1.0
    sel = jnp.asarray(sel, BF16)
    selz = np.zeros((LANES, LANES), np.float32)
    for j in range(BIAS_PARTS):
        for group in (0, BIAS_LANES):
            selz[heads, group + BIAS_PARTS * heads + j] = 1.0
    selz = jnp.asarray(selz, BF16)
    full = lambda a: pl.BlockSpec(a.shape, lambda i: (0, 0))
    rows = lambda width: pl.BlockSpec((tm, width), lambda i: (i, 0))
    act = jax.ShapeDtypeStruct((m, WIDTH), BF16)
    aug = jax.ShapeDtypeStruct((m, LANES), BF16)
    return pl.pallas_call(
        _fox_prep_kernel,
        grid=(m // tm,),
        in_specs=[rows(FOX_PAD), full(qn), full(kn), full(bf_rep), full(part), full(sel), full(selz)],
        out_specs=[rows(WIDTH), rows(WIDTH), pl.BlockSpec((WIDTH, tm), lambda i: (0, i)), rows(LANES), rows(LANES),
                   pl.BlockSpec((8, LANES), lambda i: (0, 0))],
        out_shape=[act, act, jax.ShapeDtypeStruct((WIDTH, m), BF16), aug, aug, jax.ShapeDtypeStruct((8, LANES), F32)],
        scratch_shapes=[pltpu.VMEM((1, LANES), F32), pltpu.VMEM((1, WIDTH), F32), pltpu.VMEM((1, LANES), F32)],
        compiler_params=_params(("arbitrary",)),
    )(p_fox, qn, kn, bf_rep, part, sel, selz)


def _fox_attn_kernel(qi_ref, kb_ref, kind_ref, q_ref, qa_ref, k_ref, ka_ref, vt_ref, o_ref, m_ref, acc_ref, den_ref, *,
                     running_max):
    tq = q_ref.shape[0]
    pair, step_id = pl.program_id(0), pl.program_id(1)
    kb, kind = kb_ref[step_id], kind_ref[step_id]

    @pl.when(kb == 0)
    def _():
        m_ref[...] = jnp.full_like(m_ref, -jnp.inf)
        acc_ref[...] = jnp.zeros_like(acc_ref)
        den_ref[...] = jnp.zeros_like(den_ref)

    def step(kinds):
        q, qa = q_ref[...], qa_ref[...]
        lane_q = lax.broadcasted_iota(jnp.int32, (tq, LANES), 1)
        q_full = []
        for h in range(2):
            hmask = (lane_q < HEAD_DIM) if h == 0 else (lane_q >= HEAD_DIM)
            first = (2 * pair + h) * BIAS_PARTS
            in_group = lambda lo: jnp.logical_and(lane_q >= lo, lane_q < lo + BIAS_PARTS)
            aug = jnp.logical_or(in_group(first), in_group(first + BIAS_LANES))
            q_full.append(jnp.concatenate([jnp.where(hmask, q, jnp.zeros_like(q)),
                                           jnp.where(aug, qa, jnp.zeros_like(qa))], axis=1))
        subs = [slice(b * tq, (b + 1) * tq) for b in range(len(kinds))]
        k_full = [jnp.concatenate([k_ref[sl, :], ka_ref[sl, :]], axis=1) for sl in subs]
        v_t = [vt_ref[:, sl] for sl in subs]
        scores = lambda b: [_mm([k_full[b]], [q_full[h]], "nt") for h in range(2)]
        s = [scores(b) for b in range(min(AHEAD, len(kinds)))]
        causal = lax.broadcasted_iota(jnp.int32, (tq, tq), 0) <= lax.broadcasted_iota(jnp.int32, (tq, tq), 1)
        m = [m_ref[h] for h in range(2)]
        acc = [acc_ref[h] for h in range(2)]
        den = [den_ref[h] for h in range(2)]
        for b, diagonal in enumerate(kinds):
            if b + AHEAD < len(kinds):
                s.append(scores(b + AHEAD))
            sb = [jnp.where(causal, x, -jnp.inf) for x in s[b]] if diagonal else s[b]
            if running_max:
                m_new = [jnp.maximum(m[h], jnp.max(sb[h], axis=0, keepdims=True)) for h in range(2)]
                p = [jnp.exp2(sb[h] - m_new[h]) for h in range(2)]
                alpha = [jnp.exp2(m[h] - m_new[h]) for h in range(2)]
                acc = [alpha[h] * acc[h] for h in range(2)]
                den = [alpha[h] * den[h] for h in range(2)]
                m = m_new
            else:
                p = [jnp.exp2(sb[h]) for h in range(2)]
            den = [den[h] + jnp.sum(p[h].reshape(tq // 8, 8, tq), axis=0) for h in range(2)]
            pv = [_mm([v_t[b]], _bf(p[h])) for h in range(2)]
            acc = [acc[h] + pv[h] for h in range(2)]
        for h in range(2):
            acc_ref[h] = acc[h]
            den_ref[h] = den[h]
            if running_max:
                m_ref[h] = m[h]

    def finish():
        row = lax.broadcasted_iota(jnp.int32, (LANES, tq), 0)
        o = [acc_ref[h] / jnp.sum(den_ref[h], axis=0, keepdims=True) for h in range(2)]
        o_ref[...] = jnp.where(row < HEAD_DIM, o[0], o[1]).T.astype(o_ref.dtype)

    @pl.when(kind == 0)
    def _():
        step((False,) * KEY_BLOCKS)

    for n_full in range(KEY_BLOCKS):
        @pl.when(kind == 1 + n_full)
        def _(n_full=n_full):
            step((False,) * n_full + (True,))
            finish()


def _fox_attn(q, qa, k, ka, vt, tq, running_max):
    m = q.shape[0]
    tq = min(tq, m)
    nq = m // tq
    g = KEY_BLOCKS
    assert nq % g == 0, "key blocks are fetched in groups"
    steps = []
    for i in range(nq):
        steps += [(i, j, 0) for j in range(i // g)] + [(i, i // g, 1 + i % g)]
    qi_tab, kb_tab, kind_tab = (jnp.asarray([st[c] for st in steps], jnp.int32) for c in range(3))
    qspec = pl.BlockSpec((tq, LANES), lambda p, t, qi, kb, kind: (qi[t], p))
    qaspec = pl.BlockSpec((tq, LANES), lambda p, t, qi, kb, kind: (qi[t], 0))
    kspec = pl.BlockSpec((g * tq, LANES), lambda p, t, qi, kb, kind: (kb[t], p))
    kaspec = pl.BlockSpec((g * tq, LANES), lambda p, t, qi, kb, kind: (kb[t], 0))
    vtspec = pl.BlockSpec((LANES, g * tq), lambda p, t, qi, kb, kind: (p, kb[t]))
    return pl.pallas_call(
        functools.partial(_fox_attn_kernel, running_max=running_max),
        grid_spec=pltpu.PrefetchScalarGridSpec(
            num_scalar_prefetch=3,
            grid=(N_PAIRS, len(steps)),
            in_specs=[qspec, qaspec, kspec, kaspec, vtspec],
            out_specs=qspec,
            scratch_shapes=[pltpu.VMEM((2, 1, tq), F32), pltpu.VMEM((2, LANES, tq), F32),
                            pltpu.VMEM((2, 8, tq), F32)],
        ),
        out_shape=jax.ShapeDtypeStruct((m, WIDTH), BF16),
        compiler_params=_params(("parallel", "arbitrary")),
    )(qi_tab, kb_tab, kind_tab, q, qa, k, ka, vt)


def _mem_kv_kernel(mem_ref, g_ref, w_ref, o_ref):
    x = mem_ref[...]
    ms = jnp.mean(x * x, axis=-1, keepdims=True)
    xn = (x * lax.rsqrt(ms + NORM_EPS) * g_ref[...]).astype(BF16)
    o_ref[...] = _dot(xn, w_ref[...]).astype(o_ref.dtype)


def _mem_kv(mem, ln_mem, w_ckv):
    n, d = mem.shape
    ins = [mem, ln_mem.reshape(1, d), w_ckv]
    return pl.pallas_call(
        _mem_kv_kernel,
        grid=(1,),
        in_specs=[pl.BlockSpec(a.shape, lambda i: (0, 0)) for a in ins],
        out_specs=pl.BlockSpec((n, w_ckv.shape[1]), lambda i: (0, 0)),
        out_shape=jax.ShapeDtypeStruct((n, w_ckv.shape[1]), BF16),
        compiler_params=_params(("arbitrary",)),
    )(*ins)


def _post_kernel(ya_ref, yb_ref, pg_ref, x_ref, wa_ref, wb_ref, wo_ref, lnc_ref, wcq_ref, kv_ref, wco_ref, o_ref):
    d = x_ref.shape[1]
    xw = X_HEADS * X_HEAD_DIM
    ga = pg_ref[:, 0:d].astype(F32)
    gb = pg_ref[:, d:2 * d].astype(F32)
    merged = ga * _dot(ya_ref[...], wa_ref[...]) + gb * _dot(yb_ref[...], wb_ref[...])
    h = x_ref[...] + _dot(merged.astype(BF16), wo_ref[...])

    ms = jnp.mean(h * h, axis=-1, keepdims=True)
    hn = (h * lax.rsqrt(ms + NORM_EPS) * lnc_ref[...]).astype(BF16)
    q = _dot(hn, wcq_ref[...])
    outs = []
    for hd in range(X_HEADS):
        cols = slice(hd * X_HEAD_DIM, (hd + 1) * X_HEAD_DIM)
        vcols = slice(xw + hd * X_HEAD_DIM, xw + (hd + 1) * X_HEAD_DIM)
        s = _dot_t(q[:, cols].astype(BF16), kv_ref[:, cols]) * (X_HEAD_DIM ** -0.5)
        s = s - jnp.max(s, axis=1, keepdims=True)
        e = jnp.exp(s)
        pr = e / jnp.sum(e, axis=1, keepdims=True)
        outs.append(_dot(pr.astype(BF16), kv_ref[:, vcols]))
    o = jnp.concatenate(outs, axis=1).astype(BF16)
    o_ref[...] = h + _dot(o, wco_ref[...])


def _post(ya, yb, pg, x, wa, wb, wo, ln_cross, wcq, kv, wco, tm):
    m, d = x.shape
    tm = min(tm, m)
    lnc = ln_cross.reshape(1, d)
    rows = lambda a: pl.BlockSpec((tm, a.shape[1]), lambda i: (i, 0))
    const = lambda a: pl.BlockSpec(a.shape, lambda i: (0, 0), pipeline_mode=pl.Buffered(1))
    return pl.pallas_call(
        _post_kernel,
        grid=(m // tm,),
        in_specs=[rows(ya), rows(yb), rows(pg), rows(x)] + [const(a) for a in (wa, wb, wo, lnc, wcq, kv, wco)],
        out_specs=pl.BlockSpec((tm, d), lambda i: (i, 0)),
        out_shape=jax.ShapeDtypeStruct((m, d), F32),
        compiler_params=_params(("parallel",)),
    )(ya, yb, pg, x, wa, wb, wo, lnc, wcq, kv, wco)


def _mlp_kernel(h_ref, g_ref, wu_ref, wd_ref, gf_ref, o_ref, xn_ref):
    f = pl.program_id(1)

    @pl.when(f == 0)
    def _():
        h = h_ref[...]
        ms = jnp.mean(h * h, axis=-1, keepdims=True)
        xn_ref[...] = (h * lax.rsqrt(ms + NORM_EPS) * g_ref[...]).astype(xn_ref.dtype)
        o_ref[...] = jnp.zeros_like(o_ref)

    u = jnp.maximum(_dot(xn_ref[...], wu_ref[...]), 0.0)
    o_ref[...] += _dot((u * u).astype(BF16), wd_ref[...])

    @pl.when(f == pl.num_programs(1) - 1)
    def _():
        h = h_ref[...] + o_ref[...]
        ms = jnp.mean(h * h, axis=-1, keepdims=True)
        o_ref[...] = h * lax.rsqrt(ms + NORM_EPS) * gf_ref[...]


def _mlp(h, ln_mlp, w_up, w_down, ln_final, tm, tf):
    m, d = h.shape
    dff = w_up.shape[1]
    tm, tf = min(tm, m), min(tf, dff)
    return pl.pallas_call(
        _mlp_kernel,
        grid=(m // tm, dff // tf),
        in_specs=[pl.BlockSpec((tm, d), lambda i, f: (i, 0)),
                  pl.BlockSpec((1, d), lambda i, f: (0, 0)),
                  pl.BlockSpec((d, tf), lambda i, f: (0, f)),
                  pl.BlockSpec((tf, d), lambda i, f: (f, 0)),
                  pl.BlockSpec((1, d), lambda i, f: (0, 0))],
        out_specs=pl.BlockSpec((tm, d), lambda i, f: (i, 0)),
        out_shape=jax.ShapeDtypeStruct((m, d), F32),
        scratch_shapes=[pltpu.VMEM((tm, d), BF16)],
        compiler_params=_params(("parallel", "arbitrary")),
    )(h, ln_mlp.reshape(1, d), w_up, w_down, ln_final.reshape(1, d))


def _pad_rows(a, height):
    return jnp.pad(a, ((0, height - a.shape[0]), (0, 0)))


def _layer(h, mem, ln_mix, w_in, rw_mu, rw_w0, rw_w_up, rw_a0, rw_a_up, rw_g_up, rw_k_k, rw_k_a, rw_r_k,
           rw_ln_w, rw_ln_b, fox_b_f, fox_q_norm, fox_k_norm, w_proj_a, w_proj_b, w_out, ln_cross, ln_mem,
           w_cq, w_ckv, w_co, ln_mlp, w_up, w_down, ln_final):
    w_t = w_in.T
    fox_row0 = RW_COLS
    gate_row0 = RW_COLS + 3 * WIDTH + N_HEADS
    mu_p = jnp.pad(rw_mu, (0, RW_PAD - RW_COLS))
    bf_rep = jnp.pad(jnp.tile(jnp.repeat(fox_b_f, BIAS_PARTS), 2), (0, LANES - 2 * BIAS_LANES)).reshape(1, LANES)
    wup_p = _pad_rows(rw_w_up, LORA_PAD).astype(BF16)
    aup_p = jnp.pad(rw_a_up, ((DECAY_LORA, 0), (0, 0))).astype(BF16)
    gup_p = _pad_rows(rw_g_up, GATE_PAD).astype(BF16)

    xn = _rmsnorm_bf16(h, ln_mix, 512)
    p_rw = _project(xn, w_t, 0, RW_PAD, 1024, 1792)
    p_fox = _project(xn, w_t, fox_row0, FOX_PAD, 1024, 1792)
    gates = _project(xn, w_t, gate_row0, w_t.shape[0] - gate_row0, 1024, 2048, BF16, gate=True)

    y_a = _rw_branch(p_rw, mu_p, rw_w0, wup_p, rw_a0, aup_p, gup_p, rw_k_k, rw_k_a, rw_r_k.reshape(-1), rw_ln_w,
                     rw_ln_b, 8 * CHUNK)

    fq, fk, fvt, fqa, fka, fbound = _fox_prep(p_fox, fox_q_norm, fox_k_norm, bf_rep, 512)
    y_b = lax.cond(jnp.max(fbound) < SAFE_LOGIT_BOUND,
                   functools.partial(_fox_attn, tq=512, running_max=False),
                   functools.partial(_fox_attn, tq=512, running_max=True), fq, fqa, fk, fka, fvt)

    kv = _mem_kv(mem, ln_mem, w_ckv.astype(BF16))
    h = _post(y_a, y_b, gates, h, w_proj_a.astype(BF16), w_proj_b.astype(BF16), w_out.astype(BF16), ln_cross,
              w_cq.astype(BF16), kv, w_co.astype(BF16), 512)
    return _mlp(h, ln_mlp, w_up.astype(BF16), w_down.astype(BF16), ln_final, 512, 1024)


def kernel(x, mem, ln_mix, w_in, rw_mu, rw_w0, rw_w_up, rw_a0, rw_a_up, rw_g_up, rw_k_k, rw_k_a, rw_r_k, rw_ln_w, rw_ln_b, fox_b_f, fox_q_norm, fox_k_norm, w_proj_a, w_proj_b, w_out, ln_cross, ln_mem, w_cq, w_ckv, w_co, ln_mlp, w_up, w_down, ln_final):
    b, s, d = x.shape
    assert b == 1 and ln_mix.shape[0] == 1, "single sequence, single layer"
    out = _layer(x[0], mem[0], ln_mix[0], w_in[0], rw_mu[0], rw_w0[0], rw_w_up[0], rw_a0[0], rw_a_up[0], rw_g_up[0],
                 rw_k_k[0], rw_k_a[0], rw_r_k[0], rw_ln_w[0], rw_ln_b[0], fox_b_f[0], fox_q_norm[0], fox_k_norm[0],
                 w_proj_a[0], w_proj_b[0], w_out[0], ln_cross[0], ln_mem[0], w_cq[0], w_ckv[0], w_co[0],
                 ln_mlp[0], w_up[0], w_down[0], ln_final)
    return out.reshape(b, s, d)
```

```python
import functools

import numpy as np

import jax
import jax.numpy as jnp
from jax import lax
from jax.experimental import pallas as pl
from jax.experimental.pallas import tpu as pltpu

F32 = jnp.float32
BF16 = jnp.bfloat16

LANES = 128
MXU_COLS = 256
HEAD_DIM = 64
N_HEADS = 16
WIDTH = N_HEADS * HEAD_DIM
N_PAIRS = WIDTH // LANES
DECAY_LORA, ICLR_LORA, GATE_LORA = 64, 64, 160
LORA_PAD = 128
GATE_PAD = 256
RW_COLS = 3 * WIDTH + DECAY_LORA + ICLR_LORA + GATE_LORA
RW_PAD = 3584
FZ_PAD = 512
FOX_PAD = 3 * WIDTH + FZ_PAD
BOUND_SLACK = 1.02
SAFE_LOGIT_BOUND = 50.0
BIAS_PARTS = 3
BIAS_LANES = BIAS_PARTS * N_HEADS
LOG2E = 1.4426950408889634
KEY_BLOCKS = 8
AHEAD = KEY_BLOCKS
X_HEADS, X_HEAD_DIM = 4, 128
NORM_EPS = 1e-5
GN_EPS = 64e-5
CHUNK = 64
DECAY_SCALE = -0.6065306597126334
SUB = 16
VMEM_LIMIT = 56 * 1024 * 1024


def _params(sem, vmem=VMEM_LIMIT):
    return pltpu.CompilerParams(dimension_semantics=sem, vmem_limit_bytes=vmem)


def _dot(a, b):
    return jnp.dot(a, b, preferred_element_type=F32)


def _dot_t(a, b):
    return lax.dot_general(a, b, (((1,), (1,)), ((), ())), preferred_element_type=F32)


def _split(x, terms):
    parts, rem = [], x
    for i in range(terms):
        part = rem.astype(BF16)
        parts.append(part)
        if i + 1 < terms:
            rem = rem - part.astype(F32)
    return parts


_DIMS = {"nn": (((1,), (0,)), ((), ())), "nt": (((1,), (1,)), ((), ())), "tn": (((0,), (0,)), ((), ()))}


def _mm(a_parts, b_parts, kind="nn"):
    order = max(len(a_parts), len(b_parts))
    acc = None
    for i, a in enumerate(a_parts):
        for j, b in enumerate(b_parts):
            if i + j < order:
                term = lax.dot_general(a, b, _DIMS[kind], preferred_element_type=F32)
                acc = term if acc is None else acc + term
    return acc


def _bf(x):
    return [x.astype(BF16)]


def _softplus(x):
    return jnp.maximum(x, 0.0) + jnp.log(1.0 + jnp.exp(-jnp.abs(x)))


def _sigmoid(x):
    return 1.0 / (1.0 + jnp.exp(-x))


def _head_ones():
    r = lax.broadcasted_iota(jnp.int32, (LANES, LANES), 0) // HEAD_DIM
    c = lax.broadcasted_iota(jnp.int32, (LANES, LANES), 1) // HEAD_DIM
    return (r == c).astype(F32)


def _stacked_headsum(xs, ones, terms):
    n = xs[0].shape[0]
    out = _mm(_split(jnp.concatenate(xs, axis=0), terms), ones)
    return [out[i * n:(i + 1) * n] for i in range(len(xs))]


def _headsum(x, ones_f32, terms=2):
    chunks = [x[:, c * LANES:(c + 1) * LANES] for c in range(x.shape[1] // LANES)]
    return jnp.concatenate(_stacked_headsum(chunks, [ones_f32.astype(BF16)], terms), axis=1)


def _rmsnorm_kernel(x_ref, g_ref, o_ref):
    x = x_ref[...]
    ms = jnp.mean(x * x, axis=-1, keepdims=True)
    o_ref[...] = (x * lax.rsqrt(ms + NORM_EPS) * g_ref[...]).astype(o_ref.dtype)


def _rmsnorm_bf16(x, g, tm):
    m, d = x.shape
    tm = min(tm, m)
    return pl.pallas_call(
        _rmsnorm_kernel,
        grid=(m // tm,),
        in_specs=[pl.BlockSpec((tm, d), lambda i: (i, 0)), pl.BlockSpec((1, d), lambda i: (0, 0))],
        out_specs=pl.BlockSpec((tm, d), lambda i: (i, 0)),
        out_shape=jax.ShapeDtypeStruct((m, d), BF16),
        compiler_params=_params(("parallel",)),
    )(x, g.reshape(1, d))


def _proj_kernel(a_ref, wt_ref, o_ref, w_bf16, *, gate):
    @pl.when(pl.program_id(1) == 0)
    def _():
        for c0 in range(0, wt_ref.shape[0], MXU_COLS):
            w_bf16[:, c0:c0 + MXU_COLS] = wt_ref[c0:c0 + MXU_COLS, :].T.astype(BF16)

    acc = _dot(a_ref[...], w_bf16[...])
    o_ref[...] = (_sigmoid(acc) if gate else acc).astype(o_ref.dtype)


def _project(a, w_t, row0, n, bm, bn, out_dtype=F32, gate=False):
    m, k = a.shape
    bm = min(bm, m)
    assert row0 % 8 == 0 and n % bn == 0 and bn % MXU_COLS == 0 and m % bm == 0 and row0 + n <= w_t.shape[0]
    return pl.pallas_call(
        functools.partial(_proj_kernel, gate=gate),
        grid=(n // bn, m // bm),
        in_specs=[pl.BlockSpec((bm, k), lambda j, i: (i, 0)),
                  pl.BlockSpec((pl.Element(bn), pl.Element(k)), lambda j, i: (pl.multiple_of(row0 + j * bn, 8), 0),
                               pipeline_mode=pl.Buffered(1))],
        out_specs=pl.BlockSpec((bm, bn), lambda j, i: (i, j)),
        out_shape=jax.ShapeDtypeStruct((m, n), out_dtype),
        scratch_shapes=[pltpu.VMEM((k, bn), BF16)],
        compiler_params=_params(("arbitrary", "arbitrary")),
    )(a, w_t)


def _rw_inputs(p, prev_row, mu, w0, wup, a0, aup, gup, k_k, k_a, ones):
    rows = lax.broadcasted_iota(jnp.int32, p.shape, 0)
    prev = jnp.where(rows == 0, prev_row, pltpu.roll(p, 1, 0))
    ps = p + (prev - p) * mu
    w = WIDTH
    r, k, v = ps[:, 0:w], ps[:, w:2 * w], ps[:, 2 * w:3 * w]
    o = 3 * w
    lora = ps[:, o:o + LORA_PAD]
    gd = ps[:, o + LORA_PAD:o + LORA_PAD + GATE_PAD]
    z = w0 + _mm(_bf(jnp.tanh(lora)), [wup])
    lw = DECAY_SCALE * _sigmoid(z)
    iclr = _sigmoid(a0 + _mm(_bf(lora), [aup]))
    gate = _mm(_bf(_sigmoid(gd)), [gup])
    kk = k * k_k
    kk2 = kk * kk
    ss = _stacked_headsum([kk2[:, c * LANES:(c + 1) * LANES] for c in range(N_PAIRS)], ones, 2)
    kk = kk * lax.rsqrt(jnp.maximum(jnp.concatenate(ss, axis=1), 1e-24))
    return r, k * (1.0 + (iclr - 1.0) * k_a), v, lw, -kk, kk * iclr, gate


def _pair_diag(x, head0):
    xb = x.astype(BF16)
    zero = jnp.zeros_like(xb)
    return [jnp.concatenate([jnp.where(head0, xb, zero), jnp.where(head0, zero, xb)], axis=0)]


def _unit_lower_inverse(a_list, sub_mask, eye, head0):
    c = CHUNK
    d = [jnp.where(sub_mask, a, 0.0) for a in a_list]
    e = [a - x for a, x in zip(a_list, d)]
    p = [eye + x for x in d]
    diag = lambda xs: [_pair_diag(x, head0) for x in xs]
    stack = lambda xs, ys: [_bf(jnp.concatenate([x, y], axis=0)) for x, y in zip(xs, ys)]
    dp = [_mm(_bf(x), y) for x, y in zip(d, diag(d))]
    for _ in range(SUB.bit_length() - 3):
        both = [_mm(x, y) for x, y in zip(stack(dp, p), diag(dp))]
        dp = [x[:c] for x in both]
        p = [x + y[c:] for x, y in zip(p, both)]
    p = [x + _mm(_bf(x), y) for x, y in zip(p, diag(dp))]
    f = [_mm(_bf(x), y) for x, y in zip(p, diag(e))]
    g = [eye + x for x in f]
    fp = [_mm(_bf(x), y) for x, y in zip(f, diag(f))]
    for _ in range((CHUNK // SUB).bit_length() - 3):
        both = [_mm(x, y) for x, y in zip(stack(fp, g), diag(fp))]
        fp = [x[:c] for x in both]
        g = [x + y[c:] for x, y in zip(g, both)]
    g = [x + _mm(_bf(x), y) for x, y in zip(g, diag(fp))]
    return [_mm(_bf(x), y) for x, y in zip(g, diag(p))]


def _rw_kernel(p_ref, mu_ref, w0_ref, wup_ref, a0_ref, aup_ref, gup_ref, kkw_ref, ka_ref, rk_ref, lnw_ref, lnb_ref,
               o_ref, h_ref, carry_ref):
    c = CHUNK
    t_tile = p_ref.shape[0]
    n_chunks = t_tile // c

    @pl.when(pl.program_id(0) == 0)
    def _():
        h_ref[...] = jnp.zeros_like(h_ref)
        carry_ref[...] = jnp.zeros_like(carry_ref)

    lane = lax.broadcasted_iota(jnp.int32, (c, LANES), 1)
    head0 = lane < HEAD_DIM
    ri = lax.broadcasted_iota(jnp.int32, (c, LANES), 0)
    ci = jnp.bitwise_and(lane, HEAD_DIM - 1)
    strict = ri > ci
    incl = ri >= ci
    eye = (ri == ci).astype(F32)
    sub_mask = (ri // SUB) == (ci // SUB)
    tril = _bf(lax.broadcasted_iota(jnp.int32, (c, c), 0) >= lax.broadcasted_iota(jnp.int32, (c, c), 1))
    ones_f = _head_ones()
    ones = _bf(ones_f)
    pair_mask = ones_f > 0.5
    diag = lambda x: _pair_diag(x, head0)
    units = [(s, p) for s in range(n_chunks) for p in range(N_PAIRS)]
    rows = lambda s: slice(s * c, (s + 1) * c)
    cols = lambda p: slice(p * LANES, (p + 1) * LANES)

    p_tile = p_ref[...]
    r_all, k_all, v_all, lw_all, a_all, b_all, gate_all = _rw_inputs(
        p_tile, carry_ref[...], mu_ref[...], w0_ref[...], wup_ref[...], a0_ref[...], aup_ref[...], gup_ref[...],
        kkw_ref[...], ka_ref[...], ones)
    carry_ref[...] = p_tile[t_tile - 1:t_tile, :]

    pre = []
    for s in range(n_chunks):
        lw, k, b = lw_all[rows(s), :], k_all[rows(s), :], b_all[rows(s), :]
        cl = _mm(tril, _split(lw, 3))
        cl_end = cl[c - 1:c, :]
        e_neg = jnp.exp(-cl)
        e_end = jnp.exp(cl_end - cl)
        pre.append(dict(rt=r_all[rows(s), :] * jnp.exp(cl), at=a_all[rows(s), :] * jnp.exp(cl - lw),
                        bt=b * e_neg, kt=k * e_neg, b_end=b * e_end, k_end=k * e_end, decay_end=jnp.exp(cl_end)))
    get = lambda name: [pre[s][name][:, cols(p)] for s, p in units]
    tile = lambda x: [x[rows(s), cols(p)] for s, p in units]
    r, k, v, gate = tile(r_all), tile(k_all), tile(v_all), tile(gate_all)
    rt, at = get("rt"), get("at")
    stack = lambda xs, ys: [jnp.concatenate([x, y], axis=0) for x, y in zip(xs, ys)]
    bk_end = [_bf(x) for x in stack(get("b_end"), get("k_end"))]
    v_diag = [diag(x) for x in v]
    mask2 = jnp.concatenate([strict, incl], axis=0)

    lhs = [_bf(x) for x in stack(at, rt)]
    sb = [_mm(x, diag(y), "nt") for x, y in zip(lhs, get("bt"))]
    sk = [_mm(x, diag(y), "nt") for x, y in zip(lhs, get("kt"))]
    skv = [_mm(_bf(jnp.where(mask2, x, 0.0)), y) for x, y in zip(sk, v_diag)]
    rb = [_bf(jnp.where(incl, x[c:], 0.0)) for x in sb]
    t_inv = [_bf(x) for x in _unit_lower_inverse([jnp.where(strict, x[:c], 0.0) for x in sb], sub_mask, eye, head0)]
    w = [_mm(x, diag(y)) for x, y in zip(t_inv, at)]
    tk = [_mm(x, diag(y[:c])) for x, y in zip(t_inv, skv)]
    s0_lhs = [_bf(x) for x in stack(w, rt)]

    ht = [h_ref[p] for p in range(N_PAIRS)]
    u, rs0 = [], []
    for s in range(n_chunks):
        i0 = s * N_PAIRS
        s0 = [_mm(s0_lhs[i0 + p], _bf(ht[p]), "nt") for p in range(N_PAIRS)]
        u += [s0[p][:c] + tk[i0 + p] for p in range(N_PAIRS)]
        rs0 += [s0[p][c:] for p in range(N_PAIRS)]
        upd = [_mm(_bf(jnp.concatenate([u[i0 + p], v[i0 + p]], axis=0)), bk_end[i0 + p], "tn") for p in range(N_PAIRS)]
        ht = [ht[p] * pre[s]["decay_end"][:, cols(p)] + jnp.where(pair_mask, upd[p], 0.0) for p in range(N_PAIRS)]
    for p in range(N_PAIRS):
        h_ref[p] = ht[p]

    y = [a + _mm(x, diag(z)) + b[c:] for a, x, z, b in zip(rs0, rb, u, skv)]
    rkr = [r[i] * k[i] * rk_ref[:, cols(p)] for i, (s, p) in enumerate(units)]
    sums = _stacked_headsum(y + rkr, ones, 2)
    dlt = [a - b * (1.0 / HEAD_DIM) for a, b in zip(y, sums[:len(units)])]
    var = _stacked_headsum([x * x for x in dlt], ones, 2)
    for i, (s, p) in enumerate(units):
        yn = dlt[i] * lax.rsqrt(var[i] * (1.0 / HEAD_DIM) + GN_EPS) * lnw_ref[:, cols(p)] + lnb_ref[:, cols(p)]
        o_ref[rows(s), cols(p)] = ((yn + sums[len(units) + i] * v[i]) * gate[i]).astype(o_ref.dtype)


def _rw_branch(p_rw, mu_p, w0, wup_p, a0, aup_p, gup_p, k_k, k_a, r_k, ln_w, ln_b, t_tile):
    m = p_rw.shape[0]
    t_tile = min(t_tile, m)
    row = lambda a: a.reshape(1, -1)
    full = lambda a: pl.BlockSpec(a.shape, lambda t: (0, 0))
    params = [row(mu_p), row(w0), wup_p, row(a0), aup_p, gup_p, row(k_k), row(k_a), row(r_k), row(ln_w), row(ln_b)]
    return pl.pallas_call(
        _rw_kernel,
        grid=(m // t_tile,),
        in_specs=[pl.BlockSpec((t_tile, RW_PAD), lambda t: (t, 0))] + [full(a) for a in params],
        out_specs=pl.BlockSpec((t_tile, WIDTH), lambda t: (t, 0)),
        out_shape=jax.ShapeDtypeStruct((m, WIDTH), BF16),
        scratch_shapes=[pltpu.VMEM((N_PAIRS, LANES, LANES), F32), pltpu.VMEM((1, RW_PAD), F32)],
        compiler_params=_params(("arbitrary",)),
    )(p_rw, *params)


def _fox_prep_kernel(p_ref, qn_ref, kn_ref, bf_ref, part_ref, sel_ref, selz_ref,
                     q_out, k_out, vt_out, qa_out, ka_out, bound_out, carry_ref, kmax_ref, bmax_ref):
    tm = p_ref.shape[0]

    @pl.when(pl.program_id(0) == 0)
    def _():
        carry_ref[...] = jnp.zeros_like(carry_ref)
        kmax_ref[...] = jnp.zeros_like(kmax_ref)
        bmax_ref[...] = jnp.zeros_like(bmax_ref)

    w = WIDTH
    ones = _head_ones()
    q, k = p_ref[:, 0:w], p_ref[:, w:2 * w]
    q = q * lax.rsqrt(_headsum(q * q, ones) * (1.0 / HEAD_DIM) + NORM_EPS) * qn_ref[...]
    k = k * lax.rsqrt(_headsum(k * k, ones) * (1.0 / HEAD_DIM) + NORM_EPS) * kn_ref[...]
    qb = (q * (HEAD_DIM ** -0.5 * LOG2E)).astype(BF16)
    kb = k.astype(BF16)
    q_out[...] = qb
    k_out[...] = kb
    vt_out[...] = p_ref[:, 2 * w:3 * w].T.astype(vt_out.dtype)

    qf, kf = qb.astype(F32), kb.astype(F32)
    k_sq = _headsum(kf * kf, ones, 1)
    kmax_sq = jnp.maximum(kmax_ref[...], jnp.max(k_sq, axis=0, keepdims=True))
    kmax_ref[...] = kmax_sq
    bound_sq = _headsum(qf * qf, ones, 1) * kmax_sq
    bsel = jnp.sqrt(_mm(_bf(bound_sq), [sel_ref[...]])) * BOUND_SLACK

    fz = _mm(_split(p_ref[:, 3 * w:3 * w + LANES], 3), [selz_ref[...]])
    logf = -_softplus(-(fz + bf_ref[...]))
    ri = lax.broadcasted_iota(jnp.int32, (tm, tm), 0)
    ci = lax.broadcasted_iota(jnp.int32, (tm, tm), 1)
    cum = _mm(_bf(ri >= ci), _split(logf, 3)) + carry_ref[...]
    carry_ref[...] = cum[tm - 1:tm, :]

    part = jnp.broadcast_to(part_ref[...], (tm, LANES))

    def pick(x):
        parts = [t.astype(F32) for t in _split(x, BIAS_PARTS)]
        return jnp.where(part == 0, parts[0], jnp.where(part == 1, parts[1], parts[2]))

    lane = lax.broadcasted_iota(jnp.int32, (tm, LANES), 1)
    low = lane < BIAS_LANES
    high = jnp.logical_and(lane >= BIAS_LANES, lane < 2 * BIAS_LANES)
    ka_out[...] = jnp.where(low, pick(cum * (-LOG2E)), jnp.where(high, 1.0, 0.0)).astype(ka_out.dtype)
    qa_out[...] = jnp.where(low, 1.0, jnp.where(high, pick(cum * LOG2E - bsel), 0.0)).astype(qa_out.dtype)
    bmax = jnp.maximum(bmax_ref[...], jnp.max(bsel, axis=0, keepdims=True))
    bmax_ref[...] = bmax
    bound_out[...] = jnp.broadcast_to(bmax, bound_out.shape)


def _fox_prep(p_fox, q_norm, k_norm, bf_rep, tm):
    m = p_fox.shape[0]
    tm = min(tm, m)
    qn = jnp.tile(q_norm, N_HEADS).reshape(1, WIDTH)
    kn = jnp.tile(k_norm, N_HEADS).reshape(1, WIDTH)
    part = (jnp.arange(LANES, dtype=jnp.int32) % BIAS_PARTS).reshape(1, LANES)
    heads = np.arange(N_HEADS)
    sel = np.zeros((WIDTH, LANES), np.float32)
    for j in range(BIAS_PARTS):
        sel[heads * HEAD_DIM, BIAS_LANES + BIAS_PARTS * heads + j] = 1.0
    sel = jnp.asarray(sel, BF16)
    selz = np.zeros((LANES, LANES), np.float32)
    for j in range(BIAS_PARTS):
        for group in (0, BIAS_LANES):
            selz[heads, group + BIAS_PARTS * heads + j] = 1.0
    selz = jnp.asarray(selz, BF16)
    full = lambda a: pl.BlockSpec(a.shape, lambda i: (0, 0))
    rows = lambda width: pl.BlockSpec((tm, width), lambda i: (i, 0))
    act = jax.ShapeDtypeStruct((m, WIDTH), BF16)
    aug = jax.ShapeDtypeStruct((m, LANES), BF16)
    return pl.pallas_call(
        _fox_prep_kernel,
        grid=(m // tm,),
        in_specs=[rows(FOX_PAD), full(qn), full(kn), full(bf_rep), full(part), full(sel), full(selz)],
        out_specs=[rows(WIDTH), rows(WIDTH), pl.BlockSpec((WIDTH, tm), lambda i: (0, i)), rows(LANES), rows(LANES),
                   pl.BlockSpec((8, LANES), lambda i: (0, 0))],
        out_shape=[act, act, jax.ShapeDtypeStruct((WIDTH, m), BF16), aug, aug, jax.ShapeDtypeStruct((8, LANES), F32)],
        scratch_shapes=[pltpu.VMEM((1, LANES), F32), pltpu.VMEM((1, WIDTH), F32), pltpu.VMEM((1, LANES), F32)],
        compiler_params=_params(("arbitrary",)),
    )(p_fox, qn, kn, bf_rep, part, sel, selz)


def _fox_attn_kernel(qi_ref, kb_ref, kind_ref, q_ref, qa_ref, k_ref, ka_ref, vt_ref, o_ref, m_ref, acc_ref, den_ref, *,
                     running_max):
    tq = q_ref.shape[0]
    pair, step_id = pl.program_id(0), pl.program_id(1)
    kb, kind = kb_ref[step_id], kind_ref[step_id]

    @pl.when(kb == 0)
    def _():
        m_ref[...] = jnp.full_like(m_ref, -jnp.inf)
        acc_ref[...] = jnp.zeros_like(acc_ref)
        den_ref[...] = jnp.zeros_like(den_ref)

    def step(kinds):
        q, qa = q_ref[...], qa_ref[...]
        lane_q = lax.broadcasted_iota(jnp.int32, (tq, LANES), 1)
        q_full = []
        for h in range(2):
            hmask = (lane_q < HEAD_DIM) if h == 0 else (lane_q >= HEAD_DIM)
            first = (2 * pair + h) * BIAS_PARTS
            in_group = lambda lo: jnp.logical_and(lane_q >= lo, lane_q < lo + BIAS_PARTS)
            aug = jnp.logical_or(in_group(first), in_group(first + BIAS_LANES))
            q_full.append(jnp.concatenate([jnp.where(hmask, q, jnp.zeros_like(q)),
                                           jnp.where(aug, qa, jnp.zeros_like(qa))], axis=1))
        subs = [slice(b * tq, (b + 1) * tq) for b in range(len(kinds))]
        k_full = [jnp.concatenate([k_ref[sl, :], ka_ref[sl, :]], axis=1) for sl in subs]
        v_t = [vt_ref[:, sl] for sl in subs]
        scores = lambda b: [_mm([k_full[b]], [q_full[h]], "nt") for h in range(2)]
        s = [scores(b) for b in range(min(AHEAD, len(kinds)))]
        causal = lax.broadcasted_iota(jnp.int32, (tq, tq), 0) <= lax.broadcasted_iota(jnp.int32, (tq, tq), 1)
        m = [m_ref[h] for h in range(2)]
        acc = [acc_ref[h] for h in range(2)]
        den = [den_ref[h] for h in range(2)]
        for b, diagonal in enumerate(kinds):
            if b + AHEAD < len(kinds):
                s.append(scores(b + AHEAD))
            sb = [jnp.where(causal, x, -jnp.inf) for x in s[b]] if diagonal else s[b]
            if running_max:
                m_new = [jnp.maximum(m[h], jnp.max(sb[h], axis=0, keepdims=True)) for h in range(2)]
                p = [jnp.exp2(sb[h] - m_new[h]) for h in range(2)]
                alpha = [jnp.exp2(m[h] - m_new[h]) for h in range(2)]
                acc = [alpha[h] * acc[h] for h in range(2)]
                den = [alpha[h] * den[h] for h in range(2)]
                m = m_new
            else:
                p = [jnp.exp2(sb[h]) for h in range(2)]
            den = [den[h] + jnp.sum(p[h].reshape(tq // 8, 8, tq), axis=0) for h in range(2)]
            pv = [_mm([v_t[b]], _bf(p[h])) for h in range(2)]
            acc = [acc[h] + pv[h] for h in range(2)]
        for h in range(2):
            acc_ref[h] = acc[h]
            den_ref[h] = den[h]
            if running_max:
                m_ref[h] = m[h]

    def finish():
        row = lax.broadcasted_iota(jnp.int32, (LANES, tq), 0)
        o = [acc_ref[h] / jnp.sum(den_ref[h], axis=0, keepdims=True) for h in range(2)]
        o_ref[...] = jnp.where(row < HEAD_DIM, o[0], o[1]).T.astype(o_ref.dtype)

    @pl.when(kind == 0)
    def _():
        step((False,) * KEY_BLOCKS)

    for n_full in range(KEY_BLOCKS):
        @pl.when(kind == 1 + n_full)
        def _(n_full=n_full):
            step((False,) * n_full + (True,))
            finish()


def _fox_attn(q, qa, k, ka, vt, tq, running_max):
    m = q.shape[0]
    tq = min(tq, m)
    nq = m // tq
    g = KEY_BLOCKS
    assert nq % g == 0, "key blocks are fetched in groups"
    steps = []
    for i in range(nq):
        steps += [(i, j, 0) for j in range(i // g)] + [(i, i // g, 1 + i % g)]
    qi_tab, kb_tab, kind_tab = (jnp.asarray([st[c] for st in steps], jnp.int32) for c in range(3))
    qspec = pl.BlockSpec((tq, LANES), lambda p, t, qi, kb, kind: (qi[t], p))
    qaspec = pl.BlockSpec((tq, LANES), lambda p, t, qi, kb, kind: (qi[t], 0))
    kspec = pl.BlockSpec((g * tq, LANES), lambda p, t, qi, kb, kind: (kb[t], p))
    kaspec = pl.BlockSpec((g * tq, LANES), lambda p, t, qi, kb, kind: (kb[t], 0))
    vtspec = pl.BlockSpec((LANES, g * tq), lambda p, t, qi, kb, kind: (p, kb[t]))
    return pl.pallas_call(
        functools.partial(_fox_attn_kernel, running_max=running_max),
        grid_spec=pltpu.PrefetchScalarGridSpec(
            num_scalar_prefetch=3,
            grid=(N_PAIRS, len(steps)),
            in_specs=[qspec, qaspec, kspec, kaspec, vtspec],
            out_specs=qspec,
            scratch_shapes=[pltpu.VMEM((2, 1, tq), F32), pltpu.VMEM((2, LANES, tq), F32),
                            pltpu.VMEM((2, 8, tq), F32)],
        ),
        out_shape=jax.ShapeDtypeStruct((m, WIDTH), BF16),
        compiler_params=_params(("parallel", "arbitrary")),
    )(qi_tab, kb_tab, kind_tab, q, qa, k, ka, vt)


def _mem_kv_kernel(mem_ref, g_ref, w_ref, o_ref):
    x = mem_ref[...]
    ms = jnp.mean(x * x, axis=-1, keepdims=True)
    xn = (x * lax.rsqrt(ms + NORM_EPS) * g_ref[...]).astype(BF16)
    o_ref[...] = _dot(xn, w_ref[...]).astype(o_ref.dtype)


def _mem_kv(mem, ln_mem, w_ckv):
    n, d = mem.shape
    ins = [mem, ln_mem.reshape(1, d), w_ckv]
    return pl.pallas_call(
        _mem_kv_kernel,
        grid=(1,),
        in_specs=[pl.BlockSpec(a.shape, lambda i: (0, 0)) for a in ins],
        out_specs=pl.BlockSpec((n, w_ckv.shape[1]), lambda i: (0, 0)),
        out_shape=jax.ShapeDtypeStruct((n, w_ckv.shape[1]), BF16),
        compiler_params=_params(("arbitrary",)),
    )(*ins)


def _post_kernel(ya_ref, yb_ref, pg_ref, x_ref, wa_ref, wb_ref, wo_ref, lnc_ref, wcq_ref, kv_ref, wco_ref, o_ref):
    d = x_ref.shape[1]
    xw = X_HEADS * X_HEAD_DIM
    ga = pg_ref[:, 0:d].astype(F32)
    gb = pg_ref[:, d:2 * d].astype(F32)
    merged = ga * _dot(ya_ref[...], wa_ref[...]) + gb * _dot(yb_ref[...], wb_ref[...])
    h = x_ref[...] + _dot(merged.astype(BF16), wo_ref[...])

    ms = jnp.mean(h * h, axis=-1, keepdims=True)
    hn = (h * lax.rsqrt(ms + NORM_EPS) * lnc_ref[...]).astype(BF16)
    q = _dot(hn, wcq_ref[...])
    outs = []
    for hd in range(X_HEADS):
        cols = slice(hd * X_HEAD_DIM, (hd + 1) * X_HEAD_DIM)
        vcols = slice(xw + hd * X_HEAD_DIM, xw + (hd + 1) * X_HEAD_DIM)
        s = _dot_t(q[:, cols].astype(BF16), kv_ref[:, cols]) * (X_HEAD_DIM ** -0.5)
        s = s - jnp.max(s, axis=1, keepdims=True)
        e = jnp.exp(s)
        pr = e / jnp.sum(e, axis=1, keepdims=True)
        outs.append(_dot(pr.astype(BF16), kv_ref[:, vcols]))
    o = jnp.concatenate(outs, axis=1).astype(BF16)
    o_ref[...] = h + _dot(o, wco_ref[...])


def _post(ya, yb, pg, x, wa, wb, wo, ln_cross, wcq, kv, wco, tm):
    m, d = x.shape
    tm = min(tm, m)
    lnc = ln_cross.reshape(1, d)
    rows = lambda a: pl.BlockSpec((tm, a.shape[1]), lambda i: (i, 0))
    const = lambda a: pl.BlockSpec(a.shape, lambda i: (0, 0), pipeline_mode=pl.Buffered(1))
    return pl.pallas_call(
        _post_kernel,
        grid=(m // tm,),
        in_specs=[rows(ya), rows(yb), rows(pg), rows(x)] + [const(a) for a in (wa, wb, wo, lnc, wcq, kv, wco)],
        out_specs=pl.BlockSpec((tm, d), lambda i: (i, 0)),
        out_shape=jax.ShapeDtypeStruct((m, d), F32),
        compiler_params=_params(("parallel",)),
    )(ya, yb, pg, x, wa, wb, wo, lnc, wcq, kv, wco)


def _mlp_kernel(h_ref, g_ref, wu_ref, wd_ref, gf_ref, o_ref, xn_ref):
    f = pl.program_id(1)

    @pl.when(f == 0)
    def _():
        h = h_ref[...]
        ms = jnp.mean(h * h, axis=-1, keepdims=True)
        xn_ref[...] = (h * lax.rsqrt(ms + NORM_EPS) * g_ref[...]).astype(xn_ref.dtype)
        o_ref[...] = jnp.zeros_like(o_ref)

    u = jnp.maximum(_dot(xn_ref[...], wu_ref[...]), 0.0)
    o_ref[...] += _dot((u * u).astype(BF16), wd_ref[...])

    @pl.when(f == pl.num_programs(1) - 1)
    def _():
        h = h_ref[...] + o_ref[...]
        ms = jnp.mean(h * h, axis=-1, keepdims=True)
        o_ref[...] = h * lax.rsqrt(ms + NORM_EPS) * gf_ref[...]


def _mlp(h, ln_mlp, w_up, w_down, ln_final, tm, tf):
    m, d = h.shape
    dff = w_up.shape[1]
    tm, tf = min(tm, m), min(tf, dff)
    return pl.pallas_call(
        _mlp_kernel,
        grid=(m // tm, dff // tf),
        in_specs=[pl.BlockSpec((tm, d), lambda i, f: (i, 0)),
                  pl.BlockSpec((1, d), lambda i, f: (0, 0)),
                  pl.BlockSpec((d, tf), lambda i, f: (0, f)),
                  pl.BlockSpec((tf, d), lambda i, f: (f, 0)),
                  pl.BlockSpec((1, d), lambda i, f: (0, 0))],
        out_specs=pl.BlockSpec((tm, d), lambda i, f: (i, 0)),
        out_shape=jax.ShapeDtypeStruct((m, d), F32),
        scratch_shapes=[pltpu.VMEM((tm, d), BF16)],
        compiler_params=_params(("parallel", "arbitrary")),
    )(h, ln_mlp.reshape(1, d), w_up, w_down, ln_final.reshape(1, d))


def _pad_rows(a, height):
    return jnp.pad(a, ((0, height - a.shape[0]), (0, 0)))


def _layer(h, mem, ln_mix, w_in, rw_mu, rw_w0, rw_w_up, rw_a0, rw_a_up, rw_g_up, rw_k_k, rw_k_a, rw_r_k,
           rw_ln_w, rw_ln_b, fox_b_f, fox_q_norm, fox_k_norm, w_proj_a, w_proj_b, w_out, ln_cross, ln_mem,
           w_cq, w_ckv, w_co, ln_mlp, w_up, w_down, ln_final):
    w_t = w_in.T
    fox_row0 = RW_COLS
    gate_row0 = RW_COLS + 3 * WIDTH + N_HEADS
    mu_p = jnp.pad(rw_mu, (0, RW_PAD - RW_COLS))
    bf_rep = jnp.pad(jnp.tile(jnp.repeat(fox_b_f, BIAS_PARTS), 2), (0, LANES - 2 * BIAS_LANES)).reshape(1, LANES)
    wup_p = _pad_rows(rw_w_up, LORA_PAD).astype(BF16)
    aup_p = jnp.pad(rw_a_up, ((DECAY_LORA, 0), (0, 0))).astype(BF16)
    gup_p = _pad_rows(rw_g_up, GATE_PAD).astype(BF16)

    xn = _rmsnorm_bf16(h, ln_mix, 512)
    p_rw = _project(xn, w_t, 0, RW_PAD, 1024, 1792)
    p_fox = _project(xn, w_t, fox_row0, FOX_PAD, 1024, 1792)
    gates = _project(xn, w_t, gate_row0, w_t.shape[0] - gate_row0, 1024, 2048, BF16, gate=True)

    y_a = _rw_branch(p_rw, mu_p, rw_w0, wup_p, rw_a0, aup_p, gup_p, rw_k_k, rw_k_a, rw_r_k.reshape(-1), rw_ln_w,
                     rw_ln_b, 8 * CHUNK)

    fq, fk, fvt, fqa, fka, fbound = _fox_prep(p_fox, fox_q_norm, fox_k_norm, bf_rep, 512)
    y_b = lax.cond(jnp.max(fbound) < SAFE_LOGIT_BOUND,
                   functools.partial(_fox_attn, tq=512, running_max=False),
                   functools.partial(_fox_attn, tq=512, running_max=True), fq, fqa, fk, fka, fvt)

    kv = _mem_kv(mem, ln_mem, w_ckv.astype(BF16))
    h = _post(y_a, y_b, gates, h, w_proj_a.astype(BF16), w_proj_b.astype(BF16), w_out.astype(BF16), ln_cross,
              w_cq.astype(BF16), kv, w_co.astype(BF16), 512)
    return _mlp(h, ln_mlp, w_up.astype(BF16), w_down.astype(BF16), ln_final, 512, 1024)


def kernel(x, mem, ln_mix, w_in, rw_mu, rw_w0, rw_w_up, rw_a0, rw_a_up, rw_g_up, rw_k_k, rw_k_a, rw_r_k, rw_ln_w, rw_ln_b, fox_b_f, fox_q_norm, fox_k_norm, w_proj_a, w_proj_b, w_out, ln_cross, ln_mem, w_cq, w_ckv, w_co, ln_mlp, w_up, w_down, ln_final):
    b, s, d = x.shape
    assert b == 1 and ln_mix.shape[0] == 1, "single sequence, single layer"
    out = _layer(x[0], mem[0], ln_mix[0], w_in[0], rw_mu[0], rw_w0[0], rw_w_up[0], rw_a0[0], rw_a_up[0], rw_g_up[0],
                 rw_k_k[0], rw_k_a[0], rw_r_k[0], rw_ln_w[0], rw_ln_b[0], fox_b_f[0], fox_q_norm[0], fox_k_norm[0],
                 w_proj_a[0], w_proj_b[0], w_out[0], ln_cross[0], ln_mem[0], w_cq[0], w_ckv[0], w_co[0],
                 ln_mlp[0], w_up[0], w_down[0], ln_final)
    return out.reshape(b, s, d)
```

```python
import functools

import numpy as np

import jax
import jax.numpy as jnp
from jax import lax
from jax.experimental import pallas as pl
from jax.experimental.pallas import tpu as pltpu

F32 = jnp.float32
BF16 = jnp.bfloat16

LANES = 128
SUBLANES = 8
MXU_COLS = 256
HEAD_DIM = 64
N_HEADS = 16
WIDTH = N_HEADS * HEAD_DIM
N_PAIRS = WIDTH // LANES
DECAY_LORA, ICLR_LORA, GATE_LORA = 64, 64, 160
LORA_PAD = 128
GATE_PAD = 256
RW_COLS = 3 * WIDTH + DECAY_LORA + ICLR_LORA + GATE_LORA
RW_PAD = 3584
FZ_PAD = 512
FOX_PAD = 3 * WIDTH + FZ_PAD
BOUND_SLACK = 1.02
SAFE_LOGIT_BOUND = 50.0
BIAS_PARTS = 3
BIAS_LANES = BIAS_PARTS * N_HEADS
LOG2E = 1.4426950408889634
KEY_BLOCKS = 8
AHEAD = KEY_BLOCKS
X_HEADS, X_HEAD_DIM = 4, 128
NORM_EPS = 1e-5
GN_EPS = 64e-5
CHUNK = 64
DECAY_SCALE = -0.6065306597126334
SUB = 16
VMEM_LIMIT = 56 * 1024 * 1024


def _params(sem, vmem=VMEM_LIMIT):
    return pltpu.CompilerParams(dimension_semantics=sem, vmem_limit_bytes=vmem)


def _dot(a, b):
    return jnp.dot(a, b, preferred_element_type=F32)


def _dot_t(a, b):
    return lax.dot_general(a, b, (((1,), (1,)), ((), ())), preferred_element_type=F32)


def _split(x, terms):
    parts, rem = [], x
    for i in range(terms):
        part = rem.astype(BF16)
        parts.append(part)
        if i + 1 < terms:
            rem = rem - part.astype(F32)
    return parts


_DIMS = {"nn": (((1,), (0,)), ((), ())), "nt": (((1,), (1,)), ((), ())), "tn": (((0,), (0,)), ((), ()))}


def _mm(a_parts, b_parts, kind="nn"):
    order = max(len(a_parts), len(b_parts))
    acc = None
    for i, a in enumerate(a_parts):
        for j, b in enumerate(b_parts):
            if i + j < order:
                term = lax.dot_general(a, b, _DIMS[kind], preferred_element_type=F32)
                acc = term if acc is None else acc + term
    return acc


def _bf(x):
    return [x.astype(BF16)]


def _softplus(x):
    return jnp.maximum(x, 0.0) + jnp.log(1.0 + jnp.exp(-jnp.abs(x)))


def _sigmoid(x):
    return 1.0 / (1.0 + jnp.exp(-x))


def _head_ones():
    r = lax.broadcasted_iota(jnp.int32, (LANES, LANES), 0) // HEAD_DIM
    c = lax.broadcasted_iota(jnp.int32, (LANES, LANES), 1) // HEAD_DIM
    return (r == c).astype(F32)


def _stacked_headsum(xs, ones, terms):
    n = xs[0].shape[0]
    out = _mm(_split(jnp.concatenate(xs, axis=0), terms), ones)
    return [out[i * n:(i + 1) * n] for i in range(len(xs))]


def _headsum(x, ones_f32, terms=2):
    chunks = [x[:, c * LANES:(c + 1) * LANES] for c in range(x.shape[1] // LANES)]
    return jnp.concatenate(_stacked_headsum(chunks, [ones_f32.astype(BF16)], terms), axis=1)


def _rmsnorm_kernel(x_ref, g_ref, o_ref):
    x = x_ref[...]
    ms = jnp.mean(x * x, axis=-1, keepdims=True)
    o_ref[...] = (x * lax.rsqrt(ms + NORM_EPS) * g_ref[...]).astype(o_ref.dtype)


def _rmsnorm_bf16(x, g, tm):
    m, d = x.shape
    tm = min(tm, m)
    return pl.pallas_call(
        _rmsnorm_kernel,
        grid=(m // tm,),
        in_specs=[pl.BlockSpec((tm, d), lambda i: (i, 0)), pl.BlockSpec((1, d), lambda i: (0, 0))],
        out_specs=pl.BlockSpec((tm, d), lambda i: (i, 0)),
        out_shape=jax.ShapeDtypeStruct((m, d), BF16),
        compiler_params=_params(("parallel",)),
    )(x, g.reshape(1, d))


def _proj_kernel(a_ref, wt_ref, o_ref, w_bf16, *, gate):
    @pl.when(pl.program_id(1) == 0)
    def _():
        for c0 in range(0, wt_ref.shape[0], MXU_COLS):
            w_bf16[:, c0:c0 + MXU_COLS] = wt_ref[c0:c0 + MXU_COLS, :].T.astype(BF16)

    acc = _dot(a_ref[...], w_bf16[...])
    o_ref[...] = (_sigmoid(acc) if gate else acc).astype(o_ref.dtype)


def _project(a, w_t, row0, n, bm, bn, out_dtype=F32, gate=False):
    m, k = a.shape
    bm = min(bm, m)
    assert row0 % SUBLANES == 0 and n % bn == 0 and bn % MXU_COLS == 0 and m % bm == 0 and row0 + n <= w_t.shape[0]
    return pl.pallas_call(
        functools.partial(_proj_kernel, gate=gate),
        grid=(n // bn, m // bm),
        in_specs=[pl.BlockSpec((bm, k), lambda j, i: (i, 0)),
                  pl.BlockSpec((pl.Element(bn), pl.Element(k)), lambda j, i: (pl.multiple_of(row0 + j * bn, SUBLANES), 0),
                               pipeline_mode=pl.Buffered(1))],
        out_specs=pl.BlockSpec((bm, bn), lambda j, i: (i, j)),
        out_shape=jax.ShapeDtypeStruct((m, n), out_dtype),
        scratch_shapes=[pltpu.VMEM((k, bn), BF16)],
        compiler_params=_params(("arbitrary", "arbitrary")),
    )(a, w_t)


def _rw_inputs(p, prev_row, mu, w0, wup, a0, aup, gup, k_k, k_a, ones):
    rows = lax.broadcasted_iota(jnp.int32, p.shape, 0)
    prev = jnp.where(rows == 0, prev_row, pltpu.roll(p, 1, 0))
    ps = p + (prev - p) * mu
    w = WIDTH
    r, k, v = ps[:, 0:w], ps[:, w:2 * w], ps[:, 2 * w:3 * w]
    o = 3 * w
    lora = ps[:, o:o + LORA_PAD]
    gd = ps[:, o + LORA_PAD:o + LORA_PAD + GATE_PAD]
    z = w0 + _mm(_bf(jnp.tanh(lora)), [wup])
    lw = DECAY_SCALE * _sigmoid(z)
    iclr = _sigmoid(a0 + _mm(_bf(lora), [aup]))
    gate = _mm(_bf(_sigmoid(gd)), [gup])
    kk = k * k_k
    kk2 = kk * kk
    ss = _stacked_headsum([kk2[:, c * LANES:(c + 1) * LANES] for c in range(N_PAIRS)], ones, 2)
    kk = kk * lax.rsqrt(jnp.maximum(jnp.concatenate(ss, axis=1), 1e-24))
    return r, k * (1.0 + (iclr - 1.0) * k_a), v, lw, -kk, kk * iclr, gate


def _pair_diag(x, head0):
    xb = x.astype(BF16)
    zero = jnp.zeros_like(xb)
    return [jnp.concatenate([jnp.where(head0, xb, zero), jnp.where(head0, zero, xb)], axis=0)]


def _unit_lower_inverse(a_list, sub_mask, eye, head0):
    c = CHUNK
    d = [jnp.where(sub_mask, a, 0.0) for a in a_list]
    e = [a - x for a, x in zip(a_list, d)]
    p = [eye + x for x in d]
    diag = lambda xs: [_pair_diag(x, head0) for x in xs]
    stack = lambda xs, ys: [_bf(jnp.concatenate([x, y], axis=0)) for x, y in zip(xs, ys)]
    dp = [_mm(_bf(x), y) for x, y in zip(d, diag(d))]
    for _ in range(SUB.bit_length() - 3):
        both = [_mm(x, y) for x, y in zip(stack(dp, p), diag(dp))]
        dp = [x[:c] for x in both]
        p = [x + y[c:] for x, y in zip(p, both)]
    p = [x + _mm(_bf(x), y) for x, y in zip(p, diag(dp))]
    f = [_mm(_bf(x), y) for x, y in zip(p, diag(e))]
    g = [eye + x for x in f]
    fp = [_mm(_bf(x), y) for x, y in zip(f, diag(f))]
    for _ in range((CHUNK // SUB).bit_length() - 3):
        both = [_mm(x, y) for x, y in zip(stack(fp, g), diag(fp))]
        fp = [x[:c] for x in both]
        g = [x + y[c:] for x, y in zip(g, both)]
    g = [x + _mm(_bf(x), y) for x, y in zip(g, diag(fp))]
    return [_mm(_bf(x), y) for x, y in zip(g, diag(p))]


def _rw_kernel(p_ref, mu_ref, w0_ref, wup_ref, a0_ref, aup_ref, gup_ref, kkw_ref, ka_ref, rk_ref, lnw_ref, lnb_ref,
               o_ref, h_ref, carry_ref):
    c = CHUNK
    t_tile = p_ref.shape[0]
    n_chunks = t_tile // c

    @pl.when(pl.program_id(0) == 0)
    def _():
        h_ref[...] = jnp.zeros_like(h_ref)
        carry_ref[...] = jnp.zeros_like(carry_ref)

    lane = lax.broadcasted_iota(jnp.int32, (c, LANES), 1)
    head0 = lane < HEAD_DIM
    ri = lax.broadcasted_iota(jnp.int32, (c, LANES), 0)
    ci = jnp.bitwise_and(lane, HEAD_DIM - 1)
    strict = ri > ci
    incl = ri >= ci
    eye = (ri == ci).astype(F32)
    sub_mask = (ri // SUB) == (ci // SUB)
    tril = _bf(lax.broadcasted_iota(jnp.int32, (c, c), 0) >= lax.broadcasted_iota(jnp.int32, (c, c), 1))
    ones_f = _head_ones()
    ones = _bf(ones_f)
    pair_mask = ones_f > 0.5
    diag = lambda x: _pair_diag(x, head0)
    units = [(s, p) for s in range(n_chunks) for p in range(N_PAIRS)]
    rows = lambda s: slice(s * c, (s + 1) * c)
    cols = lambda p: slice(p * LANES, (p + 1) * LANES)

    p_tile = p_ref[...]
    r_all, k_all, v_all, lw_all, a_all, b_all, gate_all = _rw_inputs(
        p_tile, carry_ref[...], mu_ref[...], w0_ref[...], wup_ref[...], a0_ref[...], aup_ref[...], gup_ref[...],
        kkw_ref[...], ka_ref[...], ones)
    carry_ref[...] = p_tile[t_tile - 1:t_tile, :]

    pre = []
    for s in range(n_chunks):
        lw, k, b = lw_all[rows(s), :], k_all[rows(s), :], b_all[rows(s), :]
        cl = _mm(tril, _split(lw, 3))
        cl_end = cl[c - 1:c, :]
        e_neg = jnp.exp(-cl)
        e_end = jnp.exp(cl_end - cl)
        pre.append(dict(rt=r_all[rows(s), :] * jnp.exp(cl), at=a_all[rows(s), :] * jnp.exp(cl - lw),
                        bt=b * e_neg, kt=k * e_neg, b_end=b * e_end, k_end=k * e_end, decay_end=jnp.exp(cl_end)))
    get = lambda name: [pre[s][name][:, cols(p)] for s, p in units]
    tile = lambda x: [x[rows(s), cols(p)] for s, p in units]
    r, k, v, gate = tile(r_all), tile(k_all), tile(v_all), tile(gate_all)
    rt, at = get("rt"), get("at")
    stack = lambda xs, ys: [jnp.concatenate([x, y], axis=0) for x, y in zip(xs, ys)]
    bk_end = [_bf(x) for x in stack(get("b_end"), get("k_end"))]
    v_diag = [diag(x) for x in v]
    mask2 = jnp.concatenate([strict, incl], axis=0)

    lhs = [_bf(x) for x in stack(at, rt)]
    sb = [_mm(x, diag(y), "nt") for x, y in zip(lhs, get("bt"))]
    sk = [_mm(x, diag(y), "nt") for x, y in zip(lhs, get("kt"))]
    skv = [_mm(_bf(jnp.where(mask2, x, 0.0)), y) for x, y in zip(sk, v_diag)]
    rb = [_bf(jnp.where(incl, x[c:], 0.0)) for x in sb]
    t_inv = [_bf(x) for x in _unit_lower_inverse([jnp.where(strict, x[:c], 0.0) for x in sb], sub_mask, eye, head0)]
    w = [_mm(x, diag(y)) for x, y in zip(t_inv, at)]
    tk = [_mm(x, diag(y[:c])) for x, y in zip(t_inv, skv)]
    s0_lhs = [_bf(x) for x in stack(w, rt)]

    ht = [h_ref[p] for p in range(N_PAIRS)]
    u, rs0 = [], []
    for s in range(n_chunks):
        i0 = s * N_PAIRS
        s0 = [_mm(s0_lhs[i0 + p], _bf(ht[p]), "nt") for p in range(N_PAIRS)]
        u += [s0[p][:c] + tk[i0 + p] for p in range(N_PAIRS)]
        rs0 += [s0[p][c:] for p in range(N_PAIRS)]
        upd = [_mm(_bf(jnp.concatenate([u[i0 + p], v[i0 + p]], axis=0)), bk_end[i0 + p], "tn") for p in range(N_PAIRS)]
        ht = [ht[p] * pre[s]["decay_end"][:, cols(p)] + jnp.where(pair_mask, upd[p], 0.0) for p in range(N_PAIRS)]
    for p in range(N_PAIRS):
        h_ref[p] = ht[p]

    y = [a + _mm(x, diag(z)) + b[c:] for a, x, z, b in zip(rs0, rb, u, skv)]
    rkr = [r[i] * k[i] * rk_ref[:, cols(p)] for i, (s, p) in enumerate(units)]
    sums = _stacked_headsum(y + rkr, ones, 2)
    dlt = [a - b * (1.0 / HEAD_DIM) for a, b in zip(y, sums[:len(units)])]
    var = _stacked_headsum([x * x for x in dlt], ones, 2)
    for i, (s, p) in enumerate(units):
        yn = dlt[i] * lax.rsqrt(var[i] * (1.0 / HEAD_DIM) + GN_EPS) * lnw_ref[:, cols(p)] + lnb_ref[:, cols(p)]
        o_ref[rows(s), cols(p)] = ((yn + sums[len(units) + i] * v[i]) * gate[i]).astype(o_ref.dtype)


def _rw_branch(p_rw, mu_p, w0, wup_p, a0, aup_p, gup_p, k_k, k_a, r_k, ln_w, ln_b, t_tile):
    m = p_rw.shape[0]
    t_tile = min(t_tile, m)
    row = lambda a: a.reshape(1, -1)
    full = lambda a: pl.BlockSpec(a.shape, lambda t: (0, 0))
    params = [row(mu_p), row(w0), wup_p, row(a0), aup_p, gup_p, row(k_k), row(k_a), row(r_k), row(ln_w), row(ln_b)]
    return pl.pallas_call(
        _rw_kernel,
        grid=(m // t_tile,),
        in_specs=[pl.BlockSpec((t_tile, RW_PAD), lambda t: (t, 0))] + [full(a) for a in params],
        out_specs=pl.BlockSpec((t_tile, WIDTH), lambda t: (t, 0)),
        out_shape=jax.ShapeDtypeStruct((m, WIDTH), BF16),
        scratch_shapes=[pltpu.VMEM((N_PAIRS, LANES, LANES), F32), pltpu.VMEM((1, RW_PAD), F32)],
        compiler_params=_params(("arbitrary",)),
    )(p_rw, *params)


def _fox_prep_kernel(p_ref, qn_ref, kn_ref, bf_ref, part_ref, sel_ref, selz_ref,
                     q_out, k_out, vt_out, qa_out, ka_out, bound_out, carry_ref, kmax_ref, bmax_ref):
    tm = p_ref.shape[0]

    @pl.when(pl.program_id(0) == 0)
    def _():
        carry_ref[...] = jnp.zeros_like(carry_ref)
        kmax_ref[...] = jnp.zeros_like(kmax_ref)
        bmax_ref[...] = jnp.zeros_like(bmax_ref)

    w = WIDTH
    ones = _head_ones()
    q, k = p_ref[:, 0:w], p_ref[:, w:2 * w]
    q = q * lax.rsqrt(_headsum(q * q, ones) * (1.0 / HEAD_DIM) + NORM_EPS) * qn_ref[...]
    k = k * lax.rsqrt(_headsum(k * k, ones) * (1.0 / HEAD_DIM) + NORM_EPS) * kn_ref[...]
    qb = (q * (HEAD_DIM ** -0.5 * LOG2E)).astype(BF16)
    kb = k.astype(BF16)
    q_out[...] = qb
    k_out[...] = kb
    vt_out[...] = p_ref[:, 2 * w:3 * w].T.astype(vt_out.dtype)

    qf, kf = qb.astype(F32), kb.astype(F32)
    k_sq = _headsum(kf * kf, ones, 1)
    kmax_sq = jnp.maximum(kmax_ref[...], jnp.max(k_sq, axis=0, keepdims=True))
    kmax_ref[...] = kmax_sq
    bound_sq = _headsum(qf * qf, ones, 1) * kmax_sq
    bsel = jnp.sqrt(_mm(_bf(bound_sq), [sel_ref[...]])) * BOUND_SLACK

    fz = _mm(_split(p_ref[:, 3 * w:3 * w + LANES], 3), [selz_ref[...]])
    logf = -_softplus(-(fz + bf_ref[...]))
    ri = lax.broadcasted_iota(jnp.int32, (tm, tm), 0)
    ci = lax.broadcasted_iota(jnp.int32, (tm, tm), 1)
    cum = _mm(_bf(ri >= ci), _split(logf, 3)) + carry_ref[...]
    carry_ref[...] = cum[tm - 1:tm, :]

    part = jnp.broadcast_to(part_ref[...], (tm, LANES))

    def pick(x):
        parts = [t.astype(F32) for t in _split(x, BIAS_PARTS)]
        return jnp.where(part == 0, parts[0], jnp.where(part == 1, parts[1], parts[2]))

    lane = lax.broadcasted_iota(jnp.int32, (tm, LANES), 1)
    low = lane < BIAS_LANES
    high = jnp.logical_and(lane >= BIAS_LANES, lane < 2 * BIAS_LANES)
    ka_out[...] = jnp.where(low, pick(cum * (-LOG2E)), jnp.where(high, 1.0, 0.0)).astype(ka_out.dtype)
    qa_out[...] = jnp.where(low, 1.0, jnp.where(high, pick(cum * LOG2E - bsel), 0.0)).astype(qa_out.dtype)
    bmax = jnp.maximum(bmax_ref[...], jnp.max(bsel, axis=0, keepdims=True))
    bmax_ref[...] = bmax
    bound_out[...] = jnp.broadcast_to(bmax, bound_out.shape)


def _fox_prep(p_fox, q_norm, k_norm, bf_rep, tm):
    m = p_fox.shape[0]
    tm = min(tm, m)
    qn = jnp.tile(q_norm, N_HEADS).reshape(1, WIDTH)
    kn = jnp.tile(k_norm, N_HEADS).reshape(1, WIDTH)
    part = (jnp.arange(LANES, dtype=jnp.int32) % BIAS_PARTS).reshape(1, LANES)
    heads = np.arange(N_HEADS)
    sel = np.zeros((WIDTH, LANES), np.float32)
    for j in range(BIAS_PARTS):
        sel[heads * HEAD_DIM, BIAS_LANES + BIAS_PARTS * heads + j] = 1.0
    sel = jnp.asarray(sel, BF16)
    selz = np.zeros((LANES, LANES), np.float32)
    for j in range(BIAS_PARTS):
        for group in (0, BIAS_LANES):
            selz[heads, group + BIAS_PARTS * heads + j] = 1.0
    selz = jnp.asarray(selz, BF16)
    full = lambda a: pl.BlockSpec(a.shape, lambda i: (0, 0))
    rows = lambda width: pl.BlockSpec((tm, width), lambda i: (i, 0))
    act = jax.ShapeDtypeStruct((m, WIDTH), BF16)
    aug = jax.ShapeDtypeStruct((m, LANES), BF16)
    return pl.pallas_call(
        _fox_prep_kernel,
        grid=(m // tm,),
        in_specs=[rows(FOX_PAD), full(qn), full(kn), full(bf_rep), full(part), full(sel), full(selz)],
        out_specs=[rows(WIDTH), rows(WIDTH), pl.BlockSpec((WIDTH, tm), lambda i: (0, i)), rows(LANES), rows(LANES),
                   pl.BlockSpec((SUBLANES, LANES), lambda i: (0, 0))],
        out_shape=[act, act, jax.ShapeDtypeStruct((WIDTH, m), BF16), aug, aug, jax.ShapeDtypeStruct((SUBLANES, LANES), F32)],
        scratch_shapes=[pltpu.VMEM((1, LANES), F32), pltpu.VMEM((1, WIDTH), F32), pltpu.VMEM((1, LANES), F32)],
        compiler_params=_params(("arbitrary",)),
    )(p_fox, qn, kn, bf_rep, part, sel, selz)


def _fox_attn_kernel(qi_ref, kb_ref, kind_ref, q_ref, qa_ref, k_ref, ka_ref, vt_ref, o_ref, m_ref, acc_ref, den_ref, *,
                     running_max):
    tq = q_ref.shape[0]
    pair, step_id = pl.program_id(0), pl.program_id(1)
    kb, kind = kb_ref[step_id], kind_ref[step_id]

    @pl.when(kb == 0)
    def _():
        m_ref[...] = jnp.full_like(m_ref, -jnp.inf)
        acc_ref[...] = jnp.zeros_like(acc_ref)
        den_ref[...] = jnp.zeros_like(den_ref)

    def step(kinds):
        q, qa = q_ref[...], qa_ref[...]
        lane_q = lax.broadcasted_iota(jnp.int32, (tq, LANES), 1)
        q_full = []
        for h in range(2):
            hmask = (lane_q < HEAD_DIM) if h == 0 else (lane_q >= HEAD_DIM)
            first = (2 * pair + h) * BIAS_PARTS
            in_group = lambda lo: jnp.logical_and(lane_q >= lo, lane_q < lo + BIAS_PARTS)
            aug = jnp.logical_or(in_group(first), in_group(first + BIAS_LANES))
            q_full.append(jnp.concatenate([jnp.where(hmask, q, jnp.zeros_like(q)),
                                           jnp.where(aug, qa, jnp.zeros_like(qa))], axis=1))
        subs = [slice(b * tq, (b + 1) * tq) for b in range(len(kinds))]
        k_full = [jnp.concatenate([k_ref[sl, :], ka_ref[sl, :]], axis=1) for sl in subs]
        v_t = [vt_ref[:, sl] for sl in subs]
        scores = lambda b: [_mm([k_full[b]], [q_full[h]], "nt") for h in range(2)]
        s = [scores(b) for b in range(min(AHEAD, len(kinds)))]
        causal = lax.broadcasted_iota(jnp.int32, (tq, tq), 0) <= lax.broadcasted_iota(jnp.int32, (tq, tq), 1)
        m = [m_ref[h] for h in range(2)]
        acc = [acc_ref[h] for h in range(2)]
        den = [den_ref[h] for h in range(2)]
        for b, diagonal in enumerate(kinds):
            if b + AHEAD < len(kinds):
                s.append(scores(b + AHEAD))
            sb = [jnp.where(causal, x, -jnp.inf) for x in s[b]] if diagonal else s[b]
            if running_max:
                m_new = [jnp.maximum(m[h], jnp.max(sb[h], axis=0, keepdims=True)) for h in range(2)]
                p = [jnp.exp2(sb[h] - m_new[h]) for h in range(2)]
                alpha = [jnp.exp2(m[h] - m_new[h]) for h in range(2)]
                acc = [alpha[h] * acc[h] for h in range(2)]
                den = [alpha[h] * den[h] for h in range(2)]
                m = m_new
            else:
                p = [jnp.exp2(sb[h]) for h in range(2)]
            den = [den[h] + jnp.sum(p[h].reshape(tq // SUBLANES, SUBLANES, tq), axis=0) for h in range(2)]
            pv = [_mm([v_t[b]], _bf(p[h])) for h in range(2)]
            acc = [acc[h] + pv[h] for h in range(2)]
        for h in range(2):
            acc_ref[h] = acc[h]
            den_ref[h] = den[h]
            if running_max:
                m_ref[h] = m[h]

    def finish():
        row = lax.broadcasted_iota(jnp.int32, (LANES, tq), 0)
        o = [acc_ref[h] / jnp.sum(den_ref[h], axis=0, keepdims=True) for h in range(2)]
        o_ref[...] = jnp.where(row < HEAD_DIM, o[0], o[1]).T.astype(o_ref.dtype)

    @pl.when(kind == 0)
    def _():
        step((False,) * KEY_BLOCKS)

    for n_full in range(KEY_BLOCKS):
        @pl.when(kind == 1 + n_full)
        def _(n_full=n_full):
            step((False,) * n_full + (True,))
            finish()


def _fox_attn(q, qa, k, ka, vt, tq, running_max):
    m = q.shape[0]
    tq = min(tq, m)
    nq = m // tq
    g = KEY_BLOCKS
    assert nq % g == 0, "key blocks are fetched in groups"
    steps = []
    for i in range(nq):
        steps += [(i, j, 0) for j in range(i // g)] + [(i, i // g, 1 + i % g)]
    qi_tab, kb_tab, kind_tab = (jnp.asarray([st[c] for st in steps], jnp.int32) for c in range(3))
    qspec = pl.BlockSpec((tq, LANES), lambda p, t, qi, kb, kind: (qi[t], p))
    qaspec = pl.BlockSpec((tq, LANES), lambda p, t, qi, kb, kind: (qi[t], 0))
    kspec = pl.BlockSpec((g * tq, LANES), lambda p, t, qi, kb, kind: (kb[t], p))
    kaspec = pl.BlockSpec((g * tq, LANES), lambda p, t, qi, kb, kind: (kb[t], 0))
    vtspec = pl.BlockSpec((LANES, g * tq), lambda p, t, qi, kb, kind: (p, kb[t]))
    return pl.pallas_call(
        functools.partial(_fox_attn_kernel, running_max=running_max),
        grid_spec=pltpu.PrefetchScalarGridSpec(
            num_scalar_prefetch=3,
            grid=(N_PAIRS, len(steps)),
            in_specs=[qspec, qaspec, kspec, kaspec, vtspec],
            out_specs=qspec,
            scratch_shapes=[pltpu.VMEM((2, 1, tq), F32), pltpu.VMEM((2, LANES, tq), F32),
                            pltpu.VMEM((2, SUBLANES, tq), F32)],
        ),
        out_shape=jax.ShapeDtypeStruct((m, WIDTH), BF16),
        compiler_params=_params(("parallel", "arbitrary")),
    )(qi_tab, kb_tab, kind_tab, q, qa, k, ka, vt)


def _mem_kv_kernel(mem_ref, g_ref, w_ref, o_ref):
    x = mem_ref[...]
    ms = jnp.mean(x * x, axis=-1, keepdims=True)
    xn = (x * lax.rsqrt(ms + NORM_EPS) * g_ref[...]).astype(BF16)
    o_ref[...] = _dot(xn, w_ref[...]).astype(o_ref.dtype)


def _mem_kv(mem, ln_mem, w_ckv):
    n, d = mem.shape
    ins = [mem, ln_mem.reshape(1, d), w_ckv]
    return pl.pallas_call(
        _mem_kv_kernel,
        grid=(1,),
        in_specs=[pl.BlockSpec(a.shape, lambda i: (0, 0)) for a in ins],
        out_specs=pl.BlockSpec((n, w_ckv.shape[1]), lambda i: (0, 0)),
        out_shape=jax.ShapeDtypeStruct((n, w_ckv.shape[1]), BF16),
        compiler_params=_params(("arbitrary",)),
    )(*ins)


def _post_kernel(ya_ref, yb_ref, pg_ref, x_ref, wa_ref, wb_ref, wo_ref, lnc_ref, wcq_ref, kv_ref, wco_ref, o_ref):
    d = x_ref.shape[1]
    xw = X_HEADS * X_HEAD_DIM
    ga = pg_ref[:, 0:d].astype(F32)
    gb = pg_ref[:, d:2 * d].astype(F32)
    merged = ga * _dot(ya_ref[...], wa_ref[...]) + gb * _dot(yb_ref[...], wb_ref[...])
    h = x_ref[...] + _dot(merged.astype(BF16), wo_ref[...])

    ms = jnp.mean(h * h, axis=-1, keepdims=True)
    hn = (h * lax.rsqrt(ms + NORM_EPS) * lnc_ref[...]).astype(BF16)
    q = _dot(hn, wcq_ref[...])
    outs = []
    for hd in range(X_HEADS):
        cols = slice(hd * X_HEAD_DIM, (hd + 1) * X_HEAD_DIM)
        vcols = slice(xw + hd * X_HEAD_DIM, xw + (hd + 1) * X_HEAD_DIM)
        s = _dot_t(q[:, cols].astype(BF16), kv_ref[:, cols]) * (X_HEAD_DIM ** -0.5)
        s = s - jnp.max(s, axis=1, keepdims=True)
        e = jnp.exp(s)
        pr = e / jnp.sum(e, axis=1, keepdims=True)
        outs.append(_dot(pr.astype(BF16), kv_ref[:, vcols]))
    o = jnp.concatenate(outs, axis=1).astype(BF16)
    o_ref[...] = h + _dot(o, wco_ref[...])


def _post(ya, yb, pg, x, wa, wb, wo, ln_cross, wcq, kv, wco, tm):
    m, d = x.shape
    tm = min(tm, m)
    lnc = ln_cross.reshape(1, d)
    rows = lambda a: pl.BlockSpec((tm, a.shape[1]), lambda i: (i, 0))
    const = lambda a: pl.BlockSpec(a.shape, lambda i: (0, 0), pipeline_mode=pl.Buffered(1))
    return pl.pallas_call(
        _post_kernel,
        grid=(m // tm,),
        in_specs=[rows(ya), rows(yb), rows(pg), rows(x)] + [const(a) for a in (wa, wb, wo, lnc, wcq, kv, wco)],
        out_specs=pl.BlockSpec((tm, d), lambda i: (i, 0)),
        out_shape=jax.ShapeDtypeStruct((m, d), F32),
        compiler_params=_params(("parallel",)),
    )(ya, yb, pg, x, wa, wb, wo, lnc, wcq, kv, wco)


def _mlp_kernel(h_ref, g_ref, wu_ref, wd_ref, gf_ref, o_ref, xn_ref):
    f = pl.program_id(1)

    @pl.when(f == 0)
    def _():
        h = h_ref[...]
        ms = jnp.mean(h * h, axis=-1, keepdims=True)
        xn_ref[...] = (h * lax.rsqrt(ms + NORM_EPS) * g_ref[...]).astype(xn_ref.dtype)
        o_ref[...] = jnp.zeros_like(o_ref)

    u = jnp.maximum(_dot(xn_ref[...], wu_ref[...]), 0.0)
    o_ref[...] += _dot((u * u).astype(BF16), wd_ref[...])

    @pl.when(f == pl.num_programs(1) - 1)
    def _():
        h = h_ref[...] + o_ref[...]
        ms = jnp.mean(h * h, axis=-1, keepdims=True)
        o_ref[...] = h * lax.rsqrt(ms + NORM_EPS) * gf_ref[...]


def _mlp(h, ln_mlp, w_up, w_down, ln_final, tm, tf):
    m, d = h.shape
    dff = w_up.shape[1]
    tm, tf = min(tm, m), min(tf, dff)
    return pl.pallas_call(
        _mlp_kernel,
        grid=(m // tm, dff // tf),
        in_specs=[pl.BlockSpec((tm, d), lambda i, f: (i, 0)),
                  pl.BlockSpec((1, d), lambda i, f: (0, 0)),
                  pl.BlockSpec((d, tf), lambda i, f: (0, f)),
                  pl.BlockSpec((tf, d), lambda i, f: (f, 0)),
                  pl.BlockSpec((1, d), lambda i, f: (0, 0))],
        out_specs=pl.BlockSpec((tm, d), lambda i, f: (i, 0)),
        out_shape=jax.ShapeDtypeStruct((m, d), F32),
        scratch_shapes=[pltpu.VMEM((tm, d), BF16)],
        compiler_params=_params(("parallel", "arbitrary")),
    )(h, ln_mlp.reshape(1, d), w_up, w_down, ln_final.reshape(1, d))


def _pad_rows(a, height):
    return jnp.pad(a, ((0, height - a.shape[0]), (0, 0)))


def _layer(h, mem, ln_mix, w_in, rw_mu, rw_w0, rw_w_up, rw_a0, rw_a_up, rw_g_up, rw_k_k, rw_k_a, rw_r_k,
           rw_ln_w, rw_ln_b, fox_b_f, fox_q_norm, fox_k_norm, w_proj_a, w_proj_b, w_out, ln_cross, ln_mem,
           w_cq, w_ckv, w_co, ln_mlp, w_up, w_down, ln_final):
    w_t = w_in.T
    fox_row0 = RW_COLS
    gate_row0 = RW_COLS + 3 * WIDTH + N_HEADS
    mu_p = jnp.pad(rw_mu, (0, RW_PAD - RW_COLS))
    bf_rep = jnp.pad(jnp.tile(jnp.repeat(fox_b_f, BIAS_PARTS), 2), (0, LANES - 2 * BIAS_LANES)).reshape(1, LANES)
    wup_p = _pad_rows(rw_w_up, LORA_PAD).astype(BF16)
    aup_p = jnp.pad(rw_a_up, ((DECAY_LORA, 0), (0, 0))).astype(BF16)
    gup_p = _pad_rows(rw_g_up, GATE_PAD).astype(BF16)

    xn = _rmsnorm_bf16(h, ln_mix, 512)
    p_rw = _project(xn, w_t, 0, RW_PAD, 1024, 1792)
    p_fox = _project(xn, w_t, fox_row0, FOX_PAD, 1024, 1792)
    gates = _project(xn, w_t, gate_row0, w_t.shape[0] - gate_row0, 1024, 2048, BF16, gate=True)

    y_a = _rw_branch(p_rw, mu_p, rw_w0, wup_p, rw_a0, aup_p, gup_p, rw_k_k, rw_k_a, rw_r_k.reshape(-1), rw_ln_w,
                     rw_ln_b, 8 * CHUNK)

    fq, fk, fvt, fqa, fka, fbound = _fox_prep(p_fox, fox_q_norm, fox_k_norm, bf_rep, 512)
    y_b = lax.cond(jnp.max(fbound) < SAFE_LOGIT_BOUND,
                   functools.partial(_fox_attn, tq=512, running_max=False),
                   functools.partial(_fox_attn, tq=512, running_max=True), fq, fqa, fk, fka, fvt)

    kv = _mem_kv(mem, ln_mem, w_ckv.astype(BF16))
    h = _post(y_a, y_b, gates, h, w_proj_a.astype(BF16), w_proj_b.astype(BF16), w_out.astype(BF16), ln_cross,
              w_cq.astype(BF16), kv, w_co.astype(BF16), 512)
    return _mlp(h, ln_mlp, w_up.astype(BF16), w_down.astype(BF16), ln_final, 512, 1024)


def kernel(x, mem, ln_mix, w_in, rw_mu, rw_w0, rw_w_up, rw_a0, rw_a_up, rw_g_up, rw_k_k, rw_k_a, rw_r_k, rw_ln_w, rw_ln_b, fox_b_f, fox_q_norm, fox_k_norm, w_proj_a, w_proj_b, w_out, ln_cross, ln_mem, w_cq, w_ckv, w_co, ln_mlp, w_up, w_down, ln_final):
    b, s, d = x.shape
    assert b == 1 and ln_mix.shape[0] == 1, "single sequence, single layer"
    out = _layer(x[0], mem[0], ln_mix[0], w_in[0], rw_mu[0], rw_w0[0], rw_w_up[0], rw_a0[0], rw_a_up[0], rw_g_up[0],
                 rw_k_k[0], rw_k_a[0], rw_r_k[0], rw_ln_w[0], rw_ln_b[0], fox_b_f[0], fox_q_norm[0], fox_k_norm[0],
                 w_proj_a[0], w_proj_b[0], w_out[0], ln_cross[0], ln_mem[0], w_cq[0], w_ckv[0], w_co[0],
                 ln_mlp[0], w_up[0], w_down[0], ln_final)
    return out.reshape(b, s, d)
```

```python
import functools

import numpy as np

import jax
import jax.numpy as jnp
from jax import lax
from jax.experimental import pallas as pl
from jax.experimental.pallas import tpu as pltpu

F32 = jnp.float32
BF16 = jnp.bfloat16

LANES = 128
SUBLANES = 8
MXU_COLS = 256
HEAD_DIM = 64
N_HEADS = 16
WIDTH = N_HEADS * HEAD_DIM
N_PAIRS = WIDTH // LANES
DECAY_LORA, ICLR_LORA, GATE_LORA = 64, 64, 160
LORA_PAD = 128
GATE_PAD = 256
RW_COLS = 3 * WIDTH + DECAY_LORA + ICLR_LORA + GATE_LORA
RW_PAD = 3584
FZ_PAD = 512
FOX_PAD = 3 * WIDTH + FZ_PAD
BOUND_SLACK = 1.02
SAFE_LOGIT_BOUND = 50.0
BIAS_PARTS = 3
BIAS_LANES = BIAS_PARTS * N_HEADS
LOG2E = 1.4426950408889634
KEY_BLOCKS = 8
AHEAD = KEY_BLOCKS
X_HEADS, X_HEAD_DIM = 4, 128
NORM_EPS = 1e-5
GN_EPS = 64e-5
CHUNK = 64
DECAY_SCALE = -0.6065306597126334
SUB = 16
VMEM_LIMIT = 56 * 1024 * 1024


def _params(sem, vmem=VMEM_LIMIT):
    return pltpu.CompilerParams(dimension_semantics=sem, vmem_limit_bytes=vmem)


def _dot(a, b):
    return jnp.dot(a, b, preferred_element_type=F32)


def _dot_t(a, b):
    return lax.dot_general(a, b, (((1,), (1,)), ((), ())), preferred_element_type=F32)


def _split(x, terms):
    parts, rem = [], x
    for i in range(terms):
        part = rem.astype(BF16)
        parts.append(part)
        if i + 1 < terms:
            rem = rem - part.astype(F32)
    return parts


_DIMS = {"nn": (((1,), (0,)), ((), ())), "nt": (((1,), (1,)), ((), ())), "tn": (((0,), (0,)), ((), ()))}


def _mm(a_parts, b_parts, kind="nn"):
    order = max(len(a_parts), len(b_parts))
    acc = None
    for i, a in enumerate(a_parts):
        for j, b in enumerate(b_parts):
            if i + j < order:
                term = lax.dot_general(a, b, _DIMS[kind], preferred_element_type=F32)
                acc = term if acc is None else acc + term
    return acc


def _bf(x):
    return [x.astype(BF16)]


def _softplus(x):
    return jnp.maximum(x, 0.0) + jnp.log(1.0 + jnp.exp(-jnp.abs(x)))


def _sigmoid(x):
    return 1.0 / (1.0 + jnp.exp(-x))


def _head_ones():
    r = lax.broadcasted_iota(jnp.int32, (LANES, LANES), 0) // HEAD_DIM
    c = lax.broadcasted_iota(jnp.int32, (LANES, LANES), 1) // HEAD_DIM
    return (r == c).astype(F32)


def _stacked_headsum(xs, ones, terms):
    n = xs[0].shape[0]
    out = _mm(_split(jnp.concatenate(xs, axis=0), terms), ones)
    return [out[i * n:(i + 1) * n] for i in range(len(xs))]


def _headsum(x, ones_f32, terms=2):
    chunks = [x[:, c * LANES:(c + 1) * LANES] for c in range(x.shape[1] // LANES)]
    return jnp.concatenate(_stacked_headsum(chunks, [ones_f32.astype(BF16)], terms), axis=1)


def _rmsnorm_kernel(x_ref, g_ref, o_ref):
    x = x_ref[...]
    ms = jnp.mean(x * x, axis=-1, keepdims=True)
    o_ref[...] = (x * lax.rsqrt(ms + NORM_EPS) * g_ref[...]).astype(o_ref.dtype)


def _rmsnorm_bf16(x, g, tm):
    m, d = x.shape
    tm = min(tm, m)
    return pl.pallas_call(
        _rmsnorm_kernel,
        grid=(m // tm,),
        in_specs=[pl.BlockSpec((tm, d), lambda i: (i, 0)), pl.BlockSpec((1, d), lambda i: (0, 0))],
        out_specs=pl.BlockSpec((tm, d), lambda i: (i, 0)),
        out_shape=jax.ShapeDtypeStruct((m, d), BF16),
        compiler_params=_params(("parallel",)),
    )(x, g.reshape(1, d))


def _proj_kernel(a_ref, wt_hbm, o_ref, w_f32, w_bf16, sem, *, gate, row0):
    j, i = pl.program_id(0), pl.program_id(1)
    bn = w_f32.shape[0]

    def window_copy(block):
        start = pl.multiple_of(row0 + block * bn, SUBLANES)
        return pltpu.make_async_copy(wt_hbm.at[pl.ds(start, bn), :], w_f32, sem)

    @pl.when(jnp.logical_and(i == 0, j == 0))
    def _():
        window_copy(0).start()

    @pl.when(i == 0)
    def _():
        window_copy(j).wait()
        for c0 in range(0, bn, MXU_COLS):
            w_bf16[:, c0:c0 + MXU_COLS] = w_f32[c0:c0 + MXU_COLS, :].T.astype(BF16)

    @pl.when(jnp.logical_and(i == 1, j + 1 < pl.num_programs(0)))
    def _():
        window_copy(j + 1).start()

    acc = _dot(a_ref[...], w_bf16[...])
    o_ref[...] = (_sigmoid(acc) if gate else acc).astype(o_ref.dtype)


def _project(a, w_t, row0, n, bm, bn, out_dtype=F32, gate=False):
    m, k = a.shape
    bm = min(bm, m)
    assert row0 % SUBLANES == 0 and n % bn == 0 and bn % MXU_COLS == 0 and m % bm == 0 and row0 + n <= w_t.shape[0]
    assert m // bm >= 2, "the next weight window is started at the second row block"
    return pl.pallas_call(
        functools.partial(_proj_kernel, gate=gate, row0=row0),
        grid=(n // bn, m // bm),
        in_specs=[pl.BlockSpec((bm, k), lambda j, i: (i, 0)), pl.BlockSpec(memory_space=pl.ANY)],
        out_specs=pl.BlockSpec((bm, bn), lambda j, i: (i, j)),
        out_shape=jax.ShapeDtypeStruct((m, n), out_dtype),
        scratch_shapes=[pltpu.VMEM((bn, k), F32), pltpu.VMEM((k, bn), BF16), pltpu.SemaphoreType.DMA(())],
        compiler_params=_params(("arbitrary", "arbitrary")),
    )(a, w_t)


def _rw_inputs(p, prev_row, mu, w0, wup, a0, aup, gup, k_k, k_a, ones):
    rows = lax.broadcasted_iota(jnp.int32, p.shape, 0)
    prev = jnp.where(rows == 0, prev_row, pltpu.roll(p, 1, 0))
    ps = p + (prev - p) * mu
    w = WIDTH
    r, k, v = ps[:, 0:w], ps[:, w:2 * w], ps[:, 2 * w:3 * w]
    o = 3 * w
    lora = ps[:, o:o + LORA_PAD]
    gd = ps[:, o + LORA_PAD:o + LORA_PAD + GATE_PAD]
    z = w0 + _mm(_bf(jnp.tanh(lora)), [wup])
    lw = DECAY_SCALE * _sigmoid(z)
    iclr = _sigmoid(a0 + _mm(_bf(lora), [aup]))
    gate = _mm(_bf(_sigmoid(gd)), [gup])
    kk = k * k_k
    kk2 = kk * kk
    ss = _stacked_headsum([kk2[:, c * LANES:(c + 1) * LANES] for c in range(N_PAIRS)], ones, 2)
    kk = kk * lax.rsqrt(jnp.maximum(jnp.concatenate(ss, axis=1), 1e-24))
    return r, k * (1.0 + (iclr - 1.0) * k_a), v, lw, -kk, kk * iclr, gate


def _pair_diag(x, head0):
    xb = x.astype(BF16)
    zero = jnp.zeros_like(xb)
    return [jnp.concatenate([jnp.where(head0, xb, zero), jnp.where(head0, zero, xb)], axis=0)]


def _unit_lower_inverse(a_list, sub_mask, eye, head0):
    c = CHUNK
    d = [jnp.where(sub_mask, a, 0.0) for a in a_list]
    e = [a - x for a, x in zip(a_list, d)]
    p = [eye + x for x in d]
    diag = lambda xs: [_pair_diag(x, head0) for x in xs]
    stack = lambda xs, ys: [_bf(jnp.concatenate([x, y], axis=0)) for x, y in zip(xs, ys)]
    dp = [_mm(_bf(x), y) for x, y in zip(d, diag(d))]
    for _ in range(SUB.bit_length() - 3):
        both = [_mm(x, y) for x, y in zip(stack(dp, p), diag(dp))]
        dp = [x[:c] for x in both]
        p = [x + y[c:] for x, y in zip(p, both)]
    p = [x + _mm(_bf(x), y) for x, y in zip(p, diag(dp))]
    f = [_mm(_bf(x), y) for x, y in zip(p, diag(e))]
    g = [eye + x for x in f]
    fp = [_mm(_bf(x), y) for x, y in zip(f, diag(f))]
    for _ in range((CHUNK // SUB).bit_length() - 3):
        both = [_mm(x, y) for x, y in zip(stack(fp, g), diag(fp))]
        fp = [x[:c] for x in both]
        g = [x + y[c:] for x, y in zip(g, both)]
    g = [x + _mm(_bf(x), y) for x, y in zip(g, diag(fp))]
    return [_mm(_bf(x), y) for x, y in zip(g, diag(p))]


def _rw_kernel(p_ref, mu_ref, w0_ref, wup_ref, a0_ref, aup_ref, gup_ref, kkw_ref, ka_ref, rk_ref, lnw_ref, lnb_ref,
               o_ref, h_ref, carry_ref):
    c = CHUNK
    t_tile = p_ref.shape[0]
    n_chunks = t_tile // c

    @pl.when(pl.program_id(0) == 0)
    def _():
        h_ref[...] = jnp.zeros_like(h_ref)
        carry_ref[...] = jnp.zeros_like(carry_ref)

    lane = lax.broadcasted_iota(jnp.int32, (c, LANES), 1)
    head0 = lane < HEAD_DIM
    ri = lax.broadcasted_iota(jnp.int32, (c, LANES), 0)
    ci = jnp.bitwise_and(lane, HEAD_DIM - 1)
    strict = ri > ci
    incl = ri >= ci
    eye = (ri == ci).astype(F32)
    sub_mask = (ri // SUB) == (ci // SUB)
    tril = _bf(lax.broadcasted_iota(jnp.int32, (c, c), 0) >= lax.broadcasted_iota(jnp.int32, (c, c), 1))
    ones_f = _head_ones()
    ones = _bf(ones_f)
    pair_mask = ones_f > 0.5
    diag = lambda x: _pair_diag(x, head0)
    units = [(s, p) for s in range(n_chunks) for p in range(N_PAIRS)]
    rows = lambda s: slice(s * c, (s + 1) * c)
    cols = lambda p: slice(p * LANES, (p + 1) * LANES)

    p_tile = p_ref[...]
    r_all, k_all, v_all, lw_all, a_all, b_all, gate_all = _rw_inputs(
        p_tile, carry_ref[...], mu_ref[...], w0_ref[...], wup_ref[...], a0_ref[...], aup_ref[...], gup_ref[...],
        kkw_ref[...], ka_ref[...], ones)
    carry_ref[...] = p_tile[t_tile - 1:t_tile, :]

    pre = []
    for s in range(n_chunks):
        lw, k, b = lw_all[rows(s), :], k_all[rows(s), :], b_all[rows(s), :]
        cl = _mm(tril, _split(lw, 3))
        cl_end = cl[c - 1:c, :]
        e_neg = jnp.exp(-cl)
        e_end = jnp.exp(cl_end - cl)
        pre.append(dict(rt=r_all[rows(s), :] * jnp.exp(cl), at=a_all[rows(s), :] * jnp.exp(cl - lw),
                        bt=b * e_neg, kt=k * e_neg, b_end=b * e_end, k_end=k * e_end, decay_end=jnp.exp(cl_end)))
    get = lambda name: [pre[s][name][:, cols(p)] for s, p in units]
    tile = lambda x: [x[rows(s), cols(p)] for s, p in units]
    r, k, v, gate = tile(r_all), tile(k_all), tile(v_all), tile(gate_all)
    rt, at = get("rt"), get("at")
    stack = lambda xs, ys: [jnp.concatenate([x, y], axis=0) for x, y in zip(xs, ys)]
    bk_end = [_bf(x) for x in stack(get("b_end"), get("k_end"))]
    v_diag = [diag(x) for x in v]
    mask2 = jnp.concatenate([strict, incl], axis=0)

    lhs = [_bf(x) for x in stack(at, rt)]
    sb = [_mm(x, diag(y), "nt") for x, y in zip(lhs, get("bt"))]
    sk = [_mm(x, diag(y), "nt") for x, y in zip(lhs, get("kt"))]
    skv = [_mm(_bf(jnp.where(mask2, x, 0.0)), y) for x, y in zip(sk, v_diag)]
    rb = [_bf(jnp.where(incl, x[c:], 0.0)) for x in sb]
    t_inv = [_bf(x) for x in _unit_lower_inverse([jnp.where(strict, x[:c], 0.0) for x in sb], sub_mask, eye, head0)]
    w = [_mm(x, diag(y)) for x, y in zip(t_inv, at)]
    tk = [_mm(x, diag(y[:c])) for x, y in zip(t_inv, skv)]
    s0_lhs = [_bf(x) for x in stack(w, rt)]

    ht = [h_ref[p] for p in range(N_PAIRS)]
    u, rs0 = [], []
    for s in range(n_chunks):
        i0 = s * N_PAIRS
        s0 = [_mm(s0_lhs[i0 + p], _bf(ht[p]), "nt") for p in range(N_PAIRS)]
        u += [s0[p][:c] + tk[i0 + p] for p in range(N_PAIRS)]
        rs0 += [s0[p][c:] for p in range(N_PAIRS)]
        upd = [_mm(_bf(jnp.concatenate([u[i0 + p], v[i0 + p]], axis=0)), bk_end[i0 + p], "tn") for p in range(N_PAIRS)]
        ht = [ht[p] * pre[s]["decay_end"][:, cols(p)] + jnp.where(pair_mask, upd[p], 0.0) for p in range(N_PAIRS)]
    for p in range(N_PAIRS):
        h_ref[p] = ht[p]

    y = [a + _mm(x, diag(z)) + b[c:] for a, x, z, b in zip(rs0, rb, u, skv)]
    rkr = [r[i] * k[i] * rk_ref[:, cols(p)] for i, (s, p) in enumerate(units)]
    sums = _stacked_headsum(y + rkr, ones, 2)
    dlt = [a - b * (1.0 / HEAD_DIM) for a, b in zip(y, sums[:len(units)])]
    var = _stacked_headsum([x * x for x in dlt], ones, 2)
    for i, (s, p) in enumerate(units):
        yn = dlt[i] * lax.rsqrt(var[i] * (1.0 / HEAD_DIM) + GN_EPS) * lnw_ref[:, cols(p)] + lnb_ref[:, cols(p)]
        o_ref[rows(s), cols(p)] = ((yn + sums[len(units) + i] * v[i]) * gate[i]).astype(o_ref.dtype)


def _rw_branch(p_rw, mu_p, w0, wup_p, a0, aup_p, gup_p, k_k, k_a, r_k, ln_w, ln_b, t_tile):
    m = p_rw.shape[0]
    t_tile = min(t_tile, m)
    row = lambda a: a.reshape(1, -1)
    full = lambda a: pl.BlockSpec(a.shape, lambda t: (0, 0))
    params = [row(mu_p), row(w0), wup_p, row(a0), aup_p, gup_p, row(k_k), row(k_a), row(r_k), row(ln_w), row(ln_b)]
    return pl.pallas_call(
        _rw_kernel,
        grid=(m // t_tile,),
        in_specs=[pl.BlockSpec((t_tile, RW_PAD), lambda t: (t, 0))] + [full(a) for a in params],
        out_specs=pl.BlockSpec((t_tile, WIDTH), lambda t: (t, 0)),
        out_shape=jax.ShapeDtypeStruct((m, WIDTH), BF16),
        scratch_shapes=[pltpu.VMEM((N_PAIRS, LANES, LANES), F32), pltpu.VMEM((1, RW_PAD), F32)],
        compiler_params=_params(("arbitrary",)),
    )(p_rw, *params)


def _fox_prep_kernel(p_ref, qn_ref, kn_ref, bf_ref, part_ref, sel_ref, selz_ref,
                     q_out, k_out, vt_out, qa_out, ka_out, bound_out, carry_ref, kmax_ref, bmax_ref):
    tm = p_ref.shape[0]

    @pl.when(pl.program_id(0) == 0)
    def _():
        carry_ref[...] = jnp.zeros_like(carry_ref)
        kmax_ref[...] = jnp.zeros_like(kmax_ref)
        bmax_ref[...] = jnp.zeros_like(bmax_ref)

    w = WIDTH
    ones = _head_ones()
    q, k = p_ref[:, 0:w], p_ref[:, w:2 * w]
    q = q * lax.rsqrt(_headsum(q * q, ones) * (1.0 / HEAD_DIM) + NORM_EPS) * qn_ref[...]
    k = k * lax.rsqrt(_headsum(k * k, ones) * (1.0 / HEAD_DIM) + NORM_EPS) * kn_ref[...]
    qb = (q * (HEAD_DIM ** -0.5 * LOG2E)).astype(BF16)
    kb = k.astype(BF16)
    q_out[...] = qb
    k_out[...] = kb
    vt_out[...] = p_ref[:, 2 * w:3 * w].T.astype(vt_out.dtype)

    qf, kf = qb.astype(F32), kb.astype(F32)
    k_sq = _headsum(kf * kf, ones, 1)
    kmax_sq = jnp.maximum(kmax_ref[...], jnp.max(k_sq, axis=0, keepdims=True))
    kmax_ref[...] = kmax_sq
    bound_sq = _headsum(qf * qf, ones, 1) * kmax_sq
    bsel = jnp.sqrt(_mm(_bf(bound_sq), [sel_ref[...]])) * BOUND_SLACK

    fz = _mm(_split(p_ref[:, 3 * w:3 * w + LANES], 3), [selz_ref[...]])
    logf = -_softplus(-(fz + bf_ref[...]))
    ri = lax.broadcasted_iota(jnp.int32, (tm, tm), 0)
    ci = lax.broadcasted_iota(jnp.int32, (tm, tm), 1)
    cum = _mm(_bf(ri >= ci), _split(logf, 3)) + carry_ref[...]
    carry_ref[...] = cum[tm - 1:tm, :]

    part = jnp.broadcast_to(part_ref[...], (tm, LANES))

    def pick(x):
        parts = [t.astype(F32) for t in _split(x, BIAS_PARTS)]
        return jnp.where(part == 0, parts[0], jnp.where(part == 1, parts[1], parts[2]))

    lane = lax.broadcasted_iota(jnp.int32, (tm, LANES), 1)
    low = lane < BIAS_LANES
    high = jnp.logical_and(lane >= BIAS_LANES, lane < 2 * BIAS_LANES)
    ka_out[...] = jnp.where(low, pick(cum * (-LOG2E)), jnp.where(high, 1.0, 0.0)).astype(ka_out.dtype)
    qa_out[...] = jnp.where(low, 1.0, jnp.where(high, pick(cum * LOG2E - bsel), 0.0)).astype(qa_out.dtype)
    bmax = jnp.maximum(bmax_ref[...], jnp.max(bsel, axis=0, keepdims=True))
    bmax_ref[...] = bmax
    bound_out[...] = jnp.broadcast_to(bmax, bound_out.shape)


def _fox_prep(p_fox, q_norm, k_norm, bf_rep, tm):
    m = p_fox.shape[0]
    tm = min(tm, m)
    qn = jnp.tile(q_norm, N_HEADS).reshape(1, WIDTH)
    kn = jnp.tile(k_norm, N_HEADS).reshape(1, WIDTH)
    part = (jnp.arange(LANES, dtype=jnp.int32) % BIAS_PARTS).reshape(1, LANES)
    heads = np.arange(N_HEADS)
    sel = np.zeros((WIDTH, LANES), np.float32)
    for j in range(BIAS_PARTS):
        sel[heads * HEAD_DIM, BIAS_LANES + BIAS_PARTS * heads + j] = 1.0
    sel = jnp.asarray(sel, BF16)
    selz = np.zeros((LANES, LANES), np.float32)
    for j in range(BIAS_PARTS):
        for group in (0, BIAS_LANES):
            selz[heads, group + BIAS_PARTS * heads + j] = 1.0
    selz = jnp.asarray(selz, BF16)
    full = lambda a: pl.BlockSpec(a.shape, lambda i: (0, 0))
    rows = lambda width: pl.BlockSpec((tm, width), lambda i: (i, 0))
    act = jax.ShapeDtypeStruct((m, WIDTH), BF16)
    aug = jax.ShapeDtypeStruct((m, LANES), BF16)
    return pl.pallas_call(
        _fox_prep_kernel,
        grid=(m // tm,),
        in_specs=[rows(FOX_PAD), full(qn), full(kn), full(bf_rep), full(part), full(sel), full(selz)],
        out_specs=[rows(WIDTH), rows(WIDTH), pl.BlockSpec((WIDTH, tm), lambda i: (0, i)), rows(LANES), rows(LANES),
                   pl.BlockSpec((SUBLANES, LANES), lambda i: (0, 0))],
        out_shape=[act, act, jax.ShapeDtypeStruct((WIDTH, m), BF16), aug, aug, jax.ShapeDtypeStruct((SUBLANES, LANES), F32)],
        scratch_shapes=[pltpu.VMEM((1, LANES), F32), pltpu.VMEM((1, WIDTH), F32), pltpu.VMEM((1, LANES), F32)],
        compiler_params=_params(("arbitrary",)),
    )(p_fox, qn, kn, bf_rep, part, sel, selz)


def _fox_attn_kernel(qi_ref, kb_ref, kind_ref, q_ref, qa_ref, k_ref, ka_ref, vt_ref, o_ref, m_ref, acc_ref, den_ref, *,
                     running_max):
    tq = q_ref.shape[0]
    pair, step_id = pl.program_id(0), pl.program_id(1)
    kb, kind = kb_ref[step_id], kind_ref[step_id]

    @pl.when(kb == 0)
    def _():
        m_ref[...] = jnp.full_like(m_ref, -jnp.inf)
        acc_ref[...] = jnp.zeros_like(acc_ref)
        den_ref[...] = jnp.zeros_like(den_ref)

    def step(kinds):
        q, qa = q_ref[...], qa_ref[...]
        lane_q = lax.broadcasted_iota(jnp.int32, (tq, LANES), 1)
        q_full = []
        for h in range(2):
            hmask = (lane_q < HEAD_DIM) if h == 0 else (lane_q >= HEAD_DIM)
            first = (2 * pair + h) * BIAS_PARTS
            in_group = lambda lo: jnp.logical_and(lane_q >= lo, lane_q < lo + BIAS_PARTS)
            aug = jnp.logical_or(in_group(first), in_group(first + BIAS_LANES))
            q_full.append(jnp.concatenate([jnp.where(hmask, q, jnp.zeros_like(q)),
                                           jnp.where(aug, qa, jnp.zeros_like(qa))], axis=1))
        subs = [slice(b * tq, (b + 1) * tq) for b in range(len(kinds))]
        k_full = [jnp.concatenate([k_ref[sl, :], ka_ref[sl, :]], axis=1) for sl in subs]
        v_t = [vt_ref[:, sl] for sl in subs]
        scores = lambda b: [_mm([k_full[b]], [q_full[h]], "nt") for h in range(2)]
        s = [scores(b) for b in range(min(AHEAD, len(kinds)))]
        causal = lax.broadcasted_iota(jnp.int32, (tq, tq), 0) <= lax.broadcasted_iota(jnp.int32, (tq, tq), 1)
        m = [m_ref[h] for h in range(2)]
        acc = [acc_ref[h] for h in range(2)]
        den = [den_ref[h] for h in range(2)]
        for b, diagonal in enumerate(kinds):
            if b + AHEAD < len(kinds):
                s.append(scores(b + AHEAD))
            sb = [jnp.where(causal, x, -jnp.inf) for x in s[b]] if diagonal else s[b]
            if running_max:
                m_new = [jnp.maximum(m[h], jnp.max(sb[h], axis=0, keepdims=True)) for h in range(2)]
                p = [jnp.exp2(sb[h] - m_new[h]) for h in range(2)]
                alpha = [jnp.exp2(m[h] - m_new[h]) for h in range(2)]
                acc = [alpha[h] * acc[h] for h in range(2)]
                den = [alpha[h] * den[h] for h in range(2)]
                m = m_new
            else:
                p = [jnp.exp2(sb[h]) for h in range(2)]
            den = [den[h] + jnp.sum(p[h].reshape(tq // SUBLANES, SUBLANES, tq), axis=0) for h in range(2)]
            pv = [_mm([v_t[b]], _bf(p[h])) for h in range(2)]
            acc = [acc[h] + pv[h] for h in range(2)]
        for h in range(2):
            acc_ref[h] = acc[h]
            den_ref[h] = den[h]
            if running_max:
                m_ref[h] = m[h]

    def finish():
        row = lax.broadcasted_iota(jnp.int32, (LANES, tq), 0)
        o = [acc_ref[h] / jnp.sum(den_ref[h], axis=0, keepdims=True) for h in range(2)]
        o_ref[...] = jnp.where(row < HEAD_DIM, o[0], o[1]).T.astype(o_ref.dtype)

    @pl.when(kind == 0)
    def _():
        step((False,) * KEY_BLOCKS)

    for n_full in range(KEY_BLOCKS):
        @pl.when(kind == 1 + n_full)
        def _(n_full=n_full):
            step((False,) * n_full + (True,))
            finish()


def _fox_attn(q, qa, k, ka, vt, tq, running_max):
    m = q.shape[0]
    tq = min(tq, m)
    nq = m // tq
    g = KEY_BLOCKS
    assert nq % g == 0, "key blocks are fetched in groups"
    steps = []
    for i in range(nq):
        steps += [(i, j, 0) for j in range(i // g)] + [(i, i // g, 1 + i % g)]
    qi_tab, kb_tab, kind_tab = (jnp.asarray([st[c] for st in steps], jnp.int32) for c in range(3))
    qspec = pl.BlockSpec((tq, LANES), lambda p, t, qi, kb, kind: (qi[t], p))
    qaspec = pl.BlockSpec((tq, LANES), lambda p, t, qi, kb, kind: (qi[t], 0))
    kspec = pl.BlockSpec((g * tq, LANES), lambda p, t, qi, kb, kind: (kb[t], p))
    kaspec = pl.BlockSpec((g * tq, LANES), lambda p, t, qi, kb, kind: (kb[t], 0))
    vtspec = pl.BlockSpec((LANES, g * tq), lambda p, t, qi, kb, kind: (p, kb[t]))
    return pl.pallas_call(
        functools.partial(_fox_attn_kernel, running_max=running_max),
        grid_spec=pltpu.PrefetchScalarGridSpec(
            num_scalar_prefetch=3,
            grid=(N_PAIRS, len(steps)),
            in_specs=[qspec, qaspec, kspec, kaspec, vtspec],
            out_specs=qspec,
            scratch_shapes=[pltpu.VMEM((2, 1, tq), F32), pltpu.VMEM((2, LANES, tq), F32),
                            pltpu.VMEM((2, SUBLANES, tq), F32)],
        ),
        out_shape=jax.ShapeDtypeStruct((m, WIDTH), BF16),
        compiler_params=_params(("parallel", "arbitrary")),
    )(qi_tab, kb_tab, kind_tab, q, qa, k, ka, vt)


def _mem_kv_kernel(mem_ref, g_ref, w_ref, o_ref):
    x = mem_ref[...]
    ms = jnp.mean(x * x, axis=-1, keepdims=True)
    xn = (x * lax.rsqrt(ms + NORM_EPS) * g_ref[...]).astype(BF16)
    o_ref[...] = _dot(xn, w_ref[...]).astype(o_ref.dtype)


def _mem_kv(mem, ln_mem, w_ckv):
    n, d = mem.shape
    ins = [mem, ln_mem.reshape(1, d), w_ckv]
    return pl.pallas_call(
        _mem_kv_kernel,
        grid=(1,),
        in_specs=[pl.BlockSpec(a.shape, lambda i: (0, 0)) for a in ins],
        out_specs=pl.BlockSpec((n, w_ckv.shape[1]), lambda i: (0, 0)),
        out_shape=jax.ShapeDtypeStruct((n, w_ckv.shape[1]), BF16),
        compiler_params=_params(("arbitrary",)),
    )(*ins)


def _post_kernel(ya_ref, yb_ref, pg_ref, x_ref, wa_ref, wb_ref, wo_ref, lnc_ref, wcq_ref, kv_ref, wco_ref, o_ref):
    d = x_ref.shape[1]
    xw = X_HEADS * X_HEAD_DIM
    ga = pg_ref[:, 0:d].astype(F32)
    gb = pg_ref[:, d:2 * d].astype(F32)
    merged = ga * _dot(ya_ref[...], wa_ref[...]) + gb * _dot(yb_ref[...], wb_ref[...])
    h = x_ref[...] + _dot(merged.astype(BF16), wo_ref[...])

    ms = jnp.mean(h * h, axis=-1, keepdims=True)
    hn = (h * lax.rsqrt(ms + NORM_EPS) * lnc_ref[...]).astype(BF16)
    q = _dot(hn, wcq_ref[...])
    outs = []
    for hd in range(X_HEADS):
        cols = slice(hd * X_HEAD_DIM, (hd + 1) * X_HEAD_DIM)
        vcols = slice(xw + hd * X_HEAD_DIM, xw + (hd + 1) * X_HEAD_DIM)
        s = _dot_t(q[:, cols].astype(BF16), kv_ref[:, cols]) * (X_HEAD_DIM ** -0.5)
        s = s - jnp.max(s, axis=1, keepdims=True)
        e = jnp.exp(s)
        pr = e / jnp.sum(e, axis=1, keepdims=True)
        outs.append(_dot(pr.astype(BF16), kv_ref[:, vcols]))
    o = jnp.concatenate(outs, axis=1).astype(BF16)
    o_ref[...] = h + _dot(o, wco_ref[...])


def _post(ya, yb, pg, x, wa, wb, wo, ln_cross, wcq, kv, wco, tm):
    m, d = x.shape
    tm = min(tm, m)
    lnc = ln_cross.reshape(1, d)
    rows = lambda a: pl.BlockSpec((tm, a.shape[1]), lambda i: (i, 0))
    const = lambda a: pl.BlockSpec(a.shape, lambda i: (0, 0), pipeline_mode=pl.Buffered(1))
    return pl.pallas_call(
        _post_kernel,
        grid=(m // tm,),
        in_specs=[rows(ya), rows(yb), rows(pg), rows(x)] + [const(a) for a in (wa, wb, wo, lnc, wcq, kv, wco)],
        out_specs=pl.BlockSpec((tm, d), lambda i: (i, 0)),
        out_shape=jax.ShapeDtypeStruct((m, d), F32),
        compiler_params=_params(("parallel",)),
    )(ya, yb, pg, x, wa, wb, wo, lnc, wcq, kv, wco)


def _mlp_kernel(h_ref, g_ref, wu_ref, wd_ref, gf_ref, o_ref, xn_ref):
    f = pl.program_id(1)

    @pl.when(f == 0)
    def _():
        h = h_ref[...]
        ms = jnp.mean(h * h, axis=-1, keepdims=True)
        xn_ref[...] = (h * lax.rsqrt(ms + NORM_EPS) * g_ref[...]).astype(xn_ref.dtype)
        o_ref[...] = jnp.zeros_like(o_ref)

    u = jnp.maximum(_dot(xn_ref[...], wu_ref[...]), 0.0)
    o_ref[...] += _dot((u * u).astype(BF16), wd_ref[...])

    @pl.when(f == pl.num_programs(1) - 1)
    def _():
        h = h_ref[...] + o_ref[...]
        ms = jnp.mean(h * h, axis=-1, keepdims=True)
        o_ref[...] = h * lax.rsqrt(ms + NORM_EPS) * gf_ref[...]


def _mlp(h, ln_mlp, w_up, w_down, ln_final, tm, tf):
    m, d = h.shape
    dff = w_up.shape[1]
    tm, tf = min(tm, m), min(tf, dff)
    return pl.pallas_call(
        _mlp_kernel,
        grid=(m // tm, dff // tf),
        in_specs=[pl.BlockSpec((tm, d), lambda i, f: (i, 0)),
                  pl.BlockSpec((1, d), lambda i, f: (0, 0)),
                  pl.BlockSpec((d, tf), lambda i, f: (0, f)),
                  pl.BlockSpec((tf, d), lambda i, f: (f, 0)),
                  pl.BlockSpec((1, d), lambda i, f: (0, 0))],
        out_specs=pl.BlockSpec((tm, d), lambda i, f: (i, 0)),
        out_shape=jax.ShapeDtypeStruct((m, d), F32),
        scratch_shapes=[pltpu.VMEM((tm, d), BF16)],
        compiler_params=_params(("parallel", "arbitrary")),
    )(h, ln_mlp.reshape(1, d), w_up, w_down, ln_final.reshape(1, d))


def _pad_rows(a, height):
    return jnp.pad(a, ((0, height - a.shape[0]), (0, 0)))


def _layer(h, mem, ln_mix, w_in, rw_mu, rw_w0, rw_w_up, rw_a0, rw_a_up, rw_g_up, rw_k_k, rw_k_a, rw_r_k,
           rw_ln_w, rw_ln_b, fox_b_f, fox_q_norm, fox_k_norm, w_proj_a, w_proj_b, w_out, ln_cross, ln_mem,
           w_cq, w_ckv, w_co, ln_mlp, w_up, w_down, ln_final):
    w_t = w_in.T
    fox_row0 = RW_COLS
    gate_row0 = RW_COLS + 3 * WIDTH + N_HEADS
    mu_p = jnp.pad(rw_mu, (0, RW_PAD - RW_COLS))
    bf_rep = jnp.pad(jnp.tile(jnp.repeat(fox_b_f, BIAS_PARTS), 2), (0, LANES - 2 * BIAS_LANES)).reshape(1, LANES)
    wup_p = _pad_rows(rw_w_up, LORA_PAD).astype(BF16)
    aup_p = jnp.pad(rw_a_up, ((DECAY_LORA, 0), (0, 0))).astype(BF16)
    gup_p = _pad_rows(rw_g_up, GATE_PAD).astype(BF16)

    xn = _rmsnorm_bf16(h, ln_mix, 512)
    p_rw = _project(xn, w_t, 0, RW_PAD, 1024, 1792)
    p_fox = _project(xn, w_t, fox_row0, FOX_PAD, 1024, 1792)
    gates = _project(xn, w_t, gate_row0, w_t.shape[0] - gate_row0, 1024, 2048, BF16, gate=True)

    y_a = _rw_branch(p_rw, mu_p, rw_w0, wup_p, rw_a0, aup_p, gup_p, rw_k_k, rw_k_a, rw_r_k.reshape(-1), rw_ln_w,
                     rw_ln_b, 8 * CHUNK)

    fq, fk, fvt, fqa, fka, fbound = _fox_prep(p_fox, fox_q_norm, fox_k_norm, bf_rep, 512)
    y_b = lax.cond(jnp.max(fbound) < SAFE_LOGIT_BOUND,
                   functools.partial(_fox_attn, tq=512, running_max=False),
                   functools.partial(_fox_attn, tq=512, running_max=True), fq, fqa, fk, fka, fvt)

    kv = _mem_kv(mem, ln_mem, w_ckv.astype(BF16))
    h = _post(y_a, y_b, gates, h, w_proj_a.astype(BF16), w_proj_b.astype(BF16), w_out.astype(BF16), ln_cross,
              w_cq.astype(BF16), kv, w_co.astype(BF16), 512)
    return _mlp(h, ln_mlp, w_up.astype(BF16), w_down.astype(BF16), ln_final, 512, 1024)


def kernel(x, mem, ln_mix, w_in, rw_mu, rw_w0, rw_w_up, rw_a0, rw_a_up, rw_g_up, rw_k_k, rw_k_a, rw_r_k, rw_ln_w, rw_ln_b, fox_b_f, fox_q_norm, fox_k_norm, w_proj_a, w_proj_b, w_out, ln_cross, ln_mem, w_cq, w_ckv, w_co, ln_mlp, w_up, w_down, ln_final):
    b, s, d = x.shape
    assert b == 1 and ln_mix.shape[0] == 1, "single sequence, single layer"
    out = _layer(x[0], mem[0], ln_mix[0], w_in[0], rw_mu[0], rw_w0[0], rw_w_up[0], rw_a0[0], rw_a_up[0], rw_g_up[0],
                 rw_k_k[0], rw_k_a[0], rw_r_k[0], rw_ln_w[0], rw_ln_b[0], fox_b_f[0], fox_q_norm[0], fox_k_norm[0],
                 w_proj_a[0], w_proj_b[0], w_out[0], ln_cross[0], ln_mem[0], w_cq[0], w_ckv[0], w_co[0],
                 ln_mlp[0], w_up[0], w_down[0], ln_final)
    return out.reshape(b, s, d)
```
